```python
import math
import jax, jax.numpy as jnp
from jax import lax
import numpy as np

D_MODEL = 2048
BATCH = 4
SEQ = 4096
DEPTH = 2

ROPE_THETA = 10000.0
LN_EPS = 1e-5
RET_HEADS = 4
RET_HEAD_DIM = 256
RET_WIDTH = RET_HEADS * RET_HEAD_DIM
RET_CHUNK = 128
CONV_CHANNELS = D_MODEL - RET_WIDTH
CONV_TAPS = 31
EVEN_IN = 4 * RET_WIDTH + 2 * CONV_CHANNELS
DSA_HEADS = 16
DSA_HEAD_DIM = D_MODEL // DSA_HEADS
DSA_KV_HEADS = 4
IDX_HEADS = 16
IDX_DIM = 64
TOPK_MAX = 256
QUERY_BLOCK = 128
ODD_IN = DSA_HEADS * DSA_HEAD_DIM + 2 * DSA_KV_HEADS * DSA_HEAD_DIM + IDX_HEADS * IDX_DIM + IDX_DIM + IDX_HEADS
N_EXPERTS = 16
N_GROUPS = 4
EXPERTS_PER_GROUP = N_EXPERTS // N_GROUPS
TOP_K = 2
EXPERT_FF = 512
N_EVEN = (DEPTH + 1) // 2
N_ODD = DEPTH // 2
DEEPNORM_ALPHA = (2 * DEPTH) ** 0.25
DEEPNORM_BETA = (8 * DEPTH) ** -0.25

kernel_name = 'hybrid_retention_conformer_dsa_moe'

f32 = jnp.float32


def layer_norm(x, g, b):
    xf = x.astype(f32)
    mu = jnp.mean(xf, axis=-1, keepdims=True)
    var = jnp.mean(jnp.square(xf - mu), axis=-1, keepdims=True)
    y = (xf - mu) * lax.rsqrt(var + LN_EPS)
    return (y * g.astype(f32) + b.astype(f32)).astype(x.dtype)


def rope(x, pos):
    d = x.shape[-1]
    half = d // 2
    inv = ROPE_THETA ** (-jnp.arange(half, dtype=f32) / half)
    ang = pos.astype(f32)[:, None] * inv[None, :]
    cos = jnp.cos(ang)[None, :, None, :]
    sin = jnp.sin(ang)[None, :, None, :]
    xf = x.astype(f32)
    x1, x2 = xf[..., :half], xf[..., half:]
    return jnp.concatenate([x1 * cos - x2 * sin, x2 * cos + x1 * sin], axis=-1).astype(x.dtype)


def retention_chunkwise(q, k, v):
    B, S, H, Dk = q.shape
    Dv = v.shape[-1]
    C = RET_CHUNK
    nc = S // C
    log_g = jnp.log(1.0 - 2.0 ** (-5.0 - jnp.arange(H, dtype=f32)))
    j = jnp.arange(C, dtype=f32)
    diff = j[:, None] - j[None, :]
    decay_in = jnp.where(diff[None] >= 0, jnp.exp(jnp.maximum(diff, 0.0)[None] * log_g[:, None, None]), 0.0)
    xi = jnp.exp((j[None, :] + 1.0) * log_g[:, None])
    zeta = jnp.exp((C - 1.0 - j[None, :]) * log_g[:, None])
    chunk_decay = jnp.exp(C * log_g)

    def to_chunks(t):
        return t.astype(f32).reshape(B, nc, C, H, t.shape[-1]).transpose(1, 0, 3, 2, 4)

    def step(state, inp):
        qc, kc, vc = inp
        attn = jnp.einsum('bhqd,bhsd->bhqs', qc, kc) * decay_in[None]
        inner = jnp.einsum('bhqs,bhse->bhqe', attn, vc)
        cross = jnp.einsum('bhqd,bhde->bhqe', qc, state) * xi[None, :, :, None]
        new_state = state * chunk_decay[None, :, None, None] + jnp.einsum('bhsd,bhse->bhde', kc * zeta[None, :, :, None], vc)
        return new_state, inner + cross

    state0 = jnp.zeros((B, H, Dk, Dv), f32)
    _, ys = lax.scan(step, state0, (to_chunks(q), to_chunks(k), to_chunks(v)))
    return ys.transpose(1, 0, 3, 2, 4).reshape(B, S, H, Dv)


def conformer_conv(a, b, conv_w, conv_b, ln_g, ln_b):
    u = a * jax.nn.sigmoid(b)
    y = lax.conv_general_dilated(u, conv_w[:, None, :].astype(u.dtype), window_strides=(1,),
                                 padding=[(CONV_TAPS - 1, 0)], dimension_numbers=('NWC', 'WIO', 'NWC'),
                                 feature_group_count=u.shape[-1])
    y = layer_norm(y + conv_b, ln_g, ln_b)
    return jax.nn.silu(y)


def retention_conv_mixer(x, w_in, ret_gn_g, conv_w, conv_b, conv_ln_g, conv_ln_b, w_out):
    B, S, _ = x.shape
    pos = jnp.arange(S)
    R = RET_WIDTH
    proj = x @ w_in
    q, k, v, g, ga, gb = jnp.split(proj, [R, 2 * R, 3 * R, 4 * R, 4 * R + CONV_CHANNELS], axis=-1)
    q = rope(q.reshape(B, S, RET_HEADS, RET_HEAD_DIM), pos)
    k = rope(k.reshape(B, S, RET_HEADS, RET_HEAD_DIM), pos) * (RET_HEAD_DIM ** -0.5)
    v = v.reshape(B, S, RET_HEADS, RET_HEAD_DIM)
    y = retention_chunkwise(q, k, v)
    mu = jnp.mean(y, axis=-1, keepdims=True)
    var = jnp.mean(jnp.square(y - mu), axis=-1, keepdims=True)
    yn = (y - mu) * lax.rsqrt(var + LN_EPS) * ret_gn_g.astype(f32).reshape(RET_HEADS, RET_HEAD_DIM)
    ret = (jax.nn.silu(g.astype(f32)) * yn.reshape(B, S, R)).astype(x.dtype)
    conv = conformer_conv(ga, gb, conv_w, conv_b, conv_ln_g, conv_ln_b)
    return jnp.concatenate([ret, conv], axis=-1) @ w_out


def dsa_mixer(x, w_in, w_out):
    B, S, _ = x.shape
    pos = jnp.arange(S)
    hq = DSA_HEADS * DSA_HEAD_DIM
    hkv = DSA_KV_HEADS * DSA_HEAD_DIM
    hi = IDX_HEADS * IDX_DIM
    proj = x @ w_in
    q, k, v, qi, ki, wi = jnp.split(proj, [hq, hq + hkv, hq + 2 * hkv, hq + 2 * hkv + hi, hq + 2 * hkv + hi + IDX_DIM], axis=-1)
    q = rope(q.reshape(B, S, DSA_HEADS, DSA_HEAD_DIM), pos)
    k = rope(k.reshape(B, S, DSA_KV_HEADS, DSA_HEAD_DIM), pos)
    v = v.reshape(B, S, DSA_KV_HEADS, DSA_HEAD_DIM)
    qi = rope(qi.reshape(B, S, IDX_HEADS, IDX_DIM), pos).astype(f32)
    ki = rope(ki.reshape(B, S, 1, IDX_DIM), pos)[:, :, 0].astype(f32)
    wi = wi.astype(f32) * (IDX_HEADS ** -0.5)
    n_sel = min(TOPK_MAX, S // 4)
    nb = S // QUERY_BLOCK
    group = DSA_HEADS // DSA_KV_HEADS
    key_pos = jnp.arange(S)

    def blockify(t):
        return t.reshape((B, nb, QUERY_BLOCK) + t.shape[2:]).swapaxes(0, 1)

    def attend_block(args):
        qb, qib, wib, blk = args
        q_pos = blk * QUERY_BLOCK + jnp.arange(QUERY_BLOCK)
        rel = jax.nn.relu(jnp.einsum('bqhd,bsd->bqhs', qib, ki) * (IDX_DIM ** -0.5))
        score = jnp.einsum('bqhs,bqh->bqs', rel, wib)
        causal = key_pos[None, :] <= q_pos[:, None]
        score = jnp.where(causal[None], score, -jnp.inf)
        _, idx = lax.top_k(score, n_sel)
        valid = idx <= q_pos[None, :, None]
        k_sel = jax.vmap(lambda kk, ii: kk[ii])(k, idx)
        v_sel = jax.vmap(lambda vv, ii: vv[ii])(v, idx)
        qg = qb.reshape(B, QUERY_BLOCK, DSA_KV_HEADS, group, DSA_HEAD_DIM).astype(f32)
        logits = jnp.einsum('bqkgd,bqnkd->bqkgn', qg, k_sel.astype(f32)) * (DSA_HEAD_DIM ** -0.5)
        logits = jnp.where(valid[:, :, None, None, :], logits, -jnp.inf)
        p = jax.nn.softmax(logits, axis=-1)
        o = jnp.einsum('bqkgn,bqnkd->bqkgd', p, v_sel.astype(f32))
        return o.reshape(B, QUERY_BLOCK, hq).astype(x.dtype)

    out = lax.map(attend_block, (blockify(q), blockify(qi), blockify(wi), jnp.arange(nb)))
    out = out.swapaxes(0, 1).reshape(B, S, hq)
    return out @ w_out


def grouped_moe(x, router_w, router_b, w_gu, w_down):
    B, S, D = x.shape
    xt = x.reshape(B * S, D)
    affinity = jax.nn.sigmoid(xt.astype(f32) @ router_w.astype(f32))
    sel = affinity + router_b.astype(f32)
    grp = sel.reshape(-1, N_GROUPS, EXPERTS_PER_GROUP)
    group_score = jnp.sum(lax.top_k(grp, 2)[0], axis=-1)
    best = jnp.argmax(group_score, axis=-1)
    in_group = (jnp.arange(N_EXPERTS) // EXPERTS_PER_GROUP)[None, :] == best[:, None]
    _, eidx = lax.top_k(jnp.where(in_group, sel, -jnp.inf), TOP_K)
    w = jnp.take_along_axis(affinity, eidx, axis=-1)
    w = w / jnp.sum(w, axis=-1, keepdims=True)
    gate = jnp.sum(jax.nn.one_hot(eidx, N_EXPERTS, dtype=f32) * w[..., None], axis=1)
    y = jnp.zeros((B * S, D), f32)
    for e in range(N_EXPERTS):
        a, b = jnp.split(xt @ w_gu[e], 2, axis=-1)
        y = y + gate[:, e:e + 1] * ((jax.nn.silu(a) * b) @ w_down[e]).astype(f32)
    return y.reshape(B, S, D).astype(x.dtype)


def setup_inputs(seed: int = 0) -> dict:
    key = jax.random.key(seed)
    ks = jax.random.split(key, 18)

    def nrm(k, shape, scale):
        return jax.random.normal(k, shape, f32) * scale

    return {
        'x': nrm(ks[0], (BATCH, SEQ, D_MODEL), 1.0),
        'even_w_in': nrm(ks[1], (N_EVEN, D_MODEL, EVEN_IN), D_MODEL ** -0.5),
        'even_ret_gn_g': 1.0 + nrm(ks[2], (N_EVEN, RET_WIDTH), 0.02),
        'even_conv_w': nrm(ks[3], (N_EVEN, CONV_TAPS, CONV_CHANNELS), CONV_TAPS ** -0.5),
        'even_conv_b': nrm(ks[4], (N_EVEN, CONV_CHANNELS), 0.02),
        'even_conv_ln_g': 1.0 + nrm(ks[5], (N_EVEN, CONV_CHANNELS), 0.02),
        'even_conv_ln_b': nrm(ks[6], (N_EVEN, CONV_CHANNELS), 0.02),
        'even_w_out': nrm(ks[7], (N_EVEN, D_MODEL, D_MODEL), D_MODEL ** -0.5 * DEEPNORM_BETA),
        'odd_w_in': nrm(ks[8], (N_ODD, D_MODEL, ODD_IN), D_MODEL ** -0.5),
        'odd_w_out': nrm(ks[9], (N_ODD, DSA_HEADS * DSA_HEAD_DIM, D_MODEL), (DSA_HEADS * DSA_HEAD_DIM) ** -0.5 * DEEPNORM_BETA),
        'mix_ln_g': 1.0 + nrm(ks[10], (DEPTH, D_MODEL), 0.02),
        'mix_ln_b': nrm(ks[11], (DEPTH, D_MODEL), 0.02),
        'moe_w_gu': nrm(ks[12], (DEPTH, N_EXPERTS, D_MODEL, 2 * EXPERT_FF), D_MODEL ** -0.5),
        'moe_w_down': nrm(ks[13], (DEPTH, N_EXPERTS, EXPERT_FF, D_MODEL), EXPERT_FF ** -0.5 * DEEPNORM_BETA),
        'ffn_ln_g': 1.0 + nrm(ks[14], (DEPTH, D_MODEL), 0.02),
        'ffn_ln_b': nrm(ks[15], (DEPTH, D_MODEL), 0.02),
        'router_w': nrm(ks[16], (D_MODEL, N_EXPERTS), D_MODEL ** -0.5),
        'router_b': nrm(ks[17], (N_EXPERTS,), 0.01),
    }


def reference(x, even_w_in, even_ret_gn_g, even_conv_w, even_conv_b, even_conv_ln_g, even_conv_ln_b, even_w_out,
              odd_w_in, odd_w_out, mix_ln_g, mix_ln_b, moe_w_gu, moe_w_down, ffn_ln_g, ffn_ln_b, router_w, router_b):
    h = x
    for layer in range(DEPTH):
        i = layer // 2
        if layer % 2 == 0:
            m = retention_conv_mixer(h, even_w_in[i], even_ret_gn_g[i], even_conv_w[i], even_conv_b[i],
                                     even_conv_ln_g[i], even_conv_ln_b[i], even_w_out[i])
        else:
            m = dsa_mixer(h, odd_w_in[i], odd_w_out[i])
        h = layer_norm(DEEPNORM_ALPHA * h + m, mix_ln_g[layer], mix_ln_b[layer])
        f = grouped_moe(h, router_w, router_b, moe_w_gu[layer], moe_w_down[layer])
        h = layer_norm(DEEPNORM_ALPHA * h + f, ffn_ln_g[layer], ffn_ln_b[layer])
    return h
```

```python
import functools

import numpy as np
import jax
import jax.numpy as jnp
from jax import lax
from jax.experimental import pallas as pl
from jax.experimental.pallas import tpu as pltpu

f32 = jnp.float32
bf16 = jnp.bfloat16

ROPE_THETA = 10000.0
LN_EPS = 1e-5
RET_HEADS = 4
RET_HEAD_DIM = 256
RET_WIDTH = RET_HEADS * RET_HEAD_DIM
RET_CHUNK = 128
CONV_CHANNELS = 1024
CONV_TAPS = 31
DSA_HEADS = 16
DSA_HEAD_DIM = 128
DSA_KV_HEADS = 4
IDX_HEADS = 16
IDX_DIM = 64
TOPK_MAX = 256
N_EXPERTS = 16
N_GROUPS = 4
EXPERTS_PER_GROUP = N_EXPERTS // N_GROUPS
EXPERT_FF = 512

LANES = 128
VMEM_LIMIT = 56 * 1024 * 1024
NEG_BIG = -1e30
INT_MIN = -2 ** 31

NT_DIMS = (((1,), (1,)), ((), ()))
TN_DIMS = (((0,), (0,)), ((), ()))


def _params(sem):
    return pltpu.CompilerParams(dimension_semantics=sem, vmem_limit_bytes=VMEM_LIMIT)


def _layer_norm(z, g, b):
    mu = jnp.mean(z, axis=-1, keepdims=True)
    zc = z - mu
    var = jnp.mean(zc * zc, axis=-1, keepdims=True)
    return zc * lax.rsqrt(var + LN_EPS) * g + b


def _mm_kernel(x_ref, w_ref, o_ref):
    o_ref[...] = jnp.dot(x_ref[...], w_ref[...], preferred_element_type=f32).astype(o_ref.dtype)


def _matmul(x, w, out_dtype, tm=512, tn=1024):
    m, k = x.shape
    n = w.shape[1]
    tm, tn = min(tm, m), min(tn, n)
    return pl.pallas_call(
        _mm_kernel,
        grid=(n // tn, m // tm),
        in_specs=[pl.BlockSpec((tm, k), lambda j, i: (i, 0)),
                  pl.BlockSpec((k, tn), lambda j, i: (0, j))],
        out_specs=pl.BlockSpec((tm, tn), lambda j, i: (i, j)),
        out_shape=jax.ShapeDtypeStruct((m, n), out_dtype),
        compiler_params=_params(("parallel", "arbitrary")),
        name="matmul",
    )(x, w)


def _retention_tables(chunk):
    h = np.arange(RET_HEADS, dtype=np.float64)
    log_g = np.log(1.0 - 2.0 ** (-5.0 - h))
    j = np.arange(chunk, dtype=np.float64)
    diff = j[:, None] - j[None, :]
    decay_in = np.where(diff[None] >= 0, np.exp(np.maximum(diff, 0.0)[None] * log_g[:, None, None]), 0.0)
    xi = np.exp((j[None, :] + 1.0) * log_g[:, None])
    zeta = np.exp((chunk - 1.0 - j[None, :]) * log_g[:, None])
    chunk_decay = np.exp(chunk * log_g)
    xi_b = np.broadcast_to(xi[:, :, None], (RET_HEADS, chunk, RET_HEAD_DIM))
    zeta_b = np.broadcast_to(zeta[:, :, None], (RET_HEADS, chunk, RET_HEAD_DIM))
    return (jnp.asarray(decay_in, f32), jnp.asarray(xi_b, f32), jnp.asarray(zeta_b, f32),
            jnp.asarray(chunk_decay, f32))


def _rope_tables(seq, half):
    inv = ROPE_THETA ** (-np.arange(half, dtype=np.float64) / half)
    ang = np.arange(seq, dtype=np.float64)[:, None] * inv[None, :]
    return np.cos(ang), np.sin(ang)


def _retention_kernel(cd_ref, q_ref, k_ref, v_ref, g_ref, cos_ref, sin_ref, din_ref, xi_ref, zeta_ref,
                      gn_ref, o_ref, state_ref):
    h = pl.program_id(1)

    @pl.when(pl.program_id(2) == 0)
    def _():
        state_ref[...] = jnp.zeros_like(state_ref)

    cos = cos_ref[...]
    sin = sin_ref[...]
    half = RET_HEAD_DIM // 2

    def rope(x):
        x1, x2 = x[:, :half], x[:, half:]
        return jnp.concatenate([x1 * cos - x2 * sin, x2 * cos + x1 * sin], axis=1)

    q = rope(q_ref[0])
    k = rope(k_ref[0]) * (RET_HEAD_DIM ** -0.5)
    qb = q.astype(bf16)
    kb = k.astype(bf16)
    vb = v_ref[0].astype(bf16)
    attn = lax.dot_general(qb, kb, NT_DIMS, preferred_element_type=f32) * din_ref[0]
    inner = jnp.dot(attn.astype(bf16), vb, preferred_element_type=f32)
    state = state_ref[...]
    cross = jnp.dot(qb, state.astype(bf16), preferred_element_type=f32) * xi_ref[0]
    kz = (k * zeta_ref[0]).astype(bf16)
    state_ref[...] = state * cd_ref[h] + lax.dot_general(kz, vb, TN_DIMS, preferred_element_type=f32)
    y = inner + cross
    mu = jnp.mean(y, axis=-1, keepdims=True)
    yc = y - mu
    var = jnp.mean(yc * yc, axis=-1, keepdims=True)
    yn = yc * lax.rsqrt(var + LN_EPS) * gn_ref[...]
    g = g_ref[0]
    o_ref[0] = (g * jax.nn.sigmoid(g) * yn).astype(o_ref.dtype)


def _retention(proj, gn_g, batch, seq):
    c = RET_CHUNK
    dh = RET_HEAD_DIM
    decay_in, xi_b, zeta_b, chunk_decay = _retention_tables(c)
    cos, sin = _rope_tables(seq, dh // 2)
    col = lambda off: pl.BlockSpec((1, c, dh), lambda b, h, s, off=off: (b, s, off + h))
    per_head = pl.BlockSpec((1, c, dh), lambda b, h, s: (h, 0, 0))
    return pl.pallas_call(
        _retention_kernel,
        grid=(batch, RET_HEADS, seq // c),
        in_specs=[pl.BlockSpec(memory_space=pltpu.SMEM),
                  col(0), col(RET_HEADS), col(2 * RET_HEADS), col(3 * RET_HEADS),
                  pl.BlockSpec((c, dh // 2), lambda b, h, s: (s, 0)),
                  pl.BlockSpec((c, dh // 2), lambda b, h, s: (s, 0)),
                  pl.BlockSpec((1, c, c), lambda b, h, s: (h, 0, 0)),
                  per_head, per_head,
                  pl.BlockSpec((1, dh), lambda b, h, s: (0, h))],
        out_specs=pl.BlockSpec((1, c, dh), lambda b, h, s: (b, s, h)),
        out_shape=jax.ShapeDtypeStruct((batch, seq, RET_WIDTH), bf16),
        scratch_shapes=[pltpu.VMEM((dh, dh), f32)],
        compiler_params=_params(("parallel", "parallel", "arbitrary")),
        name="retention",
    )(chunk_decay, proj, proj, proj, proj, jnp.asarray(cos, f32), jnp.asarray(sin, f32),
      decay_in, xi_b, zeta_b, gn_g.reshape(1, RET_WIDTH))


CONV_ROWS = 128
CONV_HIST = 32
CONV_SUB = 32


def _conv_kernel(ga_ref, gb_ref, w_ref, b_ref, lg_ref, lb_ref, o_ref, ubuf):
    @pl.when(pl.program_id(1) == 0)
    def _():
        ubuf[0:CONV_HIST, :] = jnp.zeros((CONV_HIST, CONV_CHANNELS), f32)

    ubuf[CONV_HIST:CONV_HIST + CONV_ROWS, :] = ga_ref[0] * jax.nn.sigmoid(gb_ref[0])
    first = CONV_HIST - (CONV_TAPS - 1)
    for r in range(0, CONV_ROWS, CONV_SUB):
        acc = jnp.zeros((CONV_SUB, CONV_CHANNELS), f32)
        for t in range(CONV_TAPS):
            acc = acc + w_ref[t:t + 1, :] * ubuf[r + first + t:r + first + t + CONV_SUB, :]
        y = _layer_norm(acc + b_ref[...], lg_ref[...], lb_ref[...])
        o_ref[0, r:r + CONV_SUB, :] = (y * jax.nn.sigmoid(y)).astype(o_ref.dtype)
    ubuf[0:CONV_HIST, :] = ubuf[CONV_ROWS:CONV_ROWS + CONV_HIST, :]


def _conformer_conv(proj, conv_w, conv_b, ln_g, ln_b, batch, seq):
    c = CONV_CHANNELS
    a_blk = 4 * RET_WIDTH // c
    row = pl.BlockSpec((1, c), lambda b, s: (0, 0))
    return pl.pallas_call(
        _conv_kernel,
        grid=(batch, seq // CONV_ROWS),
        in_specs=[pl.BlockSpec((1, CONV_ROWS, c), lambda b, s: (b, s, a_blk)),
                  pl.BlockSpec((1, CONV_ROWS, c), lambda b, s: (b, s, a_blk + 1)),
                  pl.BlockSpec((CONV_TAPS, c), lambda b, s: (0, 0)),
                  row, row, row],
        out_specs=pl.BlockSpec((1, CONV_ROWS, c), lambda b, s: (b, s, 0)),
        out_shape=jax.ShapeDtypeStruct((batch, seq, c), bf16),
        scratch_shapes=[pltpu.VMEM((CONV_HIST + CONV_ROWS, c), f32)],
        compiler_params=_params(("parallel", "arbitrary")),
        name="conformer_conv",
    )(proj, proj, conv_w, conv_b.reshape(1, c), ln_g.reshape(1, c), ln_b.reshape(1, c))


def _route(sel, aff):
    epg = EXPERTS_PER_GROUP
    gscore = []
    for g in range(N_GROUPS):
        v = sel[g * epg:(g + 1) * epg]
        best = None
        for a in range(epg):
            for b in range(a + 1, epg):
                s = v[a] + v[b]
                best = s if best is None else jnp.maximum(best, s)
        gscore.append(best)
    gmax = functools.reduce(jnp.maximum, gscore)
    taken = None
    gsel = []
    for g in range(N_GROUPS):
        hit = gscore[g] == gmax
        if taken is not None:
            hit = jnp.logical_and(hit, jnp.logical_not(taken))
        taken = hit if taken is None else jnp.logical_or(taken, hit)
        gsel.append(hit)

    def pick(rows, j):
        out = rows[(N_GROUPS - 1) * epg + j]
        for g in range(N_GROUPS - 2, -1, -1):
            out = jnp.where(gsel[g], rows[g * epg + j], out)
        return out

    x = [pick(sel, j) for j in range(epg)]
    a = [pick(aff, j) for j in range(epg)]
    base = jnp.zeros_like(x[0], dtype=jnp.int32)
    for g in range(1, N_GROUPS):
        base = jnp.where(gsel[g], g * epg, base)

    def first_max(vals, excluded):
        vmax = functools.reduce(jnp.maximum, vals)
        taken = None
        hits = []
        for j in range(epg):
            hit = vals[j] == vmax
            if excluded is not None:
                hit = jnp.logical_and(hit, jnp.logical_not(excluded[j]))
            if taken is not None:
                hit = jnp.logical_and(hit, jnp.logical_not(taken))
            taken = hit if taken is None else jnp.logical_or(taken, hit)
            hits.append(hit)
        return hits

    h1 = first_max(x, None)
    x2 = [jnp.where(h1[j], -jnp.inf, x[j]) for j in range(epg)]
    h2 = first_max(x2, h1)

    def gather(hits):
        idx = base
        val = jnp.zeros_like(a[0])
        for j in range(epg):
            idx = jnp.where(hits[j], base + j, idx)
            val = jnp.where(hits[j], a[j], val)
        return idx, val

    e0, a0 = gather(h1)
    e1, a1 = gather(h2)
    tot = a0 + a1
    return e0, e1, a0 / tot, a1 / tot


def _proj_norm_router_kernel(a_ref, w_ref, h_ref, g_ref, b_ref, rwt_ref, rb_ref,
                             hn_ref, hb_ref, ridx_ref, rwgt_ref, *, alpha):
    m = jnp.dot(a_ref[...], w_ref[...], preferred_element_type=f32)
    hn = _layer_norm(alpha * h_ref[...] + m, g_ref[...], b_ref[...])
    hn_ref[...] = hn
    hb_ref[...] = hn.astype(bf16)
    logits = lax.dot_general(rwt_ref[...], hn, NT_DIMS, preferred_element_type=f32,
                             precision=lax.Precision.HIGHEST)
    aff = jax.nn.sigmoid(logits)
    sel = aff + rb_ref[...]
    e0, e1, w0, w1 = _route([sel[e:e + 1, :] for e in range(N_EXPERTS)],
                            [aff[e:e + 1, :] for e in range(N_EXPERTS)])
    ridx_ref[0:1, :] = e0
    ridx_ref[1:2, :] = e1
    rwgt_ref[0:1, :] = w0
    rwgt_ref[1:2, :] = w1


def _proj_norm_router(a, w, h, ln_g, ln_b, router_wt, router_b, alpha, tm=256):
    t, k = a.shape
    d = w.shape[1]
    tm = min(tm, t)
    row = pl.BlockSpec((1, d), lambda i: (0, 0))
    return pl.pallas_call(
        functools.partial(_proj_norm_router_kernel, alpha=alpha),
        grid=(t // tm,),
        in_specs=[pl.BlockSpec((tm, k), lambda i: (i, 0)),
                  pl.BlockSpec((k, d), lambda i: (0, 0)),
                  pl.BlockSpec((tm, d), lambda i: (i, 0)),
                  row, row,
                  pl.BlockSpec((N_EXPERTS, d), lambda i: (0, 0)),
                  pl.BlockSpec((N_EXPERTS, 1), lambda i: (0, 0))],
        out_specs=[pl.BlockSpec((tm, d), lambda i: (i, 0)),
                   pl.BlockSpec((tm, d), lambda i: (i, 0)),
                   pl.BlockSpec((2, tm), lambda i: (0, i)),
                   pl.BlockSpec((2, tm), lambda i: (0, i))],
        out_shape=[jax.ShapeDtypeStruct((t, d), f32), jax.ShapeDtypeStruct((t, d), bf16),
                   jax.ShapeDtypeStruct((2, t), jnp.int32), jax.ShapeDtypeStruct((2, t), f32)],
        compiler_params=_params(("parallel",)),
        name="proj_norm_router",
    )(a, w, h, ln_g.reshape(1, d), ln_b.reshape(1, d), router_wt, router_b.reshape(N_EXPERTS, 1))


MOE_TILE = 512


def _experts_kernel(te_ref, tv_ref, x_ref, wgu_ref, wd_ref, wgt_ref, o_ref):
    i = pl.program_id(0)

    @pl.when(tv_ref[i] > 0)
    def _():
        hgu = jnp.dot(x_ref[...], wgu_ref[0], preferred_element_type=f32)
        a, b = hgu[:, :EXPERT_FF], hgu[:, EXPERT_FF:]
        act = (a * jax.nn.sigmoid(a) * b).astype(bf16)
        y = jnp.dot(act, wd_ref[0], preferred_element_type=f32)
        o_ref[...] = y * wgt_ref[...]

    @pl.when(tv_ref[i] == 0)
    def _():
        o_ref[...] = jnp.zeros_like(o_ref)


def _experts(xs, w_gu, w_down, wgt, tile_expert, tile_valid):
    r, d = xs.shape
    ff2 = w_gu.shape[2]
    grid_spec = pltpu.PrefetchScalarGridSpec(
        num_scalar_prefetch=2,
        grid=(r // MOE_TILE,),
        in_specs=[pl.BlockSpec((MOE_TILE, d), lambda i, te, tv: (i, 0)),
                  pl.BlockSpec((1, d, ff2), lambda i, te, tv: (te[i], 0, 0)),
                  pl.BlockSpec((1, ff2 // 2, d), lambda i, te, tv: (te[i], 0, 0)),
                  pl.BlockSpec((MOE_TILE, 1), lambda i, te, tv: (i, 0))],
        out_specs=pl.BlockSpec((MOE_TILE, d), lambda i, te, tv: (i, 0)),
    )
    return pl.pallas_call(
        _experts_kernel,
        grid_spec=grid_spec,
        out_shape=jax.ShapeDtypeStruct((r, d), f32),
        compiler_params=_params(("arbitrary",)),
        name="experts",
    )(tile_expert, tile_valid, xs, w_gu, w_down, wgt)


def _dispatch(ridx, rwgt):
    t = ridx.shape[1]
    slots = 2 * t
    es = ridx.T.reshape(slots)
    ws = rwgt.T.reshape(slots)
    onehot = (es[:, None] == jnp.arange(N_EXPERTS, dtype=jnp.int32)[None, :]).astype(jnp.int32)
    csum = jnp.cumsum(onehot, axis=0)
    rank = jnp.sum((csum - onehot) * onehot, axis=1)
    tiles_per = (csum[-1] + MOE_TILE - 1) // MOE_TILE
    tile_end = jnp.cumsum(tiles_per)
    start = (tile_end - tiles_per) * MOE_TILE
    dest = start[es] + rank
    n_tiles = slots // MOE_TILE + N_EXPERTS
    tile_ids = jnp.arange(n_tiles, dtype=jnp.int32)
    tile_expert = jnp.minimum(jnp.searchsorted(tile_end, tile_ids, side="right"), N_EXPERTS - 1).astype(jnp.int32)
    tile_valid = (tile_ids < tile_end[-1]).astype(jnp.int32)
    rows = n_tiles * MOE_TILE
    tok = jnp.zeros((rows,), jnp.int32).at[dest].set(jnp.arange(slots, dtype=jnp.int32) // 2)
    wgt = jnp.zeros((rows,), f32).at[dest].set(ws)
    return tok, wgt.reshape(rows, 1), dest.reshape(t, 2), tile_expert, tile_valid


def _combine_norm_kernel(h_ref, ya_ref, yb_ref, g_ref, b_ref, o_ref, ob_ref, *, alpha):
    z = alpha * h_ref[...] + (ya_ref[...] + yb_ref[...])
    hn = _layer_norm(z, g_ref[...], b_ref[...])
    o_ref[...] = hn
    ob_ref[...] = hn.astype(bf16)


def _combine_norm(h, ya, yb, ln_g, ln_b, alpha, tm=512):
    t, d = h.shape
    tm = min(tm, t)
    blk = pl.BlockSpec((tm, d), lambda i: (i, 0))
    row = pl.BlockSpec((1, d), lambda i: (0, 0))
    return pl.pallas_call(
        functools.partial(_combine_norm_kernel, alpha=alpha),
        grid=(t // tm,),
        in_specs=[blk, blk, blk, row, row],
        out_specs=[blk, blk],
        out_shape=[jax.ShapeDtypeStruct((t, d), f32), jax.ShapeDtypeStruct((t, d), bf16)],
        compiler_params=_params(("parallel",)),
        name="combine_norm",
    )(h, ya, yb, ln_g.reshape(1, d), ln_b.reshape(1, d))


def _moe(hn, hb, ridx, rwgt, w_gu, w_down, ln_g, ln_b, alpha):
    tok, wgt, pos, tile_expert, tile_valid = _dispatch(ridx, rwgt)
    xs = hb.at[tok].get(mode="promise_in_bounds")
    ys = _experts(xs, w_gu, w_down, wgt, tile_expert, tile_valid)
    ya = ys.at[pos[:, 0]].get(mode="promise_in_bounds")
    yb = ys.at[pos[:, 1]].get(mode="promise_in_bounds")
    return _combine_norm(hn, ya, yb, ln_g, ln_b, alpha)


DSA_Q = DSA_HEADS * DSA_HEAD_DIM
DSA_KV = DSA_KV_HEADS * DSA_HEAD_DIM
DSA_QI = IDX_HEADS * IDX_DIM
DSA_MAIN = DSA_Q + 2 * DSA_KV + DSA_QI


def _lane_iota(shape):
    return lax.broadcasted_iota(jnp.int32, shape, 1)


def _rope128(x, cos2, sin2):
    return x * cos2 + pltpu.roll(x, LANES // 2, 1) * sin2


def _rope64(x, cos4, sin4):
    lane = _lane_iota(x.shape)
    partner = jnp.where(lane % IDX_DIM < IDX_DIM // 2,
                        pltpu.roll(x, LANES - IDX_DIM // 2, 1), pltpu.roll(x, IDX_DIM // 2, 1))
    return x * cos4 + partner * sin4


def _dsa_prep_kernel(p_ref, t_ref, c2_ref, s2_ref, c4_ref, s4_ref,
                     q_ref, k_ref, v_ref, qi_ref, ki_ref):
    c2, s2, c4, s4 = c2_ref[...], s2_ref[...], c4_ref[...], s4_ref[...]
    qscale = DSA_HEAD_DIM ** -0.5
    for j in range(DSA_Q // LANES):
        x = p_ref[0, :, j * LANES:(j + 1) * LANES]
        q_ref[0, :, j * LANES:(j + 1) * LANES] = (_rope128(x, c2, s2) * qscale).astype(bf16)
    for j in range(DSA_KV // LANES):
        o = DSA_Q + j * LANES
        k_ref[0, :, j * LANES:(j + 1) * LANES] = _rope128(p_ref[0, :, o:o + LANES], c2, s2).astype(bf16)
        o = DSA_Q + DSA_KV + j * LANES
        v_ref[0, :, j * LANES:(j + 1) * LANES] = p_ref[0, :, o:o + LANES].astype(bf16)
    lane = _lane_iota((p_ref.shape[1], LANES))
    low = lane < IDX_DIM
    iscale = IDX_DIM ** -0.5
    for j in range(DSA_QI // LANES):
        o = DSA_Q + 2 * DSA_KV + j * LANES
        r = _rope64(p_ref[0, :, o:o + LANES], c4, s4) * iscale
        qi_ref[0, :, (2 * j) * LANES:(2 * j + 1) * LANES] = jnp.where(low, r, 0.0).astype(bf16)
        qi_ref[0, :, (2 * j + 1) * LANES:(2 * j + 2) * LANES] = jnp.where(
            low, pltpu.roll(r, LANES // 2, 1), 0.0).astype(bf16)
    ki_ref[0] = jnp.where(low, _rope64(t_ref[0], c4, s4), 0.0).astype(bf16)


def _dsa_prep(main, tail, batch, seq, ts=256):
    ts = min(ts, seq)
    c, s = _rope_tables(seq, DSA_HEAD_DIM // 2)
    c2 = jnp.asarray(np.concatenate([c, c], axis=1), f32)
    s2 = jnp.asarray(np.concatenate([-s, s], axis=1), f32)
    c, s = _rope_tables(seq, IDX_DIM // 2)
    c4 = jnp.asarray(np.concatenate([c, c, c, c], axis=1), f32)
    s4 = jnp.asarray(np.concatenate([-s, s, -s, s], axis=1), f32)
    tab = pl.BlockSpec((ts, LANES), lambda b, i: (i, 0))
    out = lambda w: pl.BlockSpec((1, ts, w), lambda b, i: (b, i, 0))
    return pl.pallas_call(
        _dsa_prep_kernel,
        grid=(batch, seq // ts),
        in_specs=[out(DSA_MAIN), out(LANES), tab, tab, tab, tab],
        out_specs=[out(DSA_Q), out(DSA_KV), out(DSA_KV), out(IDX_HEADS * LANES), out(LANES)],
        out_shape=[jax.ShapeDtypeStruct((batch, seq, DSA_Q), bf16),
                   jax.ShapeDtypeStruct((batch, seq, DSA_KV), bf16),
                   jax.ShapeDtypeStruct((batch, seq, DSA_KV), bf16),
                   jax.ShapeDtypeStruct((batch, seq, IDX_HEADS * LANES), bf16),
                   jax.ShapeDtypeStruct((batch, seq, LANES), bf16)],
        compiler_params=_params(("parallel", "parallel")),
        name="dsa_prep",
    )(main, tail, c2, s2, c4, s4)


SEL_ROWS = 128
SEL_CHUNK = 512
SEL_SUB = 256


def _select_kernel(qi_ref, ki_ref, t_ref, bias_ref, key_ref, *, n_sel, n_chunks):
    tq, kc = SEL_ROWS, SEL_CHUNK
    qt = pl.program_id(1)
    q0 = qt * tq
    n_act = lax.div(qt, kc // tq) + 1
    qpos = q0 + lax.broadcasted_iota(jnp.int32, (tq, 1), 0)
    wscale = IDX_HEADS ** -0.5

    def score_chunk(c, carry):
        off = pl.multiple_of(c * kc, kc)
        for s in range(kc // SEL_SUB):
            ki = ki_ref[0, pl.ds(off + s * SEL_SUB, SEL_SUB), :]
            acc = jnp.zeros((tq, SEL_SUB), f32)
            for h in range(IDX_HEADS):
                r = lax.dot_general(qi_ref[0, :, h * LANES:(h + 1) * LANES], ki, NT_DIMS,
                                    preferred_element_type=f32)
                w = t_ref[0, :, IDX_DIM + h:IDX_DIM + h + 1] * wscale
                acc = acc + jnp.maximum(r, 0.0) * w
            kpos = off + s * SEL_SUB + _lane_iota((tq, SEL_SUB))
            sc = jnp.where(kpos <= qpos, acc + 0.0, -jnp.inf)
            bits = pltpu.bitcast(sc, jnp.int32)
            key_ref[c, :, s * SEL_SUB:(s + 1) * SEL_SUB] = bits ^ ((bits >> 31) & 0x7FFFFFFF)
        return carry

    lax.fori_loop(0, n_act, score_chunk, 0)

    def count_ge(cand):
        def body(c, acc):
            hit = jnp.where(key_ref[c] >= cand, 1.0, 0.0)
            for s in range(kc // LANES):
                acc = acc + hit[:, s * LANES:(s + 1) * LANES]
            return acc
        acc = lax.fori_loop(0, n_act, body, jnp.zeros((tq, LANES), f32))
        return jnp.sum(acc, axis=1, keepdims=True)

    kf = float(n_sel)
    ans = jnp.where(count_ge(jnp.zeros((tq, 1), jnp.int32)) >= kf, 0, INT_MIN).astype(jnp.int32)

    def bit_step(i, ans):
        cand = ans | jnp.left_shift(jnp.int32(1), 30 - i)
        return jnp.where(count_ge(cand) >= kf, cand, ans)

    ans = lax.fori_loop(0, 31, bit_step, ans)
    need = kf - count_ge(ans + 1)
    ans = jnp.where(qpos < n_sel, INT_MIN, ans)

    upper = (lax.broadcasted_iota(jnp.int32, (SEL_SUB, SEL_SUB), 0)
             < lax.broadcasted_iota(jnp.int32, (SEL_SUB, SEL_SUB), 1)).astype(bf16)

    def emit_chunk(c, seen):
        off = pl.multiple_of(c * kc, kc)
        for s in range(kc // SEL_SUB):
            key = key_ref[c, :, s * SEL_SUB:(s + 1) * SEL_SUB]
            eq = key == ans
            eqf = jnp.where(eq, 1.0, 0.0)
            before = seen + jnp.dot(eqf.astype(bf16), upper, preferred_element_type=f32)
            keep = jnp.logical_or(key > ans, jnp.logical_and(eq, before < need))
            kpos = off + s * SEL_SUB + _lane_iota((tq, SEL_SUB))
            keep = jnp.logical_and(keep, kpos <= qpos)
            bias_ref[0, c, :, s * SEL_SUB:(s + 1) * SEL_SUB] = jnp.where(keep, 0.0, NEG_BIG).astype(bf16)
            seen = seen + jnp.sum(eqf, axis=1, keepdims=True)
        return seen

    lax.fori_loop(0, n_act, emit_chunk, jnp.zeros((tq, 1), f32))

    def fill_chunk(c, carry):
        bias_ref[0, c] = jnp.full((tq, kc), NEG_BIG, bf16)
        return carry

    lax.fori_loop(n_act, n_chunks, fill_chunk, 0)


def _select(qi, ki, tail, batch, seq, n_sel):
    n_chunks = seq // SEL_CHUNK
    assert SEL_CHUNK >= n_sel and SEL_CHUNK % SEL_ROWS == 0
    return pl.pallas_call(
        functools.partial(_select_kernel, n_sel=n_sel, n_chunks=n_chunks),
        grid=(batch, seq // SEL_ROWS),
        in_specs=[pl.BlockSpec((1, SEL_ROWS, IDX_HEADS * LANES), lambda b, i: (b, i, 0)),
                  pl.BlockSpec((1, seq, LANES), lambda b, i: (b, 0, 0)),
                  pl.BlockSpec((1, SEL_ROWS, LANES), lambda b, i: (b, i, 0))],
        out_specs=pl.BlockSpec((1, n_chunks, SEL_ROWS, SEL_CHUNK), lambda b, i: (b, 0, i, 0)),
        out_shape=jax.ShapeDtypeStruct((batch, n_chunks, seq, SEL_CHUNK), bf16),
        scratch_shapes=[pltpu.VMEM((n_chunks, SEL_ROWS, SEL_CHUNK), jnp.int32)],
        compiler_params=_params(("parallel", "parallel")),
        name="dsa_select",
    )(qi, ki, tail)


ATT_TILE = SEL_CHUNK
ATT_GROUP = DSA_HEADS // DSA_KV_HEADS


def _attention_kernel(q_ref, k_ref, v_ref, b_ref, o_ref, m_ref, l_ref, acc_ref):
    i = pl.program_id(2)
    j = pl.program_id(3)
    dh = DSA_HEAD_DIM

    @pl.when(j == 0)
    def _():
        m_ref[...] = jnp.full(m_ref.shape, NEG_BIG, f32)
        l_ref[...] = jnp.zeros_like(l_ref)
        acc_ref[...] = jnp.zeros_like(acc_ref)

    @pl.when(j <= i)
    def _():
        bias = b_ref[0, 0].astype(f32)
        k = k_ref[0]
        v = v_ref[0]
        for h in range(ATT_GROUP):
            s = lax.dot_general(q_ref[0, :, h * dh:(h + 1) * dh], k, NT_DIMS,
                                preferred_element_type=f32) + bias
            m_old = m_ref[h]
            m_new = jnp.maximum(m_old, jnp.max(s, axis=1, keepdims=True))
            p = jnp.exp(s - m_new)
            scale = jnp.exp(m_old - m_new)
            l_ref[h] = scale * l_ref[h] + jnp.sum(p, axis=1, keepdims=True)
            acc_ref[h] = scale * acc_ref[h] + jnp.dot(p.astype(bf16), v, preferred_element_type=f32)
            m_ref[h] = m_new

    @pl.when(j == i)
    def _():
        for h in range(ATT_GROUP):
            o_ref[0, :, h * dh:(h + 1) * dh] = (acc_ref[h] / l_ref[h]).astype(o_ref.dtype)


def _attention(q, k, v, bias, batch, seq):
    t = ATT_TILE
    n = seq // t
    gw = ATT_GROUP * DSA_HEAD_DIM
    return pl.pallas_call(
        _attention_kernel,
        grid=(batch, DSA_KV_HEADS, n, n),
        in_specs=[pl.BlockSpec((1, t, gw), lambda b, g, i, j: (b, i, g)),
                  pl.BlockSpec((1, t, DSA_HEAD_DIM), lambda b, g, i, j: (b, jnp.minimum(i, j), g)),
                  pl.BlockSpec((1, t, DSA_HEAD_DIM), lambda b, g, i, j: (b, jnp.minimum(i, j), g)),
                  pl.BlockSpec((1, 1, t, t), lambda b, g, i, j: (b, jnp.minimum(i, j), i, 0))],
        out_specs=pl.BlockSpec((1, t, gw), lambda b, g, i, j: (b, i, g)),
        out_shape=jax.ShapeDtypeStruct((batch, seq, DSA_Q), bf16),
        scratch_shapes=[pltpu.VMEM((ATT_GROUP, t, 1), f32), pltpu.VMEM((ATT_GROUP, t, 1), f32),
                        pltpu.VMEM((ATT_GROUP, t, DSA_HEAD_DIM), f32)],
        compiler_params=_params(("parallel", "parallel", "parallel", "arbitrary")),
        name="dsa_attention",
    )(q, k, v, bias)


def _even_mixer(hb, w_in, gn_g, conv_w, conv_b, conv_ln_g, conv_ln_b, batch, seq):
    proj = _matmul(hb, w_in.astype(bf16), f32).reshape(batch, seq, -1)
    ret = _retention(proj, gn_g, batch, seq)
    conv = _conformer_conv(proj, conv_w, conv_b, conv_ln_g, conv_ln_b, batch, seq)
    return jnp.concatenate([ret, conv], axis=-1).reshape(batch * seq, -1)


def _odd_mixer(hb, w_in, batch, seq):
    d = w_in.shape[0]
    w_main = w_in[:, :DSA_MAIN].astype(bf16)
    w_tail = jnp.concatenate(
        [w_in[:, DSA_MAIN:], jnp.zeros((d, LANES - (w_in.shape[1] - DSA_MAIN)), w_in.dtype)], axis=1).astype(bf16)
    main = _matmul(hb, w_main, f32).reshape(batch, seq, DSA_MAIN)
    tail = _matmul(hb, w_tail, f32).reshape(batch, seq, LANES)
    q, k, v, qi, ki = _dsa_prep(main, tail, batch, seq)
    bias = _select(qi, ki, tail, batch, seq, min(TOPK_MAX, seq // 4))
    return _attention(q, k, v, bias, batch, seq).reshape(batch * seq, DSA_Q)


def kernel(x, even_w_in, even_ret_gn_g, even_conv_w, even_conv_b, even_conv_ln_g, even_conv_ln_b, even_w_out,
           odd_w_in, odd_w_out, mix_ln_g, mix_ln_b, moe_w_gu, moe_w_down, ffn_ln_g, ffn_ln_b, router_w, router_b):
    batch, seq, d = x.shape
    depth = mix_ln_g.shape[0]
    alpha = (2 * depth) ** 0.25
    router_wt = router_w.astype(f32).T
    h = x.reshape(batch * seq, d).astype(f32)
    hb = h.astype(bf16)
    for layer in range(depth):
        i = layer // 2
        if layer % 2 == 0:
            mixed = _even_mixer(hb, even_w_in[i], even_ret_gn_g[i], even_conv_w[i], even_conv_b[i],
                                even_conv_ln_g[i], even_conv_ln_b[i], batch, seq)
            w_out = even_w_out[i]
        else:
            mixed = _odd_mixer(hb, odd_w_in[i], batch, seq)
            w_out = odd_w_out[i]
        hn, hnb, ridx, rwgt = _proj_norm_router(mixed, w_out.astype(bf16), h, mix_ln_g[layer], mix_ln_b[layer],
                                                router_wt, router_b.astype(f32), alpha)
        h, hb = _moe(hn, hnb, ridx, rwgt, moe_w_gu[layer].astype(bf16), moe_w_down[layer].astype(bf16),
                     ffn_ln_g[layer], ffn_ln_b[layer], alpha)
    return h.reshape(batch, seq, d).astype(x.dtype)
```

```python
import functools

import numpy as np
import jax
import jax.numpy as jnp
from jax import lax
from jax.experimental import pallas as pl
from jax.experimental.pallas import tpu as pltpu

f32 = jnp.float32
bf16 = jnp.bfloat16

ROPE_THETA = 10000.0
LN_EPS = 1e-5
RET_HEADS = 4
RET_HEAD_DIM = 256
RET_WIDTH = RET_HEADS * RET_HEAD_DIM
RET_CHUNK = 128
CONV_CHANNELS = 1024
CONV_TAPS = 31
DSA_HEADS = 16
DSA_HEAD_DIM = 128
DSA_KV_HEADS = 4
IDX_HEADS = 16
IDX_DIM = 64
TOPK_MAX = 256
N_EXPERTS = 16
N_GROUPS = 4
EXPERTS_PER_GROUP = N_EXPERTS // N_GROUPS
EXPERT_FF = 512

LANES = 128
VMEM_LIMIT = 56 * 1024 * 1024
NEG_BIG = -1e30
LOG2E = 1.4426950408889634
INT_MIN = -2 ** 31

NT_DIMS = (((1,), (1,)), ((), ()))
TN_DIMS = (((0,), (0,)), ((), ()))


def _params(sem):
    return pltpu.CompilerParams(dimension_semantics=sem, vmem_limit_bytes=VMEM_LIMIT)


def _layer_norm(z, g, b):
    mu = jnp.mean(z, axis=-1, keepdims=True)
    zc = z - mu
    var = jnp.mean(zc * zc, axis=-1, keepdims=True)
    return zc * lax.rsqrt(var + LN_EPS) * g + b


def _mm_kernel(x_ref, w_ref, o_ref, wb_ref):
    @pl.when(pl.program_id(1) == 0)
    def _():
        wb_ref[...] = w_ref[...].astype(bf16)

    o_ref[...] = jnp.dot(x_ref[...], wb_ref[...], preferred_element_type=f32).astype(o_ref.dtype)


def _matmul(x, w, out_dtype, n=None, tm=512, tn=1024):
    m, k = x.shape
    n = w.shape[1] if n is None else n
    tm, tn = min(tm, m), min(tn, n)
    assert m % tm == 0 and n % tn == 0
    return pl.pallas_call(
        _mm_kernel,
        grid=(n // tn, m // tm),
        in_specs=[pl.BlockSpec((tm, k), lambda j, i: (i, 0)),
                  pl.BlockSpec((k, tn), lambda j, i: (0, j))],
        out_specs=pl.BlockSpec((tm, tn), lambda j, i: (i, j)),
        out_shape=jax.ShapeDtypeStruct((m, n), out_dtype),
        scratch_shapes=[pltpu.VMEM((k, tn), bf16)],
        compiler_params=_params(("arbitrary", "arbitrary")),
        name="matmul",
    )(x, w)


def _retention_tables(chunk):
    h = np.arange(RET_HEADS, dtype=np.float64)
    log_g = np.log(1.0 - 2.0 ** (-5.0 - h))
    j = np.arange(chunk, dtype=np.float64)
    diff = j[:, None] - j[None, :]
    decay_in = np.where(diff[None] >= 0, np.exp(np.maximum(diff, 0.0)[None] * log_g[:, None, None]), 0.0)
    xi = np.exp((j[None, :] + 1.0) * log_g[:, None])
    zeta = np.exp((chunk - 1.0 - j[None, :]) * log_g[:, None])
    chunk_decay = np.exp(chunk * log_g)
    xi_b = np.broadcast_to(xi[:, :, None], (RET_HEADS, chunk, RET_HEAD_DIM))
    zeta_b = np.broadcast_to(zeta[:, :, None], (RET_HEADS, chunk, RET_HEAD_DIM))
    return (jnp.asarray(decay_in, f32), jnp.asarray(xi_b, f32), jnp.asarray(zeta_b, f32),
            jnp.asarray(chunk_decay, f32))


def _rope_tables(seq, half):
    inv = ROPE_THETA ** (-np.arange(half, dtype=np.float64) / half)
    ang = np.arange(seq, dtype=np.float64)[:, None] * inv[None, :]
    return np.cos(ang), np.sin(ang)


def _retention_kernel(cd_ref, q_ref, k_ref, v_ref, g_ref, cos_ref, sin_ref, din_ref, xi_ref, zeta_ref,
                      gn_ref, o_ref, state_ref):
    h = pl.program_id(1)

    @pl.when(pl.program_id(2) == 0)
    def _():
        state_ref[...] = jnp.zeros_like(state_ref)

    cos = cos_ref[...]
    sin = sin_ref[...]
    half = RET_HEAD_DIM // 2

    def rope(x):
        x1, x2 = x[:, :half], x[:, half:]
        return jnp.concatenate([x1 * cos - x2 * sin, x2 * cos + x1 * sin], axis=1)

    q = rope(q_ref[0])
    k = rope(k_ref[0]) * (RET_HEAD_DIM ** -0.5)
    qb = q.astype(bf16)
    kb = k.astype(bf16)
    vb = v_ref[0].astype(bf16)
    attn = lax.dot_general(qb, kb, NT_DIMS, preferred_element_type=f32) * din_ref[0]
    inner = jnp.dot(attn.astype(bf16), vb, preferred_element_type=f32)
    state = state_ref[...]
    cross = jnp.dot(qb, state.astype(bf16), preferred_element_type=f32) * xi_ref[0]
    kz = (k * zeta_ref[0]).astype(bf16)
    state_ref[...] = state * cd_ref[h] + lax.dot_general(kz, vb, TN_DIMS, preferred_element_type=f32)
    y = inner + cross
    mu = jnp.mean(y, axis=-1, keepdims=True)
    yc = y - mu
    var = jnp.mean(yc * yc, axis=-1, keepdims=True)
    yn = yc * lax.rsqrt(var + LN_EPS) * gn_ref[...]
    g = g_ref[0]
    o_ref[0] = (g * jax.nn.sigmoid(g) * yn).astype(o_ref.dtype)


def _retention(proj, gn_g, batch, seq):
    c = RET_CHUNK
    dh = RET_HEAD_DIM
    decay_in, xi_b, zeta_b, chunk_decay = _retention_tables(c)
    cos, sin = _rope_tables(seq, dh // 2)
    col = lambda off: pl.BlockSpec((1, c, dh), lambda b, h, s, off=off: (b, s, off + h))
    per_head = pl.BlockSpec((1, c, dh), lambda b, h, s: (h, 0, 0))
    return pl.pallas_call(
        _retention_kernel,
        grid=(batch, RET_HEADS, seq // c),
        in_specs=[pl.BlockSpec(memory_space=pltpu.SMEM),
                  col(0), col(RET_HEADS), col(2 * RET_HEADS), col(3 * RET_HEADS),
                  pl.BlockSpec((c, dh // 2), lambda b, h, s: (s, 0)),
                  pl.BlockSpec((c, dh // 2), lambda b, h, s: (s, 0)),
                  pl.BlockSpec((1, c, c), lambda b, h, s: (h, 0, 0)),
                  per_head, per_head,
                  pl.BlockSpec((1, dh), lambda b, h, s: (0, h))],
        out_specs=pl.BlockSpec((1, c, dh), lambda b, h, s: (b, s, h)),
        out_shape=jax.ShapeDtypeStruct((batch, seq, RET_WIDTH), bf16),
        scratch_shapes=[pltpu.VMEM((dh, dh), f32)],
        compiler_params=_params(("parallel", "parallel", "arbitrary")),
        name="retention",
    )(chunk_decay, proj, proj, proj, proj, jnp.asarray(cos, f32), jnp.asarray(sin, f32),
      decay_in, xi_b, zeta_b, gn_g.reshape(1, RET_WIDTH))


CONV_ROWS = 128
CONV_HIST = 32
CONV_SUB = 32


def _conv_kernel(ga_ref, gb_ref, w_ref, b_ref, lg_ref, lb_ref, o_ref, ubuf):
    @pl.when(pl.program_id(1) == 0)
    def _():
        ubuf[0:CONV_HIST, :] = jnp.zeros((CONV_HIST, CONV_CHANNELS), f32)

    ubuf[CONV_HIST:CONV_HIST + CONV_ROWS, :] = ga_ref[0] * jax.nn.sigmoid(gb_ref[0])
    first = CONV_HIST - (CONV_TAPS - 1)
    for r in range(0, CONV_ROWS, CONV_SUB):
        acc = jnp.zeros((CONV_SUB, CONV_CHANNELS), f32)
        for t in range(CONV_TAPS):
            acc = acc + w_ref[t:t + 1, :] * ubuf[r + first + t:r + first + t + CONV_SUB, :]
        y = _layer_norm(acc + b_ref[...], lg_ref[...], lb_ref[...])
        o_ref[0, r:r + CONV_SUB, :] = (y * jax.nn.sigmoid(y)).astype(o_ref.dtype)
    ubuf[0:CONV_HIST, :] = ubuf[CONV_ROWS:CONV_ROWS + CONV_HIST, :]


def _conformer_conv(proj, conv_w, conv_b, ln_g, ln_b, batch, seq):
    c = CONV_CHANNELS
    a_blk = 4 * RET_WIDTH // c
    row = pl.BlockSpec((1, c), lambda b, s: (0, 0))
    return pl.pallas_call(
        _conv_kernel,
        grid=(batch, seq // CONV_ROWS),
        in_specs=[pl.BlockSpec((1, CONV_ROWS, c), lambda b, s: (b, s, a_blk)),
                  pl.BlockSpec((1, CONV_ROWS, c), lambda b, s: (b, s, a_blk + 1)),
                  pl.BlockSpec((CONV_TAPS, c), lambda b, s: (0, 0)),
                  row, row, row],
        out_specs=pl.BlockSpec((1, CONV_ROWS, c), lambda b, s: (b, s, 0)),
        out_shape=jax.ShapeDtypeStruct((batch, seq, c), bf16),
        scratch_shapes=[pltpu.VMEM((CONV_HIST + CONV_ROWS, c), f32)],
        compiler_params=_params(("parallel", "arbitrary")),
        name="conformer_conv",
    )(proj, proj, conv_w, conv_b.reshape(1, c), ln_g.reshape(1, c), ln_b.reshape(1, c))


def _route(sel, aff):
    epg = EXPERTS_PER_GROUP
    gscore = []
    for g in range(N_GROUPS):
        v = sel[g * epg:(g + 1) * epg]
        best = None
        for a in range(epg):
            for b in range(a + 1, epg):
                s = v[a] + v[b]
                best = s if best is None else jnp.maximum(best, s)
        gscore.append(best)
    gmax = functools.reduce(jnp.maximum, gscore)
    taken = None
    gsel = []
    for g in range(N_GROUPS):
        hit = gscore[g] == gmax
        if taken is not None:
            hit = jnp.logical_and(hit, jnp.logical_not(taken))
        taken = hit if taken is None else jnp.logical_or(taken, hit)
        gsel.append(hit)

    def pick(rows, j):
        out = rows[(N_GROUPS - 1) * epg + j]
        for g in range(N_GROUPS - 2, -1, -1):
            out = jnp.where(gsel[g], rows[g * epg + j], out)
        return out

    x = [pick(sel, j) for j in range(epg)]
    a = [pick(aff, j) for j in range(epg)]
    base = jnp.zeros_like(x[0], dtype=jnp.int32)
    for g in range(1, N_GROUPS):
        base = jnp.where(gsel[g], g * epg, base)

    def first_max(vals, excluded):
        vmax = functools.reduce(jnp.maximum, vals)
        taken = None
        hits = []
        for j in range(epg):
            hit = vals[j] == vmax
            if excluded is not None:
                hit = jnp.logical_and(hit, jnp.logical_not(excluded[j]))
            if taken is not None:
                hit = jnp.logical_and(hit, jnp.logical_not(taken))
            taken = hit if taken is None else jnp.logical_or(taken, hit)
            hits.append(hit)
        return hits

    h1 = first_max(x, None)
    x2 = [jnp.where(h1[j], -jnp.inf, x[j]) for j in range(epg)]
    h2 = first_max(x2, h1)

    def gather(hits):
        idx = base
        val = jnp.zeros_like(a[0])
        for j in range(epg):
            idx = jnp.where(hits[j], base + j, idx)
            val = jnp.where(hits[j], a[j], val)
        return idx, val

    e0, a0 = gather(h1)
    e1, a1 = gather(h2)
    tot = a0 + a1
    return e0, e1, a0 / tot, a1 / tot


def _router_weights(router_w):
    d, e = router_w.shape
    w = router_w.astype(f32)
    w_hi = w.astype(bf16)
    w_lo = (w - w_hi.astype(f32)).astype(bf16)
    top = jnp.concatenate([w_hi, w_lo, jnp.zeros((d, LANES - 2 * e), bf16)], axis=1)
    bot = jnp.concatenate([w_hi, jnp.zeros((d, LANES - e), bf16)], axis=1)
    return jnp.concatenate([top, bot], axis=0)


def _proj_norm_router_kernel(a_ref, w_ref, h_ref, g_ref, b_ref, rw_ref, rb_ref,
                             hn_ref, hb_ref, ridx_ref, rwgt_ref, *, alpha):
    m = jnp.dot(a_ref[...], w_ref[...], preferred_element_type=f32)
    hn = _layer_norm(alpha * h_ref[...] + m, g_ref[...], b_ref[...])
    hn_ref[...] = hn
    hi = hn.astype(bf16)
    hb_ref[...] = hi
    lo = (hn - hi.astype(f32)).astype(bf16)
    d = hn.shape[1]
    parts = (jnp.dot(hi, rw_ref[0:d, :], preferred_element_type=f32)
             + jnp.dot(lo, rw_ref[d:2 * d, :], preferred_element_type=f32))
    parts = parts.T
    logits = parts[0:N_EXPERTS] + parts[N_EXPERTS:2 * N_EXPERTS]
    aff = jax.nn.sigmoid(logits)
    sel = aff + rb_ref[...]
    e0, e1, w0, w1 = _route([sel[e:e + 1, :] for e in range(N_EXPERTS)],
                            [aff[e:e + 1, :] for e in range(N_EXPERTS)])
    ridx_ref[0:1, :] = e0
    ridx_ref[1:2, :] = e1
    rwgt_ref[0:1, :] = w0
    rwgt_ref[1:2, :] = w1


def _proj_norm_router(a, w, h, ln_g, ln_b, router_ws, router_b, alpha, tm=512):
    t, k = a.shape
    d = w.shape[1]
    tm = min(tm, t)
    row = pl.BlockSpec((1, d), lambda i: (0, 0))
    return pl.pallas_call(
        functools.partial(_proj_norm_router_kernel, alpha=alpha),
        grid=(t // tm,),
        in_specs=[pl.BlockSpec((tm, k), lambda i: (i, 0)),
                  pl.BlockSpec((k, d), lambda i: (0, 0)),
                  pl.BlockSpec((tm, d), lambda i: (i, 0)),
                  row, row,
                  pl.BlockSpec((2 * d, LANES), lambda i: (0, 0)),
                  pl.BlockSpec((N_EXPERTS, 1), lambda i: (0, 0))],
        out_specs=[pl.BlockSpec((tm, d), lambda i: (i, 0)),
                   pl.BlockSpec((tm, d), lambda i: (i, 0)),
                   pl.BlockSpec((2, tm), lambda i: (0, i)),
                   pl.BlockSpec((2, tm), lambda i: (0, i))],
        out_shape=[jax.ShapeDtypeStruct((t, d), f32), jax.ShapeDtypeStruct((t, d), bf16),
                   jax.ShapeDtypeStruct((2, t), jnp.int32), jax.ShapeDtypeStruct((2, t), f32)],
        compiler_params=_params(("parallel",)),
        name="proj_norm_router",
    )(a, w, h, ln_g.reshape(1, d), ln_b.reshape(1, d), router_ws, router_b.reshape(N_EXPERTS, 1))


MOE_TILE = 512


def _experts_kernel(te_ref, tv_ref, x_ref, wgu_ref, wd_ref, o_ref, wgu_b, wd_b):
    i = pl.program_id(0)
    prev = te_ref[jnp.maximum(i - 1, 0)]

    @pl.when(jnp.logical_or(i == 0, te_ref[i] != prev))
    def _():
        wgu_b[...] = wgu_ref[0].astype(bf16)
        wd_b[...] = wd_ref[0].astype(bf16)

    @pl.when(tv_ref[i] > 0)
    def _():
        hgu = jnp.dot(x_ref[...], wgu_b[...], preferred_element_type=f32)
        a, b = hgu[:, :EXPERT_FF], hgu[:, EXPERT_FF:]
        act = (a * jax.nn.sigmoid(a) * b).astype(bf16)
        o_ref[...] = jnp.dot(act, wd_b[...], preferred_element_type=f32).astype(o_ref.dtype)

    @pl.when(tv_ref[i] == 0)
    def _():
        o_ref[...] = jnp.zeros_like(o_ref)


def _experts(xs, w_gu, w_down, tile_expert, tile_valid):
    r, d = xs.shape
    ff2 = w_gu.shape[2]
    grid_spec = pltpu.PrefetchScalarGridSpec(
        num_scalar_prefetch=2,
        grid=(r // MOE_TILE,),
        in_specs=[pl.BlockSpec((MOE_TILE, d), lambda i, te, tv: (i, 0)),
                  pl.BlockSpec((1, d, ff2), lambda i, te, tv: (te[i], 0, 0)),
                  pl.BlockSpec((1, ff2 // 2, d), lambda i, te, tv: (te[i], 0, 0))],
        out_specs=pl.BlockSpec((MOE_TILE, d), lambda i, te, tv: (i, 0)),
        scratch_shapes=[pltpu.VMEM((d, ff2), bf16), pltpu.VMEM((ff2 // 2, d), bf16)],
    )
    return pl.pallas_call(
        _experts_kernel,
        grid_spec=grid_spec,
        out_shape=jax.ShapeDtypeStruct((r, d), bf16),
        compiler_params=_params(("arbitrary",)),
        name="experts",
    )(tile_expert, tile_valid, xs, w_gu, w_down)


def _dispatch(ridx):
    t = ridx.shape[1]
    slots = 2 * t
    es = ridx.T.reshape(slots)
    experts = jnp.arange(N_EXPERTS, dtype=jnp.int32)
    onehot = (es[:, None] == experts[None, :]).astype(jnp.int32)
    csum = jnp.cumsum(onehot, axis=0)
    rank = jnp.sum((csum - onehot) * onehot, axis=1)
    counts = csum[-1]
    tiles_per = (counts + MOE_TILE - 1) // MOE_TILE
    tile_end = jnp.cumsum(tiles_per)
    start = (tile_end - tiles_per) * MOE_TILE
    dest = start[es] + rank
    n_tiles = slots // MOE_TILE + N_EXPERTS
    tile_ids = jnp.arange(n_tiles, dtype=jnp.int32)
    tile_valid = (tile_ids < tile_end[-1]).astype(jnp.int32)
    tile_expert = jnp.sum((tile_ids[:, None] >= tile_end[None, :]).astype(jnp.int32), axis=1)
    last_used = jnp.max(jnp.where(counts > 0, experts, 0))
    tile_expert = jnp.minimum(tile_expert, last_used)
    rows = n_tiles * MOE_TILE
    tok = jnp.zeros((rows,), jnp.int32).at[dest].set(jnp.arange(slots, dtype=jnp.int32) // 2)
    return tok, dest, tile_expert, tile_valid


def _combine_norm_kernel(h_ref, y_ref, w0_ref, w1_ref, g_ref, b_ref, o_ref, ob_ref, *, alpha):
    d = h_ref.shape[1]
    f = w0_ref[...] * y_ref[:, :d].astype(f32) + w1_ref[...] * y_ref[:, d:].astype(f32)
    hn = _layer_norm(alpha * h_ref[...] + f, g_ref[...], b_ref[...])
    o_ref[...] = hn
    ob_ref[...] = hn.astype(bf16)


def _combine_norm(h, y2, w0, w1, ln_g, ln_b, alpha, tm=512):
    t, d = h.shape
    tm = min(tm, t)
    blk = pl.BlockSpec((tm, d), lambda i: (i, 0))
    col = pl.BlockSpec((tm, 1), lambda i: (i, 0))
    row = pl.BlockSpec((1, d), lambda i: (0, 0))
    return pl.pallas_call(
        functools.partial(_combine_norm_kernel, alpha=alpha),
        grid=(t // tm,),
        in_specs=[blk, pl.BlockSpec((tm, 2 * d), lambda i: (i, 0)), col, col, row, row],
        out_specs=[blk, blk],
        out_shape=[jax.ShapeDtypeStruct((t, d), f32), jax.ShapeDtypeStruct((t, d), bf16)],
        compiler_params=_params(("parallel",)),
        name="combine_norm",
    )(h, y2, w0, w1, ln_g.reshape(1, d), ln_b.reshape(1, d))


def _moe(hn, hb, ridx, rwgt, w_gu, w_down, ln_g, ln_b, alpha):
    t, d = hn.shape
    tok, dest, tile_expert, tile_valid = _dispatch(ridx)
    xs = hb.at[tok].get(mode="promise_in_bounds")
    ys = _experts(xs, w_gu, w_down, tile_expert, tile_valid)
    y2 = ys.at[dest].get(mode="promise_in_bounds").reshape(t, 2 * d)
    return _combine_norm(hn, y2, rwgt[0].reshape(t, 1), rwgt[1].reshape(t, 1), ln_g, ln_b, alpha)


DSA_Q = DSA_HEADS * DSA_HEAD_DIM
DSA_KV = DSA_KV_HEADS * DSA_HEAD_DIM
DSA_QI = IDX_HEADS * IDX_DIM
DSA_MAIN = DSA_Q + 2 * DSA_KV + DSA_QI


def _lane_iota(shape):
    return lax.broadcasted_iota(jnp.int32, shape, 1)


def _rope128(x, cos2, sin2):
    return x * cos2 + pltpu.roll(x, LANES // 2, 1) * sin2


def _rope64(x, cos4, sin4):
    lane = _lane_iota(x.shape)
    partner = jnp.where(lane % IDX_DIM < IDX_DIM // 2,
                        pltpu.roll(x, LANES - IDX_DIM // 2, 1), pltpu.roll(x, IDX_DIM // 2, 1))
    return x * cos4 + partner * sin4


def _dsa_prep_kernel(p_ref, t_ref, c2_ref, s2_ref, c4_ref, s4_ref,
                     q_ref, k_ref, v_ref, qi_ref, ki_ref):
    c2, s2, c4, s4 = c2_ref[...], s2_ref[...], c4_ref[...], s4_ref[...]
    qscale = LOG2E * DSA_HEAD_DIM ** -0.5
    for j in range(DSA_Q // LANES):
        x = p_ref[0, :, j * LANES:(j + 1) * LANES]
        q_ref[0, :, j * LANES:(j + 1) * LANES] = (_rope128(x, c2, s2) * qscale).astype(bf16)
    for j in range(DSA_KV // LANES):
        o = DSA_Q + j * LANES
        k_ref[0, :, j * LANES:(j + 1) * LANES] = _rope128(p_ref[0, :, o:o + LANES], c2, s2).astype(bf16)
        o = DSA_Q + DSA_KV + j * LANES
        v_ref[0, :, j * LANES:(j + 1) * LANES] = p_ref[0, :, o:o + LANES].astype(bf16)
    lane = _lane_iota((p_ref.shape[1], LANES))
    low = lane < IDX_DIM
    iscale = IDX_DIM ** -0.5
    for j in range(DSA_QI // LANES):
        o = DSA_Q + 2 * DSA_KV + j * LANES
        r = _rope64(p_ref[0, :, o:o + LANES], c4, s4) * iscale
        qi_ref[0, :, (2 * j) * LANES:(2 * j + 1) * LANES] = jnp.where(low, r, 0.0).astype(bf16)
        qi_ref[0, :, (2 * j + 1) * LANES:(2 * j + 2) * LANES] = jnp.where(
            low, pltpu.roll(r, LANES // 2, 1), 0.0).astype(bf16)
    ki_ref[0] = jnp.where(low, _rope64(t_ref[0], c4, s4), 0.0).astype(bf16)


def _dsa_prep(main, tail, batch, seq, ts=256):
    ts = min(ts, seq)
    c, s = _rope_tables(seq, DSA_HEAD_DIM // 2)
    c2 = jnp.asarray(np.concatenate([c, c], axis=1), f32)
    s2 = jnp.asarray(np.concatenate([-s, s], axis=1), f32)
    c, s = _rope_tables(seq, IDX_DIM // 2)
    c4 = jnp.asarray(np.concatenate([c, c, c, c], axis=1), f32)
    s4 = jnp.asarray(np.concatenate([-s, s, -s, s], axis=1), f32)
    tab = pl.BlockSpec((ts, LANES), lambda b, i: (i, 0))
    out = lambda w: pl.BlockSpec((1, ts, w), lambda b, i: (b, i, 0))
    return pl.pallas_call(
        _dsa_prep_kernel,
        grid=(batch, seq // ts),
        in_specs=[out(DSA_MAIN), out(LANES), tab, tab, tab, tab],
        out_specs=[out(DSA_Q), out(DSA_KV), out(DSA_KV), out(IDX_HEADS * LANES), out(LANES)],
        out_shape=[jax.ShapeDtypeStruct((batch, seq, DSA_Q), bf16),
                   jax.ShapeDtypeStruct((batch, seq, DSA_KV), bf16),
                   jax.ShapeDtypeStruct((batch, seq, DSA_KV), bf16),
                   jax.ShapeDtypeStruct((batch, seq, IDX_HEADS * LANES), bf16),
                   jax.ShapeDtypeStruct((batch, seq, LANES), bf16)],
        compiler_params=_params(("parallel", "parallel")),
        name="dsa_prep",
    )(main, tail, c2, s2, c4, s4)


SEL_ROWS = 128
SEL_CHUNK = 512
SEL_SUB = 256


def _select_kernel(qi_ref, ki_ref, t_ref, bias_ref, key_ref, *, n_sel, n_chunks):
    tq, kc = SEL_ROWS, SEL_CHUNK
    qt = pl.program_id(1)
    q0 = qt * tq
    n_act = lax.div(qt, kc // tq) + 1
    qpos = q0 + lax.broadcasted_iota(jnp.int32, (tq, 1), 0)
    wscale = IDX_HEADS ** -0.5

    def score_chunk(c, carry):
        off = pl.multiple_of(c * kc, kc)
        for s in range(kc // SEL_SUB):
            ki = ki_ref[0, pl.ds(off + s * SEL_SUB, SEL_SUB), :]
            acc = jnp.zeros((tq, SEL_SUB), f32)
            for h in range(IDX_HEADS):
                r = lax.dot_general(qi_ref[0, :, h * LANES:(h + 1) * LANES], ki, NT_DIMS,
                                    preferred_element_type=f32)
                w = t_ref[0, :, IDX_DIM + h:IDX_DIM + h + 1] * wscale
                acc = acc + jnp.maximum(r, 0.0) * w
            kpos = off + s * SEL_SUB + _lane_iota((tq, SEL_SUB))
            sc = jnp.where(kpos <= qpos, acc + 0.0, -jnp.inf)
            bits = pltpu.bitcast(sc, jnp.int32)
            key_ref[c, :, s * SEL_SUB:(s + 1) * SEL_SUB] = bits ^ ((bits >> 31) & 0x7FFFFFFF)
        return carry

    lax.fori_loop(0, n_act, score_chunk, 0)

    def count_ge(cand):
        def body(c, acc):
            hit = jnp.where(key_ref[c] >= cand, 1.0, 0.0)
            for s in range(kc // LANES):
                acc = acc + hit[:, s * LANES:(s + 1) * LANES]
            return acc
        acc = lax.fori_loop(0, n_act, body, jnp.zeros((tq, LANES), f32))
        return jnp.sum(acc, axis=1, keepdims=True)

    kf = float(n_sel)
    ans = jnp.where(count_ge(jnp.zeros((tq, 1), jnp.int32)) >= kf, 0, INT_MIN).astype(jnp.int32)

    def bit_step(i, ans):
        cand = ans | jnp.left_shift(jnp.int32(1), 30 - i)
        return jnp.where(count_ge(cand) >= kf, cand, ans)

    ans = lax.fori_loop(0, 31, bit_step, ans)
    need = kf - count_ge(ans + 1)
    ans = jnp.where(qpos < n_sel, INT_MIN, ans)

    upper = (lax.broadcasted_iota(jnp.int32, (SEL_SUB, SEL_SUB), 0)
             < lax.broadcasted_iota(jnp.int32, (SEL_SUB, SEL_SUB), 1)).astype(bf16)

    def emit_chunk(c, seen):
        off = pl.multiple_of(c * kc, kc)
        for s in range(kc // SEL_SUB):
            key = key_ref[c, :, s * SEL_SUB:(s + 1) * SEL_SUB]
            eq = key == ans
            eqf = jnp.where(eq, 1.0, 0.0)
            before = seen + jnp.dot(eqf.astype(bf16), upper, preferred_element_type=f32)
            keep = jnp.logical_or(key > ans, jnp.logical_and(eq, before < need))
            kpos = off + s * SEL_SUB + _lane_iota((tq, SEL_SUB))
            keep = jnp.logical_and(keep, kpos <= qpos)
            bias_ref[0, c, :, s * SEL_SUB:(s + 1) * SEL_SUB] = jnp.where(keep, 0.0, NEG_BIG).astype(bf16)
            seen = seen + jnp.sum(eqf, axis=1, keepdims=True)
        return seen

    lax.fori_loop(0, n_act, emit_chunk, jnp.zeros((tq, 1), f32))

    def fill_chunk(c, carry):
        bias_ref[0, c] = jnp.full((tq, kc), NEG_BIG, bf16)
        return carry

    lax.fori_loop(n_act, n_chunks, fill_chunk, 0)


def _select(qi, ki, tail, batch, seq, n_sel):
    n_chunks = seq // SEL_CHUNK
    assert SEL_CHUNK >= n_sel and SEL_CHUNK % SEL_ROWS == 0
    return pl.pallas_call(
        functools.partial(_select_kernel, n_sel=n_sel, n_chunks=n_chunks),
        grid=(batch, seq // SEL_ROWS),
        in_specs=[pl.BlockSpec((1, SEL_ROWS, IDX_HEADS * LANES), lambda b, i: (b, i, 0)),
                  pl.BlockSpec((1, seq, LANES), lambda b, i: (b, 0, 0)),
                  pl.BlockSpec((1, SEL_ROWS, LANES), lambda b, i: (b, i, 0))],
        out_specs=pl.BlockSpec((1, n_chunks, SEL_ROWS, SEL_CHUNK), lambda b, i: (b, 0, i, 0)),
        out_shape=jax.ShapeDtypeStruct((batch, n_chunks, seq, SEL_CHUNK), bf16),
        scratch_shapes=[pltpu.VMEM((n_chunks, SEL_ROWS, SEL_CHUNK), jnp.int32)],
        compiler_params=_params(("parallel", "parallel")),
        name="dsa_select",
    )(qi, ki, tail)


ATT_TILE = SEL_CHUNK
ATT_GROUP = DSA_HEADS // DSA_KV_HEADS


def _attention_kernel(q_ref, k_ref, v_ref, b_ref, o_ref, m_ref, acc_ref):
    i = pl.program_id(2)
    j = pl.program_id(3)
    dh = DSA_HEAD_DIM
    t = ATT_TILE

    @pl.when(j == 0)
    def _():
        m_ref[...] = jnp.full(m_ref.shape, NEG_BIG, f32)
        acc_ref[...] = jnp.zeros_like(acc_ref)

    @pl.when(j <= i)
    def _():
        k = k_ref[0]
        v_ones = jnp.concatenate([v_ref[0], jnp.ones((t, dh), bf16)], axis=1)
        for h in range(ATT_GROUP):
            s = lax.dot_general(q_ref[0, :, h * dh:(h + 1) * dh], k, NT_DIMS, preferred_element_type=f32)
            s = s + b_ref[0, 0].astype(f32)
            m_old = m_ref[h]
            m_new = jnp.maximum(m_old, jnp.max(s, axis=1, keepdims=True))
            p = jnp.concatenate([jnp.exp2(s[:, c * LANES:(c + 1) * LANES] - m_new) for c in range(t // LANES)],
                                axis=1)
            scale = jnp.exp2(m_old - m_new)
            pv = jnp.dot(p.astype(bf16), v_ones, preferred_element_type=f32)
            acc_ref[h] = jnp.concatenate([scale, scale], axis=1) * acc_ref[h] + pv
            m_ref[h] = m_new

    @pl.when(j == i)
    def _():
        for h in range(ATT_GROUP):
            acc = acc_ref[h]
            o_ref[0, :, h * dh:(h + 1) * dh] = (acc[:, :dh] / acc[:, dh:]).astype(o_ref.dtype)


def _attention(q, k, v, bias, batch, seq):
    t = ATT_TILE
    n = seq // t
    dh = DSA_HEAD_DIM
    gw = ATT_GROUP * dh
    return pl.pallas_call(
        _attention_kernel,
        grid=(batch, DSA_KV_HEADS, n, n),
        in_specs=[pl.BlockSpec((1, t, gw), lambda b, g, i, j: (b, i, g)),
                  pl.BlockSpec((1, t, dh), lambda b, g, i, j: (b, jnp.minimum(i, j), g)),
                  pl.BlockSpec((1, t, dh), lambda b, g, i, j: (b, jnp.minimum(i, j), g)),
                  pl.BlockSpec((1, 1, t, t), lambda b, g, i, j: (b, jnp.minimum(i, j), i, 0))],
        out_specs=pl.BlockSpec((1, t, gw), lambda b, g, i, j: (b, i, g)),
        out_shape=jax.ShapeDtypeStruct((batch, seq, DSA_Q), bf16),
        scratch_shapes=[pltpu.VMEM((ATT_GROUP, t, LANES), f32), pltpu.VMEM((ATT_GROUP, t, 2 * dh), f32)],
        compiler_params=_params(("parallel", "parallel", "parallel", "arbitrary")),
        name="dsa_attention",
    )(q, k, v, bias)


def _even_mixer(hb, w_in, gn_g, conv_w, conv_b, conv_ln_g, conv_ln_b, batch, seq):
    proj = _matmul(hb, w_in, f32).reshape(batch, seq, -1)
    ret = _retention(proj, gn_g, batch, seq)
    conv = _conformer_conv(proj, conv_w, conv_b, conv_ln_g, conv_ln_b, batch, seq)
    return jnp.concatenate([ret, conv], axis=-1).reshape(batch * seq, -1)


def _odd_mixer(hb, w_in, batch, seq):
    d = w_in.shape[0]
    w_tail = jnp.concatenate(
        [w_in[:, DSA_MAIN:], jnp.zeros((d, LANES - (w_in.shape[1] - DSA_MAIN)), w_in.dtype)], axis=1)
    main = _matmul(hb, w_in, f32, n=DSA_MAIN).reshape(batch, seq, DSA_MAIN)
    tail = _matmul(hb, w_tail, f32).reshape(batch, seq, LANES)
    q, k, v, qi, ki = _dsa_prep(main, tail, batch, seq)
    bias = _select(qi, ki, tail, batch, seq, min(TOPK_MAX, seq // 4))
    return _attention(q, k, v, bias, batch, seq).reshape(batch * seq, DSA_Q)


def kernel(x, even_w_in, even_ret_gn_g, even_conv_w, even_conv_b, even_conv_ln_g, even_conv_ln_b, even_w_out,
           odd_w_in, odd_w_out, mix_ln_g, mix_ln_b, moe_w_gu, moe_w_down, ffn_ln_g, ffn_ln_b, router_w, router_b):
    batch, seq, d = x.shape
    depth = mix_ln_g.shape[0]
    alpha = (2 * depth) ** 0.25
    router_ws = _router_weights(router_w)
    h = x.reshape(batch * seq, d).astype(f32)
    hb = h.astype(bf16)
    for layer in range(depth):
        i = layer // 2
        if layer % 2 == 0:
            mixed = _even_mixer(hb, even_w_in[i], even_ret_gn_g[i], even_conv_w[i], even_conv_b[i],
                                even_conv_ln_g[i], even_conv_ln_b[i], batch, seq)
            w_out = even_w_out[i]
        else:
            mixed = _odd_mixer(hb, odd_w_in[i], batch, seq)
            w_out = odd_w_out[i]
        hn, hnb, ridx, rwgt = _proj_norm_router(mixed, w_out.astype(bf16), h, mix_ln_g[layer], mix_ln_b[layer],
                                                router_ws, router_b.astype(f32), alpha)
        h, hb = _moe(hn, hnb, ridx, rwgt, moe_w_gu[layer], moe_w_down[layer],
                     ffn_ln_g[layer], ffn_ln_b[layer], alpha)
    return h.reshape(batch, seq, d).astype(x.dtype)
```

```python
import functools

import numpy as np
import jax
import jax.numpy as jnp
from jax import lax
from jax.experimental import pallas as pl
from jax.experimental.pallas import tpu as pltpu

f32 = jnp.float32
bf16 = jnp.bfloat16

ROPE_THETA = 10000.0
LN_EPS = 1e-5
RET_HEADS = 4
RET_HEAD_DIM = 256
RET_WIDTH = RET_HEADS * RET_HEAD_DIM
RET_BLOCK = 256
CONV_CHANNELS = 1024
CONV_TAPS = 31
DSA_HEADS = 16
DSA_HEAD_DIM = 128
DSA_KV_HEADS = 4
IDX_HEADS = 16
IDX_DIM = 64
TOPK_MAX = 256
N_EXPERTS = 16
N_GROUPS = 4
EXPERTS_PER_GROUP = N_EXPERTS // N_GROUPS
EXPERT_FF = 512

LANES = 128
SUBLANES = 8
VMEM_LIMIT = 56 * 1024 * 1024
NEG_BIG = -1e30
LOG2E = 1.4426950408889634
INT_MIN = -2 ** 31

NT_DIMS = (((1,), (1,)), ((), ()))
TN_DIMS = (((0,), (0,)), ((), ()))


def _params(sem):
    return pltpu.CompilerParams(dimension_semantics=sem, vmem_limit_bytes=VMEM_LIMIT)


def _layer_norm(z, g, b):
    mu = jnp.mean(z, axis=-1, keepdims=True)
    zc = z - mu
    var = jnp.mean(zc * zc, axis=-1, keepdims=True)
    return zc * lax.rsqrt(var + LN_EPS) * g + b


def _mm_kernel(x_ref, w_ref, o_ref, wb_ref):
    @pl.when(pl.program_id(1) == 0)
    def _():
        wb_ref[...] = w_ref[...].astype(bf16)

    o_ref[...] = jnp.dot(x_ref[...], wb_ref[...], preferred_element_type=f32).astype(o_ref.dtype)


def _matmul(x, w, out_dtype, n=None, tm=512, tn=1024):
    m, k = x.shape
    n = w.shape[1] if n is None else n
    tm, tn = min(tm, m), min(tn, n)
    assert m % tm == 0 and n % tn == 0
    return pl.pallas_call(
        _mm_kernel,
        grid=(n // tn, m // tm),
        in_specs=[pl.BlockSpec((tm, k), lambda j, i: (i, 0)),
                  pl.BlockSpec((k, tn), lambda j, i: (0, j))],
        out_specs=pl.BlockSpec((tm, tn), lambda j, i: (i, j)),
        out_shape=jax.ShapeDtypeStruct((m, n), out_dtype),
        scratch_shapes=[pltpu.VMEM((k, tn), bf16)],
        compiler_params=_params(("arbitrary", "arbitrary")),
        name="matmul",
    )(x, w)


def _mm_nt_kernel(x_ref, wt_ref, o_ref, wb_ref, *, valid):
    @pl.when(pl.program_id(1) == 0)
    def _():
        w = wt_ref[...]
        if valid < wt_ref.shape[0]:
            w = jnp.where(lax.broadcasted_iota(jnp.int32, w.shape, 0) < valid, w, 0.0)
        wb_ref[...] = w.astype(bf16)

    o_ref[...] = lax.dot_general(x_ref[...], wb_ref[...], NT_DIMS, preferred_element_type=f32).astype(o_ref.dtype)


def _matmul_nt(x, wt, out_dtype, row0, n, tm=512, tn=1024):
    m, k = x.shape
    tm, tn = min(tm, m), min(tn, n)
    assert m % tm == 0 and n % tn == 0 and row0 % tn == 0
    valid = min(tn, wt.shape[0] - row0 - (n - tn))
    assert valid == tn or n == tn
    return pl.pallas_call(
        functools.partial(_mm_nt_kernel, valid=valid),
        grid=(n // tn, m // tm),
        in_specs=[pl.BlockSpec((tm, k), lambda j, i: (i, 0)),
                  pl.BlockSpec((tn, k), lambda j, i: (row0 // tn + j, 0))],
        out_specs=pl.BlockSpec((tm, tn), lambda j, i: (i, j)),
        out_shape=jax.ShapeDtypeStruct((m, n), out_dtype),
        scratch_shapes=[pltpu.VMEM((tn, k), bf16)],
        compiler_params=_params(("arbitrary", "arbitrary")),
        name="matmul_nt",
    )(x, wt)


def _retention_tables(chunk):
    h = np.arange(RET_HEADS, dtype=np.float64)
    log_g = np.log(1.0 - 2.0 ** (-5.0 - h))
    j = np.arange(chunk, dtype=np.float64)
    diff = j[:, None] - j[None, :]
    decay_in = np.where(diff[None] >= 0, np.exp(np.maximum(diff, 0.0)[None] * log_g[:, None, None]), 0.0)
    xi = np.exp((j[None, :] + 1.0) * log_g[:, None])
    zeta = np.exp((chunk - 1.0 - j[None, :]) * log_g[:, None])
    chunk_decay = np.exp(chunk * log_g)
    xi_b = np.broadcast_to(xi[:, :, None], (RET_HEADS, chunk, RET_HEAD_DIM))
    zeta_b = np.broadcast_to(zeta[:, :, None], (RET_HEADS, chunk, RET_HEAD_DIM))
    return jnp.asarray(decay_in, f32), jnp.asarray(xi_b, f32), jnp.asarray(zeta_b, f32), chunk_decay


def _rope_tables(seq, half):
    inv = ROPE_THETA ** (-np.arange(half, dtype=np.float64) / half)
    ang = np.arange(seq, dtype=np.float64)[:, None] * inv[None, :]
    return np.cos(ang), np.sin(ang)


def _retention_kernel(q_ref, k_ref, v_ref, g_ref, cos_ref, sin_ref, din_ref, xi_ref, zeta_ref,
                      gn_ref, o_ref, state_ref, *, chunk_decay):
    @pl.when(pl.program_id(1) == 0)
    def _():
        state_ref[...] = jnp.zeros_like(state_ref)

    cos = cos_ref[...]
    sin = sin_ref[...]
    dh = RET_HEAD_DIM
    half = dh // 2

    def rope(x):
        x1, x2 = x[:, :half], x[:, half:]
        return jnp.concatenate([x1 * cos - x2 * sin, x2 * cos + x1 * sin], axis=1)

    for h in range(RET_HEADS):
        cols = slice(h * dh, (h + 1) * dh)
        q = rope(q_ref[0, :, cols])
        k = rope(k_ref[0, :, cols]) * (dh ** -0.5)
        qb = q.astype(bf16)
        kb = k.astype(bf16)
        vb = v_ref[0, :, cols].astype(bf16)
        attn = lax.dot_general(qb, kb, NT_DIMS, preferred_element_type=f32) * din_ref[h]
        inner = jnp.dot(attn.astype(bf16), vb, preferred_element_type=f32)
        state = state_ref[h]
        cross = jnp.dot(qb, state.astype(bf16), preferred_element_type=f32) * xi_ref[h]
        kz = (k * zeta_ref[h]).astype(bf16)
        state_ref[h] = state * chunk_decay[h] + lax.dot_general(kz, vb, TN_DIMS, preferred_element_type=f32)
        y = inner + cross
        mu = jnp.mean(y, axis=-1, keepdims=True)
        yc = y - mu
        var = jnp.mean(yc * yc, axis=-1, keepdims=True)
        yn = yc * lax.rsqrt(var + LN_EPS) * gn_ref[:, cols]
        g = g_ref[0, :, cols]
        o_ref[0, :, cols] = (g * jax.nn.sigmoid(g) * yn).astype(o_ref.dtype)


def _retention(proj, gn_g, batch, seq):
    c = min(RET_BLOCK, seq)
    dh = RET_HEAD_DIM
    w = RET_WIDTH
    decay_in, xi_b, zeta_b, chunk_decay = _retention_tables(c)
    cos, sin = _rope_tables(seq, dh // 2)
    col = lambda j: pl.BlockSpec((1, c, w), lambda b, s, j=j: (b, s, j))
    whole = lambda shape: pl.BlockSpec(shape, lambda b, s: (0,) * len(shape))
    return pl.pallas_call(
        functools.partial(_retention_kernel, chunk_decay=[float(x) for x in chunk_decay]),
        grid=(batch, seq // c),
        in_specs=[col(0), col(1), col(2), col(3),
                  pl.BlockSpec((c, dh // 2), lambda b, s: (s, 0)),
                  pl.BlockSpec((c, dh // 2), lambda b, s: (s, 0)),
                  whole((RET_HEADS, c, c)), whole((RET_HEADS, c, dh)), whole((RET_HEADS, c, dh)),
                  whole((1, w))],
        out_specs=pl.BlockSpec((1, c, w), lambda b, s: (b, s, 0)),
        out_shape=jax.ShapeDtypeStruct((batch, seq, w), bf16),
        scratch_shapes=[pltpu.VMEM((RET_HEADS, dh, dh), f32)],
        compiler_params=_params(("parallel", "arbitrary")),
        name="retention",
    )(proj, proj, proj, proj, jnp.asarray(cos, f32), jnp.asarray(sin, f32),
      decay_in, xi_b, zeta_b, gn_g.reshape(1, w))


CONV_ROWS = 128
CONV_HIST = 32
CONV_SUB = 32


def _conv_kernel(ga_ref, gb_ref, w_ref, b_ref, lg_ref, lb_ref, o_ref, ush):
    total = CONV_HIST + CONV_ROWS

    @pl.when(pl.program_id(1) == 0)
    def _():
        ush[0, 0:CONV_HIST, :] = jnp.zeros((CONV_HIST, CONV_CHANNELS), f32)

    ush[0, CONV_HIST:total, :] = ga_ref[0] * jax.nn.sigmoid(gb_ref[0])
    for r in range(1, SUBLANES):
        ush[r, 0:total - SUBLANES, :] = ush[0, r:r + total - SUBLANES, :]
    first = CONV_HIST - (CONV_TAPS - 1)
    for r0 in range(0, CONV_ROWS, CONV_SUB):
        acc = jnp.zeros((CONV_SUB, CONV_CHANNELS), f32)
        for t in range(CONV_TAPS):
            d = first + t
            lo = r0 + d - d % SUBLANES
            acc = acc + w_ref[t:t + 1, :] * ush[d % SUBLANES, lo:lo + CONV_SUB, :]
        y = _layer_norm(acc + b_ref[...], lg_ref[...], lb_ref[...])
        o_ref[0, r0:r0 + CONV_SUB, :] = (y * jax.nn.sigmoid(y)).astype(o_ref.dtype)
    ush[0, 0:CONV_HIST, :] = ush[0, CONV_ROWS:total, :]


def _conformer_conv(proj, conv_w, conv_b, ln_g, ln_b, batch, seq):
    c = CONV_CHANNELS
    a_blk = 4 * RET_WIDTH // c
    row = pl.BlockSpec((1, c), lambda b, s: (0, 0))
    return pl.pallas_call(
        _conv_kernel,
        grid=(batch, seq // CONV_ROWS),
        in_specs=[pl.BlockSpec((1, CONV_ROWS, c), lambda b, s: (b, s, a_blk)),
                  pl.BlockSpec((1, CONV_ROWS, c), lambda b, s: (b, s, a_blk + 1)),
                  pl.BlockSpec((CONV_TAPS, c), lambda b, s: (0, 0)),
                  row, row, row],
        out_specs=pl.BlockSpec((1, CONV_ROWS, c), lambda b, s: (b, s, 0)),
        out_shape=jax.ShapeDtypeStruct((batch, seq, c), bf16),
        scratch_shapes=[pltpu.VMEM((SUBLANES, CONV_HIST + CONV_ROWS, c), f32)],
        compiler_params=_params(("parallel", "arbitrary")),
        name="conformer_conv",
    )(proj, proj, conv_w, conv_b.reshape(1, c), ln_g.reshape(1, c), ln_b.reshape(1, c))


def _route(sel, aff):
    epg = EXPERTS_PER_GROUP
    gscore = []
    for g in range(N_GROUPS):
        v = sel[g * epg:(g + 1) * epg]
        best = None
        for a in range(epg):
            for b in range(a + 1, epg):
                s = v[a] + v[b]
                best = s if best is None else jnp.maximum(best, s)
        gscore.append(best)
    gmax = functools.reduce(jnp.maximum, gscore)
    taken = None
    gsel = []
    for g in range(N_GROUPS):
        hit = gscore[g] == gmax
        if taken is not None:
            hit = jnp.logical_and(hit, jnp.logical_not(taken))
        taken = hit if taken is None else jnp.logical_or(taken, hit)
        gsel.append(hit)

    def pick(rows, j):
        out = rows[(N_GROUPS - 1) * epg + j]
        for g in range(N_GROUPS - 2, -1, -1):
            out = jnp.where(gsel[g], rows[g * epg + j], out)
        return out

    x = [pick(sel, j) for j in range(epg)]
    a = [pick(aff, j) for j in range(epg)]
    base = jnp.zeros_like(x[0], dtype=jnp.int32)
    for g in range(1, N_GROUPS):
        base = jnp.where(gsel[g], g * epg, base)

    def first_max(vals, excluded):
        vmax = functools.reduce(jnp.maximum, vals)
        taken = None
        hits = []
        for j in range(epg):
            hit = vals[j] == vmax
            if excluded is not None:
                hit = jnp.logical_and(hit, jnp.logical_not(excluded[j]))
            if taken is not None:
                hit = jnp.logical_and(hit, jnp.logical_not(taken))
            taken = hit if taken is None else jnp.logical_or(taken, hit)
            hits.append(hit)
        return hits

    h1 = first_max(x, None)
    x2 = [jnp.where(h1[j], -jnp.inf, x[j]) for j in range(epg)]
    h2 = first_max(x2, h1)

    def gather(hits):
        idx = base
        val = jnp.zeros_like(a[0])
        for j in range(epg):
            idx = jnp.where(hits[j], base + j, idx)
            val = jnp.where(hits[j], a[j], val)
        return idx, val

    e0, a0 = gather(h1)
    e1, a1 = gather(h2)
    tot = a0 + a1
    return e0, e1, a0 / tot, a1 / tot


def _router_weights(router_w):
    d, e = router_w.shape
    w = router_w.astype(f32)
    w_hi = w.astype(bf16)
    w_lo = (w - w_hi.astype(f32)).astype(bf16)
    top = jnp.concatenate([w_hi, w_lo, jnp.zeros((d, LANES - 2 * e), bf16)], axis=1)
    bot = jnp.concatenate([w_hi, jnp.zeros((d, LANES - e), bf16)], axis=1)
    return jnp.concatenate([top, bot], axis=0)


def _proj_norm_router_kernel(a_ref, w_ref, h_ref, g_ref, b_ref, rw_ref, rb_ref,
                             hn_ref, hb_ref, ridx_ref, rwgt_ref, *, alpha):
    m = jnp.dot(a_ref[...], w_ref[...], preferred_element_type=f32)
    hn = _layer_norm(alpha * h_ref[...] + m, g_ref[...], b_ref[...])
    hn_ref[...] = hn
    hi = hn.astype(bf16)
    hb_ref[...] = hi
    lo = (hn - hi.astype(f32)).astype(bf16)
    d = hn.shape[1]
    parts = (jnp.dot(hi, rw_ref[0:d, :], preferred_element_type=f32)
             + jnp.dot(lo, rw_ref[d:2 * d, :], preferred_element_type=f32))
    parts = parts.T
    logits = parts[0:N_EXPERTS] + parts[N_EXPERTS:2 * N_EXPERTS]
    aff = jax.nn.sigmoid(logits)
    sel = aff + rb_ref[...]
    e0, e1, w0, w1 = _route([sel[e:e + 1, :] for e in range(N_EXPERTS)],
                            [aff[e:e + 1, :] for e in range(N_EXPERTS)])
    ridx_ref[0:1, :] = e0
    ridx_ref[1:2, :] = e1
    rwgt_ref[0:1, :] = w0
    rwgt_ref[1:2, :] = w1


def _proj_norm_router(a, w, h, ln_g, ln_b, router_ws, router_b, alpha, tm=512):
    t, k = a.shape
    d = w.shape[1]
    tm = min(tm, t)
    row = pl.BlockSpec((1, d), lambda i: (0, 0))
    return pl.pallas_call(
        functools.partial(_proj_norm_router_kernel, alpha=alpha),
        grid=(t // tm,),
        in_specs=[pl.BlockSpec((tm, k), lambda i: (i, 0)),
                  pl.BlockSpec((k, d), lambda i: (0, 0)),
                  pl.BlockSpec((tm, d), lambda i: (i, 0)),
                  row, row,
                  pl.BlockSpec((2 * d, LANES), lambda i: (0, 0)),
                  pl.BlockSpec((N_EXPERTS, 1), lambda i: (0, 0))],
        out_specs=[pl.BlockSpec((tm, d), lambda i: (i, 0)),
                   pl.BlockSpec((tm, d), lambda i: (i, 0)),
                   pl.BlockSpec((2, tm), lambda i: (0, i)),
                   pl.BlockSpec((2, tm), lambda i: (0, i))],
        out_shape=[jax.ShapeDtypeStruct((t, d), f32), jax.ShapeDtypeStruct((t, d), bf16),
                   jax.ShapeDtypeStruct((2, t), jnp.int32), jax.ShapeDtypeStruct((2, t), f32)],
        compiler_params=_params(("parallel",)),
        name="proj_norm_router",
    )(a, w, h, ln_g.reshape(1, d), ln_b.reshape(1, d), router_ws, router_b.reshape(N_EXPERTS, 1))


MOE_TILE = 512


def _experts_kernel(te_ref, tv_ref, x_ref, wgu_ref, wd_ref, o_ref, wgu_b, wd_b):
    i = pl.program_id(0)
    prev = te_ref[jnp.maximum(i - 1, 0)]

    @pl.when(jnp.logical_or(i == 0, te_ref[i] != prev))
    def _():
        wgu_b[...] = wgu_ref[0].astype(bf16)
        wd_b[...] = wd_ref[0].astype(bf16)

    @pl.when(tv_ref[i] > 0)
    def _():
        hgu = jnp.dot(x_ref[...], wgu_b[...], preferred_element_type=f32)
        a, b = hgu[:, :EXPERT_FF], hgu[:, EXPERT_FF:]
        act = (a * jax.nn.sigmoid(a) * b).astype(bf16)
        o_ref[...] = jnp.dot(act, wd_b[...], preferred_element_type=f32).astype(o_ref.dtype)

    @pl.when(tv_ref[i] == 0)
    def _():
        o_ref[...] = jnp.zeros_like(o_ref)


def _experts(xs, w_gu, w_down, layer, tile_expert, tile_valid):
    r, d = xs.shape
    ff2 = w_gu.shape[3]
    grid_spec = pltpu.PrefetchScalarGridSpec(
        num_scalar_prefetch=2,
        grid=(r // MOE_TILE,),
        in_specs=[pl.BlockSpec((MOE_TILE, d), lambda i, te, tv: (i, 0)),
                  pl.BlockSpec((None, 1, d, ff2), lambda i, te, tv: (layer, te[i], 0, 0)),
                  pl.BlockSpec((None, 1, ff2 // 2, d), lambda i, te, tv: (layer, te[i], 0, 0))],
        out_specs=pl.BlockSpec((MOE_TILE, d), lambda i, te, tv: (i, 0)),
        scratch_shapes=[pltpu.VMEM((d, ff2), bf16), pltpu.VMEM((ff2 // 2, d), bf16)],
    )
    return pl.pallas_call(
        _experts_kernel,
        grid_spec=grid_spec,
        out_shape=jax.ShapeDtypeStruct((r, d), bf16),
        compiler_params=_params(("arbitrary",)),
        name="experts",
    )(tile_expert, tile_valid, xs, w_gu, w_down)


def _dispatch(ridx):
    t = ridx.shape[1]
    slots = 2 * t
    es = ridx.T.reshape(slots)
    experts = jnp.arange(N_EXPERTS, dtype=jnp.int32)
    onehot = (es[:, None] == experts[None, :]).astype(jnp.int32)
    csum = jnp.cumsum(onehot, axis=0)
    rank = jnp.sum((csum - onehot) * onehot, axis=1)
    counts = csum[-1]
    tiles_per = (counts + MOE_TILE - 1) // MOE_TILE
    tile_end = jnp.cumsum(tiles_per)
    start = (tile_end - tiles_per) * MOE_TILE
    dest = start[es] + rank
    n_tiles = slots // MOE_TILE + N_EXPERTS
    tile_ids = jnp.arange(n_tiles, dtype=jnp.int32)
    tile_valid = (tile_ids < tile_end[-1]).astype(jnp.int32)
    tile_expert = jnp.sum((tile_ids[:, None] >= tile_end[None, :]).astype(jnp.int32), axis=1)
    last_used = jnp.max(jnp.where(counts > 0, experts, 0))
    tile_expert = jnp.minimum(tile_expert, last_used)
    rows = n_tiles * MOE_TILE
    tok = jnp.zeros((rows,), jnp.int32).at[dest].set(jnp.arange(slots, dtype=jnp.int32) // 2)
    return tok, dest, tile_expert, tile_valid


def _combine_norm_kernel(h_ref, y0_ref, y1_ref, w0_ref, w1_ref, g_ref, b_ref, o_ref, ob_ref, *, alpha):
    f = w0_ref[...] * y0_ref[...].astype(f32) + w1_ref[...] * y1_ref[...].astype(f32)
    hn = _layer_norm(alpha * h_ref[...] + f, g_ref[...], b_ref[...])
    o_ref[...] = hn
    ob_ref[...] = hn.astype(bf16)


def _combine_norm(h, y, w0, w1, ln_g, ln_b, alpha, tm=512):
    t, d = h.shape
    tm = min(tm, t)
    nt = t // tm
    blk = pl.BlockSpec((tm, d), lambda i: (i, 0))
    col = pl.BlockSpec((tm, 1), lambda i: (i, 0))
    row = pl.BlockSpec((1, d), lambda i: (0, 0))
    return pl.pallas_call(
        functools.partial(_combine_norm_kernel, alpha=alpha),
        grid=(nt,),
        in_specs=[blk, blk, pl.BlockSpec((tm, d), lambda i: (i + nt, 0)), col, col, row, row],
        out_specs=[blk, blk],
        out_shape=[jax.ShapeDtypeStruct((t, d), f32), jax.ShapeDtypeStruct((t, d), bf16)],
        compiler_params=_params(("parallel",)),
        name="combine_norm",
    )(h, y, y, w0, w1, ln_g.reshape(1, d), ln_b.reshape(1, d))


def _moe(hn, hb, ridx, rwgt, w_gu, w_down, layer, ln_g, ln_b, alpha):
    t, d = hn.shape
    tok, dest, tile_expert, tile_valid = _dispatch(ridx)
    xs = hb.at[tok].get(mode="promise_in_bounds")
    ys = _experts(xs, w_gu, w_down, layer, tile_expert, tile_valid)
    y = ys.at[dest.reshape(t, 2).T.reshape(2 * t)].get(mode="promise_in_bounds")
    return _combine_norm(hn, y, rwgt[0].reshape(t, 1), rwgt[1].reshape(t, 1), ln_g, ln_b, alpha)


DSA_Q = DSA_HEADS * DSA_HEAD_DIM
DSA_KV = DSA_KV_HEADS * DSA_HEAD_DIM
DSA_QI = IDX_HEADS * IDX_DIM
DSA_MAIN = DSA_Q + 2 * DSA_KV + DSA_QI


def _lane_iota(shape):
    return lax.broadcasted_iota(jnp.int32, shape, 1)


def _rope128(x, cos2, sin2):
    return x * cos2 + pltpu.roll(x, LANES // 2, 1) * sin2


def _rope64(x, cos4, sin4):
    lane = _lane_iota(x.shape)
    partner = jnp.where(lane % IDX_DIM < IDX_DIM // 2,
                        pltpu.roll(x, LANES - IDX_DIM // 2, 1), pltpu.roll(x, IDX_DIM // 2, 1))
    return x * cos4 + partner * sin4


def _dsa_prep_kernel(p_ref, t_ref, c2_ref, s2_ref, c4_ref, s4_ref,
                     q_ref, k_ref, v_ref, qi_ref, ki_ref):
    c2, s2, c4, s4 = c2_ref[...], s2_ref[...], c4_ref[...], s4_ref[...]
    qscale = LOG2E * DSA_HEAD_DIM ** -0.5
    for j in range(DSA_Q // LANES):
        x = p_ref[0, :, j * LANES:(j + 1) * LANES]
        q_ref[0, :, j * LANES:(j + 1) * LANES] = (_rope128(x, c2, s2) * qscale).astype(bf16)
    for j in range(DSA_KV // LANES):
        o = DSA_Q + j * LANES
        k_ref[0, :, j * LANES:(j + 1) * LANES] = _rope128(p_ref[0, :, o:o + LANES], c2, s2).astype(bf16)
        o = DSA_Q + DSA_KV + j * LANES
        v_ref[0, :, j * LANES:(j + 1) * LANES] = p_ref[0, :, o:o + LANES].astype(bf16)
    lane = _lane_iota((p_ref.shape[1], LANES))
    low = lane < IDX_DIM
    iscale = IDX_DIM ** -0.5
    for j in range(DSA_QI // LANES):
        o = DSA_Q + 2 * DSA_KV + j * LANES
        r = _rope64(p_ref[0, :, o:o + LANES], c4, s4) * iscale
        qi_ref[0, 2 * j] = jnp.where(low, r, 0.0).astype(bf16)
        qi_ref[0, 2 * j + 1] = jnp.where(low, pltpu.roll(r, LANES // 2, 1), 0.0).astype(bf16)
    ki_ref[0] = jnp.where(low, _rope64(t_ref[0], c4, s4), 0.0).astype(bf16)


def _dsa_prep(main, tail, batch, seq, ts=256):
    ts = min(ts, seq)
    c, s = _rope_tables(seq, DSA_HEAD_DIM // 2)
    c2 = jnp.asarray(np.concatenate([c, c], axis=1), f32)
    s2 = jnp.asarray(np.concatenate([-s, s], axis=1), f32)
    c, s = _rope_tables(seq, IDX_DIM // 2)
    c4 = jnp.asarray(np.concatenate([c, c, c, c], axis=1), f32)
    s4 = jnp.asarray(np.concatenate([-s, s, -s, s], axis=1), f32)
    tab = pl.BlockSpec((ts, LANES), lambda b, i: (i, 0))
    out = lambda w: pl.BlockSpec((1, ts, w), lambda b, i: (b, i, 0))
    return pl.pallas_call(
        _dsa_prep_kernel,
        grid=(batch, seq // ts),
        in_specs=[out(DSA_MAIN), out(LANES), tab, tab, tab, tab],
        out_specs=[out(DSA_Q), out(DSA_KV), out(DSA_KV),
                   pl.BlockSpec((1, IDX_HEADS, ts, LANES), lambda b, i: (b, 0, i, 0)), out(LANES)],
        out_shape=[jax.ShapeDtypeStruct((batch, seq, DSA_Q), bf16),
                   jax.ShapeDtypeStruct((batch, seq, DSA_KV), bf16),
                   jax.ShapeDtypeStruct((batch, seq, DSA_KV), bf16),
                   jax.ShapeDtypeStruct((batch, IDX_HEADS, seq, LANES), bf16),
                   jax.ShapeDtypeStruct((batch, seq, LANES), bf16)],
        compiler_params=_params(("parallel", "parallel")),
        name="dsa_prep",
    )(main, tail, c2, s2, c4, s4)


SEL_ROWS = 256
SEL_CHUNK = 512
SEL_SUB = 256


def _select_kernel(qi_ref, ki_ref, t_ref, bias_ref, key_ref, wb_ref, *, n_sel, n_chunks):
    tq, kc = SEL_ROWS, SEL_CHUNK
    qt = pl.program_id(1)
    q0 = qt * tq
    n_act = lax.div(qt, kc // tq) + 1
    qpos = q0 + lax.broadcasted_iota(jnp.int32, (tq, 1), 0)
    wscale = IDX_HEADS ** -0.5

    for h in range(IDX_HEADS):
        wb_ref[h] = jnp.broadcast_to(t_ref[0, :, IDX_DIM + h:IDX_DIM + h + 1] * wscale, (tq, LANES))
    q_all = qi_ref[0].reshape(IDX_HEADS * tq, LANES)

    def score_chunk(c, carry):
        off = pl.multiple_of(c * kc, kc)
        for s in range(kc // SEL_SUB):
            ki = ki_ref[0, pl.ds(off + s * SEL_SUB, SEL_SUB), :]
            r = lax.dot_general(q_all, ki, NT_DIMS, preferred_element_type=f32)
            acc = jnp.zeros((tq, SEL_SUB), f32)
            for h in range(IDX_HEADS):
                w = jnp.concatenate([wb_ref[h]] * (SEL_SUB // LANES), axis=1)
                acc = acc + jnp.maximum(r[h * tq:(h + 1) * tq], 0.0) * w
            kpos = off + s * SEL_SUB + _lane_iota((tq, SEL_SUB))
            sc = jnp.where(kpos <= qpos, acc + 0.0, -jnp.inf)
            bits = pltpu.bitcast(sc, jnp.int32)
            key_ref[c, :, s * SEL_SUB:(s + 1) * SEL_SUB] = bits ^ ((bits >> 31) & 0x7FFFFFFF)
        return carry

    lax.fori_loop(0, n_act, score_chunk, 0)

    def count_ge(cand):
        def body(c, acc):
            hit = jnp.where(key_ref[c] >= cand, 1.0, 0.0)
            for s in range(kc // LANES):
                acc = acc + hit[:, s * LANES:(s + 1) * LANES]
            return acc
        acc = lax.fori_loop(0, n_act, body, jnp.zeros((tq, LANES), f32))
        return jnp.sum(acc, axis=1, keepdims=True)

    kf = float(n_sel)
    ans = jnp.where(count_ge(jnp.zeros((tq, 1), jnp.int32)) >= kf, 0, INT_MIN).astype(jnp.int32)

    def bit_step(i, ans):
        cand = ans | jnp.left_shift(jnp.int32(1), 30 - i)
        return jnp.where(count_ge(cand) >= kf, cand, ans)

    ans = lax.fori_loop(0, 31, bit_step, ans)
    need = kf - count_ge(ans + 1)
    ans = jnp.where(qpos < n_sel, INT_MIN, ans)

    upper = (lax.broadcasted_iota(jnp.int32, (SEL_SUB, SEL_SUB), 0)
             < lax.broadcasted_iota(jnp.int32, (SEL_SUB, SEL_SUB), 1)).astype(bf16)

    def emit_chunk(c, seen):
        off = pl.multiple_of(c * kc, kc)
        for s in range(kc // SEL_SUB):
            key = key_ref[c, :, s * SEL_SUB:(s + 1) * SEL_SUB]
            eq = key == ans
            eqf = jnp.where(eq, 1.0, 0.0)
            before = seen + jnp.dot(eqf.astype(bf16), upper, preferred_element_type=f32)
            keep = jnp.logical_or(key > ans, jnp.logical_and(eq, before < need))
            kpos = off + s * SEL_SUB + _lane_iota((tq, SEL_SUB))
            keep = jnp.logical_and(keep, kpos <= qpos)
            bias_ref[0, c, :, s * SEL_SUB:(s + 1) * SEL_SUB] = jnp.where(keep, 0.0, NEG_BIG).astype(bf16)
            seen = seen + jnp.sum(eqf, axis=1, keepdims=True)
        return seen

    lax.fori_loop(0, n_act, emit_chunk, jnp.zeros((tq, 1), f32))

    def fill_chunk(c, carry):
        bias_ref[0, c] = jnp.full((tq, kc), NEG_BIG, bf16)
        return carry

    lax.fori_loop(n_act, n_chunks, fill_chunk, 0)


def _select(qi, ki, tail, batch, seq, n_sel):
    n_chunks = seq // SEL_CHUNK
    assert SEL_CHUNK >= n_sel and SEL_CHUNK % SEL_ROWS == 0
    return pl.pallas_call(
        functools.partial(_select_kernel, n_sel=n_sel, n_chunks=n_chunks),
        grid=(batch, seq // SEL_ROWS),
        in_specs=[pl.BlockSpec((1, IDX_HEADS, SEL_ROWS, LANES), lambda b, i: (b, 0, i, 0)),
                  pl.BlockSpec((1, seq, LANES), lambda b, i: (b, 0, 0)),
                  pl.BlockSpec((1, SEL_ROWS, LANES), lambda b, i: (b, i, 0))],
        out_specs=pl.BlockSpec((1, n_chunks, SEL_ROWS, SEL_CHUNK), lambda b, i: (b, 0, i, 0)),
        out_shape=jax.ShapeDtypeStruct((batch, n_chunks, seq, SEL_CHUNK), bf16),
        scratch_shapes=[pltpu.VMEM((n_chunks, SEL_ROWS, SEL_CHUNK), jnp.int32),
                        pltpu.VMEM((IDX_HEADS, SEL_ROWS, LANES), f32)],
        compiler_params=_params(("parallel", "parallel")),
        name="dsa_select",
    )(qi, ki, tail)


ATT_TILE = SEL_CHUNK
ATT_GROUP = DSA_HEADS // DSA_KV_HEADS


def _attention_kernel(q_ref, k_ref, v_ref, b_ref, o_ref, m_ref, acc_ref):
    i = pl.program_id(2)
    j = pl.program_id(3)
    dh = DSA_HEAD_DIM
    t = ATT_TILE

    @pl.when(j == 0)
    def _():
        m_ref[...] = jnp.full(m_ref.shape, NEG_BIG, f32)
        acc_ref[...] = jnp.zeros_like(acc_ref)

    @pl.when(j <= i)
    def _():
        k = k_ref[0]
        v_ones = jnp.concatenate([v_ref[0], jnp.ones((t, dh), bf16)], axis=1)
        for h in range(ATT_GROUP):
            s = lax.dot_general(q_ref[0, :, h * dh:(h + 1) * dh], k, NT_DIMS, preferred_element_type=f32)
            s = s + b_ref[0, 0].astype(f32)
            m_old = m_ref[h]
            m_new = jnp.maximum(m_old, jnp.max(s, axis=1, keepdims=True))
            p = jnp.concatenate([jnp.exp2(s[:, c * LANES:(c + 1) * LANES] - m_new) for c in range(t // LANES)],
                                axis=1)
            scale = jnp.exp2(m_old - m_new)
            pv = jnp.dot(p.astype(bf16), v_ones, preferred_element_type=f32)
            acc_ref[h] = jnp.concatenate([scale, scale], axis=1) * acc_ref[h] + pv
            m_ref[h] = m_new

    @pl.when(j == i)
    def _():
        for h in range(ATT_GROUP):
            acc = acc_ref[h]
            o_ref[0, :, h * dh:(h + 1) * dh] = (acc[:, :dh] / acc[:, dh:]).astype(o_ref.dtype)


def _attention(q, k, v, bias, batch, seq):
    t = ATT_TILE
    n = seq // t
    dh = DSA_HEAD_DIM
    gw = ATT_GROUP * dh
    return pl.pallas_call(
        _attention_kernel,
        grid=(batch, DSA_KV_HEADS, n, n),
        in_specs=[pl.BlockSpec((1, t, gw), lambda b, g, i, j: (b, i, g)),
                  pl.BlockSpec((1, t, dh), lambda b, g, i, j: (b, jnp.minimum(i, j), g)),
                  pl.BlockSpec((1, t, dh), lambda b, g, i, j: (b, jnp.minimum(i, j), g)),
                  pl.BlockSpec((1, 1, t, t), lambda b, g, i, j: (b, jnp.minimum(i, j), i, 0))],
        out_specs=pl.BlockSpec((1, t, gw), lambda b, g, i, j: (b, i, g)),
        out_shape=jax.ShapeDtypeStruct((batch, seq, DSA_Q), bf16),
        scratch_shapes=[pltpu.VMEM((ATT_GROUP, t, LANES), f32), pltpu.VMEM((ATT_GROUP, t, 2 * dh), f32)],
        compiler_params=_params(("parallel", "parallel", "parallel", "arbitrary")),
        name="dsa_attention",
    )(q, k, v, bias)


def _even_mixer(hb, w_in, gn_g, conv_w, conv_b, conv_ln_g, conv_ln_b, batch, seq):
    proj = _matmul(hb, w_in, f32).reshape(batch, seq, -1)
    ret = _retention(proj, gn_g, batch, seq)
    conv = _conformer_conv(proj, conv_w, conv_b, conv_ln_g, conv_ln_b, batch, seq)
    return jnp.concatenate([ret, conv], axis=-1).reshape(batch * seq, -1)


def _odd_mixer(hb, w_in, batch, seq):
    wt = w_in.T
    main = _matmul_nt(hb, wt, f32, 0, DSA_MAIN).reshape(batch, seq, DSA_MAIN)
    tail = _matmul_nt(hb, wt, f32, DSA_MAIN, LANES).reshape(batch, seq, LANES)
    q, k, v, qi, ki = _dsa_prep(main, tail, batch, seq)
    bias = _select(qi, ki, tail, batch, seq, min(TOPK_MAX, seq // 4))
    return _attention(q, k, v, bias, batch, seq).reshape(batch * seq, DSA_Q)


def kernel(x, even_w_in, even_ret_gn_g, even_conv_w, even_conv_b, even_conv_ln_g, even_conv_ln_b, even_w_out,
           odd_w_in, odd_w_out, mix_ln_g, mix_ln_b, moe_w_gu, moe_w_down, ffn_ln_g, ffn_ln_b, router_w, router_b):
    batch, seq, d = x.shape
    depth = mix_ln_g.shape[0]
    alpha = (2 * depth) ** 0.25
    router_ws = _router_weights(router_w)
    h = x.reshape(batch * seq, d).astype(f32)
    hb = h.astype(bf16)
    for layer in range(depth):
        i = layer // 2
        if layer % 2 == 0:
            mixed = _even_mixer(hb, even_w_in[i], even_ret_gn_g[i], even_conv_w[i], even_conv_b[i],
                                even_conv_ln_g[i], even_conv_ln_b[i], batch, seq)
            w_out = even_w_out[i]
        else:
            mixed = _odd_mixer(hb, odd_w_in[i], batch, seq)
            w_out = odd_w_out[i]
        hn, hnb, ridx, rwgt = _proj_norm_router(mixed, w_out.astype(bf16), h, mix_ln_g[layer], mix_ln_b[layer],
                                                router_ws, router_b.astype(f32), alpha)
        h, hb = _moe(hn, hnb, ridx, rwgt, moe_w_gu, moe_w_down, layer,
                     ffn_ln_g[layer], ffn_ln_b[layer], alpha)
    return h.reshape(batch, seq, d).astype(x.dtype)
```

```python
import functools

import numpy as np
import jax
import jax.numpy as jnp
from jax import lax
from jax.experimental import pallas as pl
from jax.experimental.pallas import tpu as pltpu

f32 = jnp.float32
bf16 = jnp.bfloat16

ROPE_THETA = 10000.0
LN_EPS = 1e-5
RET_HEADS = 4
RET_HEAD_DIM = 256
RET_WIDTH = RET_HEADS * RET_HEAD_DIM
RET_BLOCK = 256
CONV_CHANNELS = 1024
CONV_TAPS = 31
DSA_HEADS = 16
DSA_HEAD_DIM = 128
DSA_KV_HEADS = 4
IDX_HEADS = 16
IDX_DIM = 64
TOPK_MAX = 256
N_EXPERTS = 16
N_GROUPS = 4
EXPERTS_PER_GROUP = N_EXPERTS // N_GROUPS
EXPERT_FF = 512

LANES = 128
SUBLANES = 8
VMEM_LIMIT = 56 * 1024 * 1024
NEG_BIG = -1e30
LOG2E = 1.4426950408889634
INT_MIN = -2 ** 31

NT_DIMS = (((1,), (1,)), ((), ()))
TN_DIMS = (((0,), (0,)), ((), ()))


def _params(sem):
    return pltpu.CompilerParams(dimension_semantics=sem, vmem_limit_bytes=VMEM_LIMIT)


def _layer_norm(z, g, b):
    mu = jnp.mean(z, axis=-1, keepdims=True)
    zc = z - mu
    var = jnp.mean(zc * zc, axis=-1, keepdims=True)
    return zc * lax.rsqrt(var + LN_EPS) * g + b


def _mm_kernel(x_ref, w_ref, o_ref, wb_ref):
    @pl.when(pl.program_id(1) == 0)
    def _():
        wb_ref[...] = w_ref[...].astype(bf16)

    o_ref[...] = jnp.dot(x_ref[...].astype(bf16), wb_ref[...], preferred_element_type=f32).astype(o_ref.dtype)


def _matmul(x, w, out_dtype, n=None, tm=512, tn=1024):
    m, k = x.shape
    n = w.shape[1] if n is None else n
    tm, tn = min(tm, m), min(tn, n)
    assert m % tm == 0 and n % tn == 0
    return pl.pallas_call(
        _mm_kernel,
        grid=(n // tn, m // tm),
        in_specs=[pl.BlockSpec((tm, k), lambda j, i: (i, 0)),
                  pl.BlockSpec((k, tn), lambda j, i: (0, j))],
        out_specs=pl.BlockSpec((tm, tn), lambda j, i: (i, j)),
        out_shape=jax.ShapeDtypeStruct((m, n), out_dtype),
        scratch_shapes=[pltpu.VMEM((k, tn), bf16)],
        compiler_params=_params(("arbitrary", "arbitrary")),
        name="matmul",
    )(x, w)


def _mm_nt_kernel(x_ref, wt_ref, o_ref, wb_ref, *, valid):
    @pl.when(pl.program_id(1) == 0)
    def _():
        w = wt_ref[...]
        if valid < wt_ref.shape[0]:
            w = jnp.where(lax.broadcasted_iota(jnp.int32, w.shape, 0) < valid, w, 0.0)
        wb_ref[...] = w.astype(bf16)

    o_ref[...] = lax.dot_general(x_ref[...].astype(bf16), wb_ref[...], NT_DIMS,
                                 preferred_element_type=f32).astype(o_ref.dtype)


def _matmul_nt(x, wt, out_dtype, row0, n, tm=512, tn=1024):
    m, k = x.shape
    tm, tn = min(tm, m), min(tn, n)
    assert m % tm == 0 and n % tn == 0 and row0 % tn == 0
    valid = min(tn, wt.shape[0] - row0 - (n - tn))
    assert valid == tn or n == tn
    return pl.pallas_call(
        functools.partial(_mm_nt_kernel, valid=valid),
        grid=(n // tn, m // tm),
        in_specs=[pl.BlockSpec((tm, k), lambda j, i: (i, 0)),
                  pl.BlockSpec((tn, k), lambda j, i: (row0 // tn + j, 0))],
        out_specs=pl.BlockSpec((tm, tn), lambda j, i: (i, j)),
        out_shape=jax.ShapeDtypeStruct((m, n), out_dtype),
        scratch_shapes=[pltpu.VMEM((tn, k), bf16)],
        compiler_params=_params(("arbitrary", "arbitrary")),
        name="matmul_nt",
    )(x, wt)


def _retention_tables(chunk):
    h = np.arange(RET_HEADS, dtype=np.float64)
    log_g = np.log(1.0 - 2.0 ** (-5.0 - h))
    j = np.arange(chunk, dtype=np.float64)
    diff = j[:, None] - j[None, :]
    decay_in = np.where(diff[None] >= 0, np.exp(np.maximum(diff, 0.0)[None] * log_g[:, None, None]), 0.0)
    xi = np.exp((j[None, :] + 1.0) * log_g[:, None])
    zeta = np.exp((chunk - 1.0 - j[None, :]) * log_g[:, None])
    chunk_decay = np.exp(chunk * log_g)
    xi_b = np.broadcast_to(xi[:, :, None], (RET_HEADS, chunk, RET_HEAD_DIM))
    zeta_b = np.broadcast_to(zeta[:, :, None], (RET_HEADS, chunk, RET_HEAD_DIM))
    return jnp.asarray(decay_in, f32), jnp.asarray(xi_b, f32), jnp.asarray(zeta_b, f32), chunk_decay


def _rope_tables(seq, half):
    inv = ROPE_THETA ** (-np.arange(half, dtype=np.float64) / half)
    ang = np.arange(seq, dtype=np.float64)[:, None] * inv[None, :]
    return np.cos(ang), np.sin(ang)


def _retention_kernel(q_ref, k_ref, v_ref, g_ref, cos_ref, sin_ref, din_ref, xi_ref, zeta_ref,
                      gn_ref, o_ref, state_ref, *, chunk_decay):
    @pl.when(pl.program_id(1) == 0)
    def _():
        state_ref[...] = jnp.zeros_like(state_ref)

    cos = cos_ref[...]
    sin = sin_ref[...]
    dh = RET_HEAD_DIM
    half = dh // 2

    def rope(x):
        x1, x2 = x[:, :half], x[:, half:]
        return jnp.concatenate([x1 * cos - x2 * sin, x2 * cos + x1 * sin], axis=1)

    for h in range(RET_HEADS):
        cols = slice(h * dh, (h + 1) * dh)
        q = rope(q_ref[0, :, cols])
        k = rope(k_ref[0, :, cols]) * (dh ** -0.5)
        qb = q.astype(bf16)
        kb = k.astype(bf16)
        vb = v_ref[0, :, cols].astype(bf16)
        attn = lax.dot_general(qb, kb, NT_DIMS, preferred_element_type=f32) * din_ref[h]
        inner = jnp.dot(attn.astype(bf16), vb, preferred_element_type=f32)
        state = state_ref[h]
        cross = jnp.dot(qb, state.astype(bf16), preferred_element_type=f32) * xi_ref[h]
        kz = (k * zeta_ref[h]).astype(bf16)
        state_ref[h] = state * chunk_decay[h] + lax.dot_general(kz, vb, TN_DIMS, preferred_element_type=f32)
        y = inner + cross
        mu = jnp.mean(y, axis=-1, keepdims=True)
        yc = y - mu
        var = jnp.mean(yc * yc, axis=-1, keepdims=True)
        yn = yc * lax.rsqrt(var + LN_EPS) * gn_ref[:, cols]
        g = g_ref[0, :, cols]
        o_ref[0, :, cols] = (g * jax.nn.sigmoid(g) * yn).astype(o_ref.dtype)


def _retention(proj, gn_g, batch, seq):
    c = min(RET_BLOCK, seq)
    dh = RET_HEAD_DIM
    w = RET_WIDTH
    decay_in, xi_b, zeta_b, chunk_decay = _retention_tables(c)
    cos, sin = _rope_tables(seq, dh // 2)
    col = lambda j: pl.BlockSpec((1, c, w), lambda b, s, j=j: (b, s, j))
    whole = lambda shape: pl.BlockSpec(shape, lambda b, s: (0,) * len(shape))
    return pl.pallas_call(
        functools.partial(_retention_kernel, chunk_decay=[float(x) for x in chunk_decay]),
        grid=(batch, seq // c),
        in_specs=[col(0), col(1), col(2), col(3),
                  pl.BlockSpec((c, dh // 2), lambda b, s: (s, 0)),
                  pl.BlockSpec((c, dh // 2), lambda b, s: (s, 0)),
                  whole((RET_HEADS, c, c)), whole((RET_HEADS, c, dh)), whole((RET_HEADS, c, dh)),
                  whole((1, w))],
        out_specs=pl.BlockSpec((1, c, w), lambda b, s: (b, s, 0)),
        out_shape=jax.ShapeDtypeStruct((batch, seq, w), bf16),
        scratch_shapes=[pltpu.VMEM((RET_HEADS, dh, dh), f32)],
        compiler_params=_params(("parallel", "arbitrary")),
        name="retention",
    )(proj, proj, proj, proj, jnp.asarray(cos, f32), jnp.asarray(sin, f32),
      decay_in, xi_b, zeta_b, gn_g.reshape(1, w))


CONV_ROWS = 128
CONV_HIST = 32
CONV_SUB = 32


def _conv_kernel(ga_ref, gb_ref, w_ref, b_ref, lg_ref, lb_ref, o_ref, ush):
    total = CONV_HIST + CONV_ROWS

    @pl.when(pl.program_id(1) == 0)
    def _():
        ush[0, 0:CONV_HIST, :] = jnp.zeros((CONV_HIST, CONV_CHANNELS), f32)

    ush[0, CONV_HIST:total, :] = ga_ref[0] * jax.nn.sigmoid(gb_ref[0])
    for r in range(1, SUBLANES):
        ush[r, 0:total - SUBLANES, :] = ush[0, r:r + total - SUBLANES, :]
    first = CONV_HIST - (CONV_TAPS - 1)
    for r0 in range(0, CONV_ROWS, CONV_SUB):
        acc = jnp.zeros((CONV_SUB, CONV_CHANNELS), f32)
        for t in range(CONV_TAPS):
            d = first + t
            lo = r0 + d - d % SUBLANES
            acc = acc + w_ref[t:t + 1, :] * ush[d % SUBLANES, lo:lo + CONV_SUB, :]
        y = _layer_norm(acc + b_ref[...], lg_ref[...], lb_ref[...])
        o_ref[0, r0:r0 + CONV_SUB, :] = (y * jax.nn.sigmoid(y)).astype(o_ref.dtype)
    ush[0, 0:CONV_HIST, :] = ush[0, CONV_ROWS:total, :]


def _conformer_conv(proj, conv_w, conv_b, ln_g, ln_b, batch, seq):
    c = CONV_CHANNELS
    a_blk = 4 * RET_WIDTH // c
    row = pl.BlockSpec((1, c), lambda b, s: (0, 0))
    return pl.pallas_call(
        _conv_kernel,
        grid=(batch, seq // CONV_ROWS),
        in_specs=[pl.BlockSpec((1, CONV_ROWS, c), lambda b, s: (b, s, a_blk)),
                  pl.BlockSpec((1, CONV_ROWS, c), lambda b, s: (b, s, a_blk + 1)),
                  pl.BlockSpec((CONV_TAPS, c), lambda b, s: (0, 0)),
                  row, row, row],
        out_specs=pl.BlockSpec((1, CONV_ROWS, c), lambda b, s: (b, s, 0)),
        out_shape=jax.ShapeDtypeStruct((batch, seq, c), bf16),
        scratch_shapes=[pltpu.VMEM((SUBLANES, CONV_HIST + CONV_ROWS, c), f32)],
        compiler_params=_params(("parallel", "arbitrary")),
        name="conformer_conv",
    )(proj, proj, conv_w, conv_b.reshape(1, c), ln_g.reshape(1, c), ln_b.reshape(1, c))


def _route(sel, aff):
    epg = EXPERTS_PER_GROUP
    gscore = []
    for g in range(N_GROUPS):
        v = sel[g * epg:(g + 1) * epg]
        best = None
        for a in range(epg):
            for b in range(a + 1, epg):
                s = v[a] + v[b]
                best = s if best is None else jnp.maximum(best, s)
        gscore.append(best)
    gmax = functools.reduce(jnp.maximum, gscore)
    taken = None
    gsel = []
    for g in range(N_GROUPS):
        hit = gscore[g] == gmax
        if taken is not None:
            hit = jnp.logical_and(hit, jnp.logical_not(taken))
        taken = hit if taken is None else jnp.logical_or(taken, hit)
        gsel.append(hit)

    def pick(rows, j):
        out = rows[(N_GROUPS - 1) * epg + j]
        for g in range(N_GROUPS - 2, -1, -1):
            out = jnp.where(gsel[g], rows[g * epg + j], out)
        return out

    x = [pick(sel, j) for j in range(epg)]
    a = [pick(aff, j) for j in range(epg)]
    base = jnp.zeros_like(x[0], dtype=jnp.int32)
    for g in range(1, N_GROUPS):
        base = jnp.where(gsel[g], g * epg, base)

    def first_max(vals, excluded):
        vmax = functools.reduce(jnp.maximum, vals)
        taken = None
        hits = []
        for j in range(epg):
            hit = vals[j] == vmax
            if excluded is not None:
                hit = jnp.logical_and(hit, jnp.logical_not(excluded[j]))
            if taken is not None:
                hit = jnp.logical_and(hit, jnp.logical_not(taken))
            taken = hit if taken is None else jnp.logical_or(taken, hit)
            hits.append(hit)
        return hits

    h1 = first_max(x, None)
    x2 = [jnp.where(h1[j], -jnp.inf, x[j]) for j in range(epg)]
    h2 = first_max(x2, h1)

    def gather(hits):
        idx = base
        val = jnp.zeros_like(a[0])
        for j in range(epg):
            idx = jnp.where(hits[j], base + j, idx)
            val = jnp.where(hits[j], a[j], val)
        return idx, val

    e0, a0 = gather(h1)
    e1, a1 = gather(h2)
    tot = a0 + a1
    return e0, e1, a0 / tot, a1 / tot


def _router_weights(router_w):
    d, e = router_w.shape
    w = router_w.astype(f32)
    w_hi = w.astype(bf16)
    w_lo = (w - w_hi.astype(f32)).astype(bf16)
    top = jnp.concatenate([w_hi, w_lo, jnp.zeros((d, LANES - 2 * e), bf16)], axis=1)
    bot = jnp.concatenate([w_hi, jnp.zeros((d, LANES - e), bf16)], axis=1)
    return jnp.concatenate([top, bot], axis=0)


def _proj_norm_router_kernel(a_ref, w_ref, h_ref, g_ref, b_ref, rw_ref, rb_ref,
                             hn_ref, hb_ref, ridx_ref, rwgt_ref, *, alpha):
    m = jnp.dot(a_ref[...], w_ref[...], preferred_element_type=f32)
    hn = _layer_norm(alpha * h_ref[...] + m, g_ref[...], b_ref[...])
    hn_ref[...] = hn
    hi = hn.astype(bf16)
    hb_ref[...] = hi
    lo = (hn - hi.astype(f32)).astype(bf16)
    d = hn.shape[1]
    parts = (jnp.dot(hi, rw_ref[0:d, :], preferred_element_type=f32)
             + jnp.dot(lo, rw_ref[d:2 * d, :], preferred_element_type=f32))
    parts = parts.T
    logits = parts[0:N_EXPERTS] + parts[N_EXPERTS:2 * N_EXPERTS]
    aff = jax.nn.sigmoid(logits)
    sel = aff + rb_ref[...]
    e0, e1, w0, w1 = _route([sel[e:e + 1, :] for e in range(N_EXPERTS)],
                            [aff[e:e + 1, :] for e in range(N_EXPERTS)])
    ridx_ref[0:1, :] = e0
    ridx_ref[1:2, :] = e1
    rwgt_ref[0:1, :] = w0
    rwgt_ref[1:2, :] = w1


def _proj_norm_router(a, w, h, ln_g, ln_b, router_ws, router_b, alpha, tm=512):
    t, k = a.shape
    d = w.shape[1]
    tm = min(tm, t)
    row = pl.BlockSpec((1, d), lambda i: (0, 0))
    return pl.pallas_call(
        functools.partial(_proj_norm_router_kernel, alpha=alpha),
        grid=(t // tm,),
        in_specs=[pl.BlockSpec((tm, k), lambda i: (i, 0)),
                  pl.BlockSpec((k, d), lambda i: (0, 0)),
                  pl.BlockSpec((tm, d), lambda i: (i, 0)),
                  row, row,
                  pl.BlockSpec((2 * d, LANES), lambda i: (0, 0)),
                  pl.BlockSpec((N_EXPERTS, 1), lambda i: (0, 0))],
        out_specs=[pl.BlockSpec((tm, d), lambda i: (i, 0)),
                   pl.BlockSpec((tm, d), lambda i: (i, 0)),
                   pl.BlockSpec((2, tm), lambda i: (0, i)),
                   pl.BlockSpec((2, tm), lambda i: (0, i))],
        out_shape=[jax.ShapeDtypeStruct((t, d), f32), jax.ShapeDtypeStruct((t, d), bf16),
                   jax.ShapeDtypeStruct((2, t), jnp.int32), jax.ShapeDtypeStruct((2, t), f32)],
        compiler_params=_params(("parallel",)),
        name="proj_norm_router",
    )(a, w, h, ln_g.reshape(1, d), ln_b.reshape(1, d), router_ws, router_b.reshape(N_EXPERTS, 1))


MOE_TILE = 512


def _experts_kernel(te_ref, tv_ref, x_ref, wgu_ref, wd_ref, o_ref, wgu_b, wd_b):
    i = pl.program_id(0)
    prev = te_ref[jnp.maximum(i - 1, 0)]

    @pl.when(jnp.logical_or(i == 0, te_ref[i] != prev))
    def _():
        wgu_b[...] = wgu_ref[0].astype(bf16)
        wd_b[...] = wd_ref[0].astype(bf16)

    @pl.when(tv_ref[i] > 0)
    def _():
        hgu = jnp.dot(x_ref[...], wgu_b[...], preferred_element_type=f32)
        a, b = hgu[:, :EXPERT_FF], hgu[:, EXPERT_FF:]
        act = (a * jax.nn.sigmoid(a) * b).astype(bf16)
        o_ref[...] = jnp.dot(act, wd_b[...], preferred_element_type=f32).astype(o_ref.dtype)

    @pl.when(tv_ref[i] == 0)
    def _():
        o_ref[...] = jnp.zeros_like(o_ref)


def _experts(xs, w_gu, w_down, layer, tile_expert, tile_valid):
    r, d = xs.shape
    ff2 = w_gu.shape[3]
    grid_spec = pltpu.PrefetchScalarGridSpec(
        num_scalar_prefetch=2,
        grid=(r // MOE_TILE,),
        in_specs=[pl.BlockSpec((MOE_TILE, d), lambda i, te, tv: (i, 0)),
                  pl.BlockSpec((None, 1, d, ff2), lambda i, te, tv: (layer, te[i], 0, 0)),
                  pl.BlockSpec((None, 1, ff2 // 2, d), lambda i, te, tv: (layer, te[i], 0, 0))],
        out_specs=pl.BlockSpec((MOE_TILE, d), lambda i, te, tv: (i, 0)),
        scratch_shapes=[pltpu.VMEM((d, ff2), bf16), pltpu.VMEM((ff2 // 2, d), bf16)],
    )
    return pl.pallas_call(
        _experts_kernel,
        grid_spec=grid_spec,
        out_shape=jax.ShapeDtypeStruct((r, d), bf16),
        compiler_params=_params(("arbitrary",)),
        name="experts",
    )(tile_expert, tile_valid, xs, w_gu, w_down)


def _dispatch(ridx):
    t = ridx.shape[1]
    slots = 2 * t
    es = ridx.T.reshape(slots)
    experts = jnp.arange(N_EXPERTS, dtype=jnp.int32)
    onehot = (es[:, None] == experts[None, :]).astype(jnp.int32)
    csum = jnp.cumsum(onehot, axis=0)
    rank = jnp.sum((csum - onehot) * onehot, axis=1)
    counts = csum[-1]
    tiles_per = (counts + MOE_TILE - 1) // MOE_TILE
    tile_end = jnp.cumsum(tiles_per)
    start = (tile_end - tiles_per) * MOE_TILE
    dest = start[es] + rank
    n_tiles = slots // MOE_TILE + N_EXPERTS
    tile_ids = jnp.arange(n_tiles, dtype=jnp.int32)
    tile_valid = (tile_ids < tile_end[-1]).astype(jnp.int32)
    tile_expert = jnp.sum((tile_ids[:, None] >= tile_end[None, :]).astype(jnp.int32), axis=1)
    last_used = jnp.max(jnp.where(counts > 0, experts, 0))
    tile_expert = jnp.minimum(tile_expert, last_used)
    rows = n_tiles * MOE_TILE
    tok = (jnp.arange(rows, dtype=jnp.int32) % t).at[dest].set(jnp.arange(slots, dtype=jnp.int32) // 2)
    return tok, dest, tile_expert, tile_valid


def _combine_norm_kernel(h_ref, y0_ref, y1_ref, w0_ref, w1_ref, g_ref, b_ref, o_ref, ob_ref, *, alpha):
    f = w0_ref[...] * y0_ref[...].astype(f32) + w1_ref[...] * y1_ref[...].astype(f32)
    hn = _layer_norm(alpha * h_ref[...] + f, g_ref[...], b_ref[...])
    o_ref[...] = hn
    ob_ref[...] = hn.astype(bf16)


def _combine_norm(h, y, w0, w1, ln_g, ln_b, alpha, tm=512):
    t, d = h.shape
    tm = min(tm, t)
    nt = t // tm
    blk = pl.BlockSpec((tm, d), lambda i: (i, 0))
    col = pl.BlockSpec((tm, 1), lambda i: (i, 0))
    row = pl.BlockSpec((1, d), lambda i: (0, 0))
    return pl.pallas_call(
        functools.partial(_combine_norm_kernel, alpha=alpha),
        grid=(nt,),
        in_specs=[blk, blk, pl.BlockSpec((tm, d), lambda i: (i + nt, 0)), col, col, row, row],
        out_specs=[blk, blk],
        out_shape=[jax.ShapeDtypeStruct((t, d), f32), jax.ShapeDtypeStruct((t, d), bf16)],
        compiler_params=_params(("parallel",)),
        name="combine_norm",
    )(h, y, y, w0, w1, ln_g.reshape(1, d), ln_b.reshape(1, d))


def _moe(hn, hb, ridx, rwgt, w_gu, w_down, layer, ln_g, ln_b, alpha):
    t, d = hn.shape
    tok, dest, tile_expert, tile_valid = _dispatch(ridx)
    xs = hb.at[tok].get(mode="promise_in_bounds")
    ys = _experts(xs, w_gu, w_down, layer, tile_expert, tile_valid)
    y = ys.at[dest.reshape(t, 2).T.reshape(2 * t)].get(mode="promise_in_bounds")
    return _combine_norm(hn, y, rwgt[0].reshape(t, 1), rwgt[1].reshape(t, 1), ln_g, ln_b, alpha)


DSA_Q = DSA_HEADS * DSA_HEAD_DIM
DSA_KV = DSA_KV_HEADS * DSA_HEAD_DIM
DSA_QI = IDX_HEADS * IDX_DIM
DSA_MAIN = DSA_Q + 2 * DSA_KV + DSA_QI


def _lane_iota(shape):
    return lax.broadcasted_iota(jnp.int32, shape, 1)


def _rope128(x, cos2, sin2):
    return x * cos2 + pltpu.roll(x, LANES // 2, 1) * sin2


def _rope64(x, cos4, sin4):
    lane = _lane_iota(x.shape)
    partner = jnp.where(lane % IDX_DIM < IDX_DIM // 2,
                        pltpu.roll(x, LANES - IDX_DIM // 2, 1), pltpu.roll(x, IDX_DIM // 2, 1))
    return x * cos4 + partner * sin4


def _dsa_prep_kernel(p_ref, t_ref, c2_ref, s2_ref, c4_ref, s4_ref,
                     q_ref, k_ref, v_ref, qi_ref, ki_ref):
    c2, s2, c4, s4 = c2_ref[...], s2_ref[...], c4_ref[...], s4_ref[...]
    qscale = LOG2E * DSA_HEAD_DIM ** -0.5
    for j in range(DSA_Q // LANES):
        x = p_ref[0, :, j * LANES:(j + 1) * LANES]
        q_ref[0, :, j * LANES:(j + 1) * LANES] = (_rope128(x, c2, s2) * qscale).astype(bf16)
    for j in range(DSA_KV // LANES):
        o = DSA_Q + j * LANES
        k_ref[0, :, j * LANES:(j + 1) * LANES] = _rope128(p_ref[0, :, o:o + LANES], c2, s2).astype(bf16)
        o = DSA_Q + DSA_KV + j * LANES
        v_ref[0, :, j * LANES:(j + 1) * LANES] = p_ref[0, :, o:o + LANES].astype(bf16)
    lane = _lane_iota((p_ref.shape[1], LANES))
    low = lane < IDX_DIM
    iscale = IDX_DIM ** -0.5
    for j in range(DSA_QI // LANES):
        o = DSA_Q + 2 * DSA_KV + j * LANES
        r = _rope64(p_ref[0, :, o:o + LANES], c4, s4) * iscale
        qi_ref[0, 2 * j] = jnp.where(low, r, 0.0).astype(bf16)
        qi_ref[0, 2 * j + 1] = jnp.where(low, pltpu.roll(r, LANES // 2, 1), 0.0).astype(bf16)
    ki_ref[0] = jnp.where(low, _rope64(t_ref[0], c4, s4), 0.0).astype(bf16)


def _dsa_prep(main, tail, batch, seq, ts=256):
    ts = min(ts, seq)
    c, s = _rope_tables(seq, DSA_HEAD_DIM // 2)
    c2 = jnp.asarray(np.concatenate([c, c], axis=1), f32)
    s2 = jnp.asarray(np.concatenate([-s, s], axis=1), f32)
    c, s = _rope_tables(seq, IDX_DIM // 2)
    c4 = jnp.asarray(np.concatenate([c, c, c, c], axis=1), f32)
    s4 = jnp.asarray(np.concatenate([-s, s, -s, s], axis=1), f32)
    tab = pl.BlockSpec((ts, LANES), lambda b, i: (i, 0))
    out = lambda w: pl.BlockSpec((1, ts, w), lambda b, i: (b, i, 0))
    return pl.pallas_call(
        _dsa_prep_kernel,
        grid=(batch, seq // ts),
        in_specs=[out(DSA_MAIN), out(LANES), tab, tab, tab, tab],
        out_specs=[out(DSA_Q), out(DSA_KV), out(DSA_KV),
                   pl.BlockSpec((1, IDX_HEADS, ts, LANES), lambda b, i: (b, 0, i, 0)), out(LANES)],
        out_shape=[jax.ShapeDtypeStruct((batch, seq, DSA_Q), bf16),
                   jax.ShapeDtypeStruct((batch, seq, DSA_KV), bf16),
                   jax.ShapeDtypeStruct((batch, seq, DSA_KV), bf16),
                   jax.ShapeDtypeStruct((batch, IDX_HEADS, seq, LANES), bf16),
                   jax.ShapeDtypeStruct((batch, seq, LANES), bf16)],
        compiler_params=_params(("parallel", "parallel")),
        name="dsa_prep",
    )(main, tail, c2, s2, c4, s4)


SEL_ROWS = 256
SEL_CHUNK = 512
SEL_SUB = 256


def _select_kernel(qi_ref, ki_ref, t_ref, bias_ref, key_ref, wb_ref, *, n_sel, n_chunks):
    tq, kc = SEL_ROWS, SEL_CHUNK
    qt = pl.program_id(1)
    q0 = qt * tq
    n_act = lax.div(qt, kc // tq) + 1
    qpos = q0 + lax.broadcasted_iota(jnp.int32, (tq, 1), 0)
    wscale = IDX_HEADS ** -0.5

    for h in range(IDX_HEADS):
        wb_ref[h] = jnp.broadcast_to(t_ref[0, :, IDX_DIM + h:IDX_DIM + h + 1] * wscale, (tq, LANES))
    q_all = qi_ref[0].reshape(IDX_HEADS * tq, LANES)

    def score_chunk(c, carry):
        off = pl.multiple_of(c * kc, kc)
        for s in range(kc // SEL_SUB):
            ki = ki_ref[0, pl.ds(off + s * SEL_SUB, SEL_SUB), :]
            r = lax.dot_general(q_all, ki, NT_DIMS, preferred_element_type=f32)
            acc = jnp.zeros((tq, SEL_SUB), f32)
            for h in range(IDX_HEADS):
                w = jnp.concatenate([wb_ref[h]] * (SEL_SUB // LANES), axis=1)
                acc = acc + jnp.maximum(r[h * tq:(h + 1) * tq], 0.0) * w
            kpos = off + s * SEL_SUB + _lane_iota((tq, SEL_SUB))
            sc = jnp.where(kpos <= qpos, acc + 0.0, -jnp.inf)
            bits = pltpu.bitcast(sc, jnp.int32)
            key_ref[c, :, s * SEL_SUB:(s + 1) * SEL_SUB] = bits ^ ((bits >> 31) & 0x7FFFFFFF)
        return carry

    lax.fori_loop(0, n_act, score_chunk, 0)

    def count_ge(cand):
        def body(c, acc):
            hit = jnp.where(key_ref[c] >= cand, 1.0, 0.0)
            for s in range(kc // LANES):
                acc = acc + hit[:, s * LANES:(s + 1) * LANES]
            return acc
        acc = lax.fori_loop(0, n_act, body, jnp.zeros((tq, LANES), f32))
        return jnp.sum(acc, axis=1, keepdims=True)

    kf = float(n_sel)
    ans = jnp.where(count_ge(jnp.zeros((tq, 1), jnp.int32)) >= kf, 0, INT_MIN).astype(jnp.int32)

    def bit_step(i, ans):
        cand = ans | jnp.left_shift(jnp.int32(1), 30 - i)
        return jnp.where(count_ge(cand) >= kf, cand, ans)

    ans = lax.fori_loop(0, 31, bit_step, ans)
    need = kf - count_ge(ans + 1)
    ans = jnp.where(qpos < n_sel, INT_MIN, ans)

    upper = (lax.broadcasted_iota(jnp.int32, (SEL_SUB, SEL_SUB), 0)
             < lax.broadcasted_iota(jnp.int32, (SEL_SUB, SEL_SUB), 1)).astype(bf16)

    def emit_chunk(c, seen):
        off = pl.multiple_of(c * kc, kc)
        for s in range(kc // SEL_SUB):
            key = key_ref[c, :, s * SEL_SUB:(s + 1) * SEL_SUB]
            eq = key == ans
            eqf = jnp.where(eq, 1.0, 0.0)
            before = seen + jnp.dot(eqf.astype(bf16), upper, preferred_element_type=f32)
            keep = jnp.logical_or(key > ans, jnp.logical_and(eq, before < need))
            kpos = off + s * SEL_SUB + _lane_iota((tq, SEL_SUB))
            keep = jnp.logical_and(keep, kpos <= qpos)
            bias_ref[0, c, :, s * SEL_SUB:(s + 1) * SEL_SUB] = jnp.where(keep, 0.0, NEG_BIG).astype(bf16)
            seen = seen + jnp.sum(eqf, axis=1, keepdims=True)
        return seen

    lax.fori_loop(0, n_act, emit_chunk, jnp.zeros((tq, 1), f32))

    def fill_chunk(c, carry):
        bias_ref[0, c] = jnp.full((tq, kc), NEG_BIG, bf16)
        return carry

    lax.fori_loop(n_act, n_chunks, fill_chunk, 0)


def _select(qi, ki, tail, batch, seq, n_sel):
    n_chunks = seq // SEL_CHUNK
    assert SEL_CHUNK >= n_sel and SEL_CHUNK % SEL_ROWS == 0
    return pl.pallas_call(
        functools.partial(_select_kernel, n_sel=n_sel, n_chunks=n_chunks),
        grid=(batch, seq // SEL_ROWS),
        in_specs=[pl.BlockSpec((1, IDX_HEADS, SEL_ROWS, LANES), lambda b, i: (b, 0, i, 0)),
                  pl.BlockSpec((1, seq, LANES), lambda b, i: (b, 0, 0)),
                  pl.BlockSpec((1, SEL_ROWS, LANES), lambda b, i: (b, i, 0))],
        out_specs=pl.BlockSpec((1, n_chunks, SEL_ROWS, SEL_CHUNK), lambda b, i: (b, 0, i, 0)),
        out_shape=jax.ShapeDtypeStruct((batch, n_chunks, seq, SEL_CHUNK), bf16),
        scratch_shapes=[pltpu.VMEM((n_chunks, SEL_ROWS, SEL_CHUNK), jnp.int32),
                        pltpu.VMEM((IDX_HEADS, SEL_ROWS, LANES), f32)],
        compiler_params=_params(("parallel", "parallel")),
        name="dsa_select",
    )(qi, ki, tail)


ATT_TILE = SEL_CHUNK
ATT_GROUP = DSA_HEADS // DSA_KV_HEADS


def _attention_kernel(q_ref, k_ref, v_ref, b_ref, o_ref, m_ref, acc_ref):
    i = pl.program_id(2)
    j = pl.program_id(3)
    dh = DSA_HEAD_DIM
    t = ATT_TILE

    @pl.when(j == 0)
    def _():
        m_ref[...] = jnp.full(m_ref.shape, NEG_BIG, f32)
        acc_ref[...] = jnp.zeros_like(acc_ref)

    @pl.when(j <= i)
    def _():
        k = k_ref[0]
        v_ones = jnp.concatenate([v_ref[0], jnp.ones((t, dh), bf16)], axis=1)
        for h in range(ATT_GROUP):
            s = lax.dot_general(q_ref[0, :, h * dh:(h + 1) * dh], k, NT_DIMS, preferred_element_type=f32)
            s = s + b_ref[0, 0].astype(f32)
            m_old = m_ref[h]
            m_new = jnp.maximum(m_old, jnp.max(s, axis=1, keepdims=True))
            p = jnp.concatenate([jnp.exp2(s[:, c * LANES:(c + 1) * LANES] - m_new) for c in range(t // LANES)],
                                axis=1)
            scale = jnp.exp2(m_old - m_new)
            pv = jnp.dot(p.astype(bf16), v_ones, preferred_element_type=f32)
            acc_ref[h] = jnp.concatenate([scale, scale], axis=1) * acc_ref[h] + pv
            m_ref[h] = m_new

    @pl.when(j == i)
    def _():
        for h in range(ATT_GROUP):
            acc = acc_ref[h]
            o_ref[0, :, h * dh:(h + 1) * dh] = (acc[:, :dh] / acc[:, dh:]).astype(o_ref.dtype)


def _attention(q, k, v, bias, batch, seq):
    t = ATT_TILE
    n = seq // t
    dh = DSA_HEAD_DIM
    gw = ATT_GROUP * dh
    return pl.pallas_call(
        _attention_kernel,
        grid=(batch, DSA_KV_HEADS, n, n),
        in_specs=[pl.BlockSpec((1, t, gw), lambda b, g, i, j: (b, i, g)),
                  pl.BlockSpec((1, t, dh), lambda b, g, i, j: (b, jnp.minimum(i, j), g)),
                  pl.BlockSpec((1, t, dh), lambda b, g, i, j: (b, jnp.minimum(i, j), g)),
                  pl.BlockSpec((1, 1, t, t), lambda b, g, i, j: (b, jnp.minimum(i, j), i, 0))],
        out_specs=pl.BlockSpec((1, t, gw), lambda b, g, i, j: (b, i, g)),
        out_shape=jax.ShapeDtypeStruct((batch, seq, DSA_Q), bf16),
        scratch_shapes=[pltpu.VMEM((ATT_GROUP, t, LANES), f32), pltpu.VMEM((ATT_GROUP, t, 2 * dh), f32)],
        compiler_params=_params(("parallel", "parallel", "parallel", "arbitrary")),
        name="dsa_attention",
    )(q, k, v, bias)


def _even_mixer(hb, w_in, gn_g, conv_w, conv_b, conv_ln_g, conv_ln_b, batch, seq):
    proj = _matmul(hb, w_in, f32).reshape(batch, seq, -1)
    ret = _retention(proj, gn_g, batch, seq)
    conv = _conformer_conv(proj, conv_w, conv_b, conv_ln_g, conv_ln_b, batch, seq)
    return jnp.concatenate([ret, conv], axis=-1).reshape(batch * seq, -1)


def _odd_mixer(hb, w_in, batch, seq):
    wt = w_in.T
    main = _matmul_nt(hb, wt, f32, 0, DSA_MAIN).reshape(batch, seq, DSA_MAIN)
    tail = _matmul_nt(hb, wt, f32, DSA_MAIN, LANES).reshape(batch, seq, LANES)
    q, k, v, qi, ki = _dsa_prep(main, tail, batch, seq)
    bias = _select(qi, ki, tail, batch, seq, min(TOPK_MAX, seq // 4))
    return _attention(q, k, v, bias, batch, seq).reshape(batch * seq, DSA_Q)


def kernel(x, even_w_in, even_ret_gn_g, even_conv_w, even_conv_b, even_conv_ln_g, even_conv_ln_b, even_w_out,
           odd_w_in, odd_w_out, mix_ln_g, mix_ln_b, moe_w_gu, moe_w_down, ffn_ln_g, ffn_ln_b, router_w, router_b):
    batch, seq, d = x.shape
    depth = mix_ln_g.shape[0]
    alpha = (2 * depth) ** 0.25
    router_ws = _router_weights(router_w)
    h = x.reshape(batch * seq, d).astype(f32)
    hb = h
    for layer in range(depth):
        i = layer // 2
        if layer % 2 == 0:
            mixed = _even_mixer(hb, even_w_in[i], even_ret_gn_g[i], even_conv_w[i], even_conv_b[i],
                                even_conv_ln_g[i], even_conv_ln_b[i], batch, seq)
            w_out = even_w_out[i]
        else:
            mixed = _odd_mixer(hb, odd_w_in[i], batch, seq)
            w_out = odd_w_out[i]
        hn, hnb, ridx, rwgt = _proj_norm_router(mixed, w_out.astype(bf16), h, mix_ln_g[layer], mix_ln_b[layer],
                                                router_ws, router_b.astype(f32), alpha)
        h, hb = _moe(hn, hnb, ridx, rwgt, moe_w_gu, moe_w_down, layer,
                     ffn_ln_g[layer], ffn_ln_b[layer], alpha)
    return h.reshape(batch, seq, d).astype(x.dtype)
```

```python
import functools

import numpy as np
import jax
import jax.numpy as jnp
from jax import lax
from jax.experimental import pallas as pl
from jax.experimental.pallas import tpu as pltpu

f32 = jnp.float32
bf16 = jnp.bfloat16

ROPE_THETA = 10000.0
LN_EPS = 1e-5
RET_HEADS = 4
RET_HEAD_DIM = 256
RET_WIDTH = RET_HEADS * RET_HEAD_DIM
RET_BLOCK = 256
CONV_CHANNELS = 1024
CONV_TAPS = 31
DSA_HEADS = 16
DSA_HEAD_DIM = 128
DSA_KV_HEADS = 4
IDX_HEADS = 16
IDX_DIM = 64
TOPK_MAX = 256
N_EXPERTS = 16
N_GROUPS = 4
EXPERTS_PER_GROUP = N_EXPERTS // N_GROUPS
EXPERT_FF = 512

LANES = 128
SUBLANES = 8
VMEM_LIMIT = 56 * 1024 * 1024
NEG_BIG = -1e30
LOG2E = 1.4426950408889634
INT_MIN = -2 ** 31
I16_MIN, I16_MAX = -2 ** 15, 2 ** 15 - 1

NT_DIMS = (((1,), (1,)), ((), ()))
TN_DIMS = (((0,), (0,)), ((), ()))


def _params(sem):
    return pltpu.CompilerParams(dimension_semantics=sem, vmem_limit_bytes=VMEM_LIMIT)


def _layer_norm(z, g, b):
    mu = jnp.mean(z, axis=-1, keepdims=True)
    zc = z - mu
    var = jnp.mean(zc * zc, axis=-1, keepdims=True)
    return zc * lax.rsqrt(var + LN_EPS) * g + b


def _mm_kernel(x_ref, w_ref, o_ref, wb_ref):
    @pl.when(pl.program_id(1) == 0)
    def _():
        wb_ref[...] = w_ref[...].astype(bf16)

    o_ref[...] = jnp.dot(x_ref[...].astype(bf16), wb_ref[...], preferred_element_type=f32).astype(o_ref.dtype)


def _matmul(x, w, out_dtype, n=None, tm=1024, tn=1024):
    m, k = x.shape
    n = w.shape[1] if n is None else n
    tm, tn = min(tm, m), min(tn, n)
    assert m % tm == 0 and n % tn == 0
    return pl.pallas_call(
        _mm_kernel,
        grid=(n // tn, m // tm),
        in_specs=[pl.BlockSpec((tm, k), lambda j, i: (i, 0)),
                  pl.BlockSpec((k, tn), lambda j, i: (0, j))],
        out_specs=pl.BlockSpec((tm, tn), lambda j, i: (i, j)),
        out_shape=jax.ShapeDtypeStruct((m, n), out_dtype),
        scratch_shapes=[pltpu.VMEM((k, tn), bf16)],
        compiler_params=_params(("arbitrary", "arbitrary")),
        name="matmul",
    )(x, w)


def _mm_nt_kernel(x_ref, wt_ref, o_ref, wb_ref, *, valid):
    @pl.when(pl.program_id(1) == 0)
    def _():
        w = wt_ref[...]
        if valid < wt_ref.shape[0]:
            w = jnp.where(lax.broadcasted_iota(jnp.int32, w.shape, 0) < valid, w, 0.0)
        wb_ref[...] = w.astype(bf16)

    o_ref[...] = lax.dot_general(x_ref[...].astype(bf16), wb_ref[...], NT_DIMS,
                                 preferred_element_type=f32).astype(o_ref.dtype)


def _matmul_nt(x, wt, out_dtype, row0, n, tm=1024, tn=1024):
    m, k = x.shape
    tm, tn = min(tm, m), min(tn, n)
    assert m % tm == 0 and n % tn == 0 and row0 % tn == 0
    valid = min(tn, wt.shape[0] - row0 - (n - tn))
    assert valid == tn or n == tn
    return pl.pallas_call(
        functools.partial(_mm_nt_kernel, valid=valid),
        grid=(n // tn, m // tm),
        in_specs=[pl.BlockSpec((tm, k), lambda j, i: (i, 0)),
                  pl.BlockSpec((tn, k), lambda j, i: (row0 // tn + j, 0))],
        out_specs=pl.BlockSpec((tm, tn), lambda j, i: (i, j)),
        out_shape=jax.ShapeDtypeStruct((m, n), out_dtype),
        scratch_shapes=[pltpu.VMEM((tn, k), bf16)],
        compiler_params=_params(("arbitrary", "arbitrary")),
        name="matmul_nt",
    )(x, wt)


def _retention_tables(chunk):
    h = np.arange(RET_HEADS, dtype=np.float64)
    log_g = np.log(1.0 - 2.0 ** (-5.0 - h))
    j = np.arange(chunk, dtype=np.float64)
    diff = j[:, None] - j[None, :]
    decay_in = np.where(diff[None] >= 0, np.exp(np.maximum(diff, 0.0)[None] * log_g[:, None, None]), 0.0)
    xi = np.exp((j[None, :] + 1.0) * log_g[:, None])
    zeta = np.exp((chunk - 1.0 - j[None, :]) * log_g[:, None])
    chunk_decay = np.exp(chunk * log_g)
    xi_b = np.broadcast_to(xi[:, :, None], (RET_HEADS, chunk, RET_HEAD_DIM))
    zeta_b = np.broadcast_to(zeta[:, :, None], (RET_HEADS, chunk, RET_HEAD_DIM))
    return jnp.asarray(decay_in, f32), jnp.asarray(xi_b, f32), jnp.asarray(zeta_b, f32), chunk_decay


def _rope_tables(seq, half):
    inv = ROPE_THETA ** (-np.arange(half, dtype=np.float64) / half)
    ang = np.arange(seq, dtype=np.float64)[:, None] * inv[None, :]
    return np.cos(ang), np.sin(ang)


def _retention_kernel(q_ref, k_ref, v_ref, g_ref, cos_ref, sin_ref, din_ref, xi_ref, zeta_ref,
                      gn_ref, o_ref, state_ref, *, chunk_decay):
    @pl.when(pl.program_id(1) == 0)
    def _():
        state_ref[...] = jnp.zeros_like(state_ref)

    cos = cos_ref[...]
    sin = sin_ref[...]
    dh = RET_HEAD_DIM
    half = dh // 2

    def rope(x):
        x1, x2 = x[:, :half], x[:, half:]
        return jnp.concatenate([x1 * cos - x2 * sin, x2 * cos + x1 * sin], axis=1)

    for h in range(RET_HEADS):
        cols = slice(h * dh, (h + 1) * dh)
        q = rope(q_ref[0, :, cols])
        k = rope(k_ref[0, :, cols]) * (dh ** -0.5)
        qb = q.astype(bf16)
        kb = k.astype(bf16)
        vb = v_ref[0, :, cols].astype(bf16)
        attn = lax.dot_general(qb, kb, NT_DIMS, preferred_element_type=f32) * din_ref[h]
        inner = jnp.dot(attn.astype(bf16), vb, preferred_element_type=f32)
        state = state_ref[h]
        cross = jnp.dot(qb, state.astype(bf16), preferred_element_type=f32) * xi_ref[h]
        kz = (k * zeta_ref[h]).astype(bf16)
        state_ref[h] = state * chunk_decay[h] + lax.dot_general(kz, vb, TN_DIMS, preferred_element_type=f32)
        y = inner + cross
        mu = jnp.mean(y, axis=-1, keepdims=True)
        yc = y - mu
        var = jnp.mean(yc * yc, axis=-1, keepdims=True)
        yn = yc * lax.rsqrt(var + LN_EPS) * gn_ref[:, cols]
        g = g_ref[0, :, cols]
        o_ref[0, :, cols] = (g * jax.nn.sigmoid(g) * yn).astype(o_ref.dtype)


def _retention(proj, gn_g, batch, seq):
    c = min(RET_BLOCK, seq)
    dh = RET_HEAD_DIM
    w = RET_WIDTH
    decay_in, xi_b, zeta_b, chunk_decay = _retention_tables(c)
    cos, sin = _rope_tables(seq, dh // 2)
    col = lambda j: pl.BlockSpec((1, c, w), lambda b, s, j=j: (b, s, j))
    whole = lambda shape: pl.BlockSpec(shape, lambda b, s: (0,) * len(shape))
    return pl.pallas_call(
        functools.partial(_retention_kernel, chunk_decay=[float(x) for x in chunk_decay]),
        grid=(batch, seq // c),
        in_specs=[col(0), col(1), col(2), col(3),
                  pl.BlockSpec((c, dh // 2), lambda b, s: (s, 0)),
                  pl.BlockSpec((c, dh // 2), lambda b, s: (s, 0)),
                  whole((RET_HEADS, c, c)), whole((RET_HEADS, c, dh)), whole((RET_HEADS, c, dh)),
                  whole((1, w))],
        out_specs=pl.BlockSpec((1, c, w), lambda b, s: (b, s, 0)),
        out_shape=jax.ShapeDtypeStruct((batch, seq, w), bf16),
        scratch_shapes=[pltpu.VMEM((RET_HEADS, dh, dh), f32)],
        compiler_params=_params(("parallel", "arbitrary")),
        name="retention",
    )(proj, proj, proj, proj, jnp.asarray(cos, f32), jnp.asarray(sin, f32),
      decay_in, xi_b, zeta_b, gn_g.reshape(1, w))


CONV_ROWS = 128
CONV_HIST = 32
CONV_SUB = 32


def _conv_kernel(ga_ref, gb_ref, w_ref, b_ref, lg_ref, lb_ref, o_ref, ush):
    total = CONV_HIST + CONV_ROWS

    @pl.when(pl.program_id(1) == 0)
    def _():
        ush[0, 0:CONV_HIST, :] = jnp.zeros((CONV_HIST, CONV_CHANNELS), f32)

    ush[0, CONV_HIST:total, :] = ga_ref[0] * jax.nn.sigmoid(gb_ref[0])
    for r in range(1, SUBLANES):
        ush[r, 0:total - SUBLANES, :] = ush[0, r:r + total - SUBLANES, :]
    first = CONV_HIST - (CONV_TAPS - 1)
    for r0 in range(0, CONV_ROWS, CONV_SUB):
        acc = jnp.zeros((CONV_SUB, CONV_CHANNELS), f32)
        for t in range(CONV_TAPS):
            d = first + t
            lo = r0 + d - d % SUBLANES
            acc = acc + w_ref[t:t + 1, :] * ush[d % SUBLANES, lo:lo + CONV_SUB, :]
        y = _layer_norm(acc + b_ref[...], lg_ref[...], lb_ref[...])
        o_ref[0, r0:r0 + CONV_SUB, :] = (y * jax.nn.sigmoid(y)).astype(o_ref.dtype)
    ush[0, 0:CONV_HIST, :] = ush[0, CONV_ROWS:total, :]


def _conformer_conv(proj, conv_w, conv_b, ln_g, ln_b, batch, seq):
    c = CONV_CHANNELS
    a_blk = 4 * RET_WIDTH // c
    row = pl.BlockSpec((1, c), lambda b, s: (0, 0))
    return pl.pallas_call(
        _conv_kernel,
        grid=(batch, seq // CONV_ROWS),
        in_specs=[pl.BlockSpec((1, CONV_ROWS, c), lambda b, s: (b, s, a_blk)),
                  pl.BlockSpec((1, CONV_ROWS, c), lambda b, s: (b, s, a_blk + 1)),
                  pl.BlockSpec((CONV_TAPS, c), lambda b, s: (0, 0)),
                  row, row, row],
        out_specs=pl.BlockSpec((1, CONV_ROWS, c), lambda b, s: (b, s, 0)),
        out_shape=jax.ShapeDtypeStruct((batch, seq, c), bf16),
        scratch_shapes=[pltpu.VMEM((SUBLANES, CONV_HIST + CONV_ROWS, c), f32)],
        compiler_params=_params(("parallel", "arbitrary")),
        name="conformer_conv",
    )(proj, proj, conv_w, conv_b.reshape(1, c), ln_g.reshape(1, c), ln_b.reshape(1, c))


def _route(sel, aff):
    epg = EXPERTS_PER_GROUP
    gscore = []
    for g in range(N_GROUPS):
        v = sel[g * epg:(g + 1) * epg]
        best = None
        for a in range(epg):
            for b in range(a + 1, epg):
                s = v[a] + v[b]
                best = s if best is None else jnp.maximum(best, s)
        gscore.append(best)
    gmax = functools.reduce(jnp.maximum, gscore)
    taken = None
    gsel = []
    for g in range(N_GROUPS):
        hit = gscore[g] == gmax
        if taken is not None:
            hit = jnp.logical_and(hit, jnp.logical_not(taken))
        taken = hit if taken is None else jnp.logical_or(taken, hit)
        gsel.append(hit)

    def pick(rows, j):
        out = rows[(N_GROUPS - 1) * epg + j]
        for g in range(N_GROUPS - 2, -1, -1):
            out = jnp.where(gsel[g], rows[g * epg + j], out)
        return out

    x = [pick(sel, j) for j in range(epg)]
    a = [pick(aff, j) for j in range(epg)]
    base = jnp.zeros_like(x[0], dtype=jnp.int32)
    for g in range(1, N_GROUPS):
        base = jnp.where(gsel[g], g * epg, base)

    def first_max(vals, excluded):
        vmax = functools.reduce(jnp.maximum, vals)
        taken = None
        hits = []
        for j in range(epg):
            hit = vals[j] == vmax
            if excluded is not None:
                hit = jnp.logical_and(hit, jnp.logical_not(excluded[j]))
            if taken is not None:
                hit = jnp.logical_and(hit, jnp.logical_not(taken))
            taken = hit if taken is None else jnp.logical_or(taken, hit)
            hits.append(hit)
        return hits

    h1 = first_max(x, None)
    x2 = [jnp.where(h1[j], -jnp.inf, x[j]) for j in range(epg)]
    h2 = first_max(x2, h1)

    def gather(hits):
        idx = base
        val = jnp.zeros_like(a[0])
        for j in range(epg):
            idx = jnp.where(hits[j], base + j, idx)
            val = jnp.where(hits[j], a[j], val)
        return idx, val

    e0, a0 = gather(h1)
    e1, a1 = gather(h2)
    tot = a0 + a1
    return e0, e1, a0 / tot, a1 / tot


def _router_weights(router_w):
    d, e = router_w.shape
    w = router_w.astype(f32)
    w_hi = w.astype(bf16)
    w_lo = (w - w_hi.astype(f32)).astype(bf16)
    top = jnp.concatenate([w_hi, w_lo, jnp.zeros((d, LANES - 2 * e), bf16)], axis=1)
    bot = jnp.concatenate([w_hi, jnp.zeros((d, LANES - e), bf16)], axis=1)
    return jnp.concatenate([top, bot], axis=0)


def _proj_norm_router_kernel(a_ref, w_ref, h_ref, g_ref, b_ref, rw_ref, rb_ref,
                             hn_ref, hb_ref, ridx_ref, rwgt_ref, *, alpha):
    m = jnp.dot(a_ref[...], w_ref[...], preferred_element_type=f32)
    hn = _layer_norm(alpha * h_ref[...] + m, g_ref[...], b_ref[...])
    hn_ref[...] = hn
    hi = hn.astype(bf16)
    hb_ref[...] = hi
    lo = (hn - hi.astype(f32)).astype(bf16)
    d = hn.shape[1]
    parts = (jnp.dot(hi, rw_ref[0:d, :], preferred_element_type=f32)
             + jnp.dot(lo, rw_ref[d:2 * d, :], preferred_element_type=f32))
    parts = parts.T
    logits = parts[0:N_EXPERTS] + parts[N_EXPERTS:2 * N_EXPERTS]
    aff = jax.nn.sigmoid(logits)
    sel = aff + rb_ref[...]
    e0, e1, w0, w1 = _route([sel[e:e + 1, :] for e in range(N_EXPERTS)],
                            [aff[e:e + 1, :] for e in range(N_EXPERTS)])
    ridx_ref[0:1, :] = e0
    ridx_ref[1:2, :] = e1
    rwgt_ref[0:1, :] = w0
    rwgt_ref[1:2, :] = w1


def _proj_norm_router(a, w, h, ln_g, ln_b, router_ws, router_b, alpha, tm=512):
    t, k = a.shape
    d = w.shape[1]
    tm = min(tm, t)
    row = pl.BlockSpec((1, d), lambda i: (0, 0))
    return pl.pallas_call(
        functools.partial(_proj_norm_router_kernel, alpha=alpha),
        grid=(t // tm,),
        in_specs=[pl.BlockSpec((tm, k), lambda i: (i, 0)),
                  pl.BlockSpec((k, d), lambda i: (0, 0)),
                  pl.BlockSpec((tm, d), lambda i: (i, 0)),
                  row, row,
                  pl.BlockSpec((2 * d, LANES), lambda i: (0, 0)),
                  pl.BlockSpec((N_EXPERTS, 1), lambda i: (0, 0))],
        out_specs=[pl.BlockSpec((tm, d), lambda i: (i, 0)),
                   pl.BlockSpec((tm, d), lambda i: (i, 0)),
                   pl.BlockSpec((2, tm), lambda i: (0, i)),
                   pl.BlockSpec((2, tm), lambda i: (0, i))],
        out_shape=[jax.ShapeDtypeStruct((t, d), f32), jax.ShapeDtypeStruct((t, d), bf16),
                   jax.ShapeDtypeStruct((2, t), jnp.int32), jax.ShapeDtypeStruct((2, t), f32)],
        compiler_params=_params(("parallel",)),
        name="proj_norm_router",
    )(a, w, h, ln_g.reshape(1, d), ln_b.reshape(1, d), router_ws, router_b.reshape(N_EXPERTS, 1))


MOE_TILE = 512


def _experts_kernel(te_ref, tv_ref, x_ref, wgu_ref, wd_ref, o_ref, wgu_b, wd_b):
    i = pl.program_id(0)
    prev = te_ref[jnp.maximum(i - 1, 0)]

    @pl.when(jnp.logical_or(i == 0, te_ref[i] != prev))
    def _():
        wgu_b[...] = wgu_ref[0].astype(bf16)
        wd_b[...] = wd_ref[0].astype(bf16)

    @pl.when(tv_ref[i] > 0)
    def _():
        hgu = jnp.dot(x_ref[...], wgu_b[...], preferred_element_type=f32)
        a, b = hgu[:, :EXPERT_FF], hgu[:, EXPERT_FF:]
        act = (a * jax.nn.sigmoid(a) * b).astype(bf16)
        o_ref[...] = jnp.dot(act, wd_b[...], preferred_element_type=f32).astype(o_ref.dtype)

    @pl.when(tv_ref[i] == 0)
    def _():
        o_ref[...] = jnp.zeros_like(o_ref)


def _experts(xs, w_gu, w_down, layer, tile_expert, tile_valid):
    r, d = xs.shape
    ff2 = w_gu.shape[3]
    grid_spec = pltpu.PrefetchScalarGridSpec(
        num_scalar_prefetch=2,
        grid=(r // MOE_TILE,),
        in_specs=[pl.BlockSpec((MOE_TILE, d), lambda i, te, tv: (i, 0)),
                  pl.BlockSpec((None, 1, d, ff2), lambda i, te, tv: (layer, te[i], 0, 0)),
                  pl.BlockSpec((None, 1, ff2 // 2, d), lambda i, te, tv: (layer, te[i], 0, 0))],
        out_specs=pl.BlockSpec((MOE_TILE, d), lambda i, te, tv: (i, 0)),
        scratch_shapes=[pltpu.VMEM((d, ff2), bf16), pltpu.VMEM((ff2 // 2, d), bf16)],
    )
    return pl.pallas_call(
        _experts_kernel,
        grid_spec=grid_spec,
        out_shape=jax.ShapeDtypeStruct((r, d), bf16),
        compiler_params=_params(("arbitrary",)),
        name="experts",
    )(tile_expert, tile_valid, xs, w_gu, w_down)


def _dispatch(ridx):
    t = ridx.shape[1]
    slots = 2 * t
    es = ridx.T.reshape(slots)
    experts = jnp.arange(N_EXPERTS, dtype=jnp.int32)
    onehot = (es[:, None] == experts[None, :]).astype(jnp.int32)
    csum = jnp.cumsum(onehot, axis=0)
    rank = jnp.sum((csum - onehot) * onehot, axis=1)
    counts = csum[-1]
    tiles_per = (counts + MOE_TILE - 1) // MOE_TILE
    tile_end = jnp.cumsum(tiles_per)
    start = (tile_end - tiles_per) * MOE_TILE
    dest = start[es] + rank
    n_tiles = slots // MOE_TILE + N_EXPERTS
    tile_ids = jnp.arange(n_tiles, dtype=jnp.int32)
    tile_valid = (tile_ids < tile_end[-1]).astype(jnp.int32)
    tile_expert = jnp.sum((tile_ids[:, None] >= tile_end[None, :]).astype(jnp.int32), axis=1)
    last_used = jnp.max(jnp.where(counts > 0, experts, 0))
    tile_expert = jnp.minimum(tile_expert, last_used)
    rows = n_tiles * MOE_TILE
    tok = (jnp.arange(rows, dtype=jnp.int32) % t).at[dest].set(jnp.arange(slots, dtype=jnp.int32) // 2)
    return tok, dest, tile_expert, tile_valid


def _combine_norm_kernel(h_ref, y0_ref, y1_ref, w0_ref, w1_ref, g_ref, b_ref, o_ref, ob_ref, *, alpha):
    f = w0_ref[...] * y0_ref[...].astype(f32) + w1_ref[...] * y1_ref[...].astype(f32)
    hn = _layer_norm(alpha * h_ref[...] + f, g_ref[...], b_ref[...])
    o_ref[...] = hn
    ob_ref[...] = hn.astype(bf16)


def _combine_norm(h, y, w0, w1, ln_g, ln_b, alpha, tm=512):
    t, d = h.shape
    tm = min(tm, t)
    nt = t // tm
    blk = pl.BlockSpec((tm, d), lambda i: (i, 0))
    col = pl.BlockSpec((tm, 1), lambda i: (i, 0))
    row = pl.BlockSpec((1, d), lambda i: (0, 0))
    return pl.pallas_call(
        functools.partial(_combine_norm_kernel, alpha=alpha),
        grid=(nt,),
        in_specs=[blk, blk, pl.BlockSpec((tm, d), lambda i: (i + nt, 0)), col, col, row, row],
        out_specs=[blk, blk],
        out_shape=[jax.ShapeDtypeStruct((t, d), f32), jax.ShapeDtypeStruct((t, d), bf16)],
        compiler_params=_params(("parallel",)),
        name="combine_norm",
    )(h, y, y, w0, w1, ln_g.reshape(1, d), ln_b.reshape(1, d))


def _moe(hn, hb, ridx, rwgt, w_gu, w_down, layer, ln_g, ln_b, alpha):
    t, d = hn.shape
    tok, dest, tile_expert, tile_valid = _dispatch(ridx)
    xs = hb.at[tok].get(mode="promise_in_bounds")
    ys = _experts(xs, w_gu, w_down, layer, tile_expert, tile_valid)
    y = ys.at[dest.reshape(t, 2).T.reshape(2 * t)].get(mode="promise_in_bounds")
    return _combine_norm(hn, y, rwgt[0].reshape(t, 1), rwgt[1].reshape(t, 1), ln_g, ln_b, alpha)


DSA_Q = DSA_HEADS * DSA_HEAD_DIM
DSA_KV = DSA_KV_HEADS * DSA_HEAD_DIM
DSA_QI = IDX_HEADS * IDX_DIM
DSA_MAIN = DSA_Q + 2 * DSA_KV + DSA_QI


def _lane_iota(shape):
    return lax.broadcasted_iota(jnp.int32, shape, 1)


def _rope128(x, cos2, sin2):
    return x * cos2 + pltpu.roll(x, LANES // 2, 1) * sin2


def _rope64(x, cos4, sin4):
    lane = _lane_iota(x.shape)
    partner = jnp.where(lane % IDX_DIM < IDX_DIM // 2,
                        pltpu.roll(x, LANES - IDX_DIM // 2, 1), pltpu.roll(x, IDX_DIM // 2, 1))
    return x * cos4 + partner * sin4


def _dsa_prep_kernel(p_ref, t_ref, c2_ref, s2_ref, c4_ref, s4_ref,
                     q_ref, k_ref, v_ref, qi_ref, ki_ref):
    c2, s2, c4, s4 = c2_ref[...], s2_ref[...], c4_ref[...], s4_ref[...]
    qscale = LOG2E * DSA_HEAD_DIM ** -0.5
    for j in range(DSA_Q // LANES):
        x = p_ref[0, :, j * LANES:(j + 1) * LANES]
        q_ref[0, :, j * LANES:(j + 1) * LANES] = (_rope128(x, c2, s2) * qscale).astype(bf16)
    for j in range(DSA_KV // LANES):
        o = DSA_Q + j * LANES
        k_ref[0, :, j * LANES:(j + 1) * LANES] = _rope128(p_ref[0, :, o:o + LANES], c2, s2).astype(bf16)
        o = DSA_Q + DSA_KV + j * LANES
        v_ref[0, :, j * LANES:(j + 1) * LANES] = p_ref[0, :, o:o + LANES].astype(bf16)
    lane = _lane_iota((p_ref.shape[1], LANES))
    low = lane < IDX_DIM
    iscale = IDX_DIM ** -0.5
    for j in range(DSA_QI // LANES):
        o = DSA_Q + 2 * DSA_KV + j * LANES
        r = _rope64(p_ref[0, :, o:o + LANES], c4, s4) * iscale
        qi_ref[0, 2 * j] = jnp.where(low, r, 0.0).astype(bf16)
        qi_ref[0, 2 * j + 1] = jnp.where(low, pltpu.roll(r, LANES // 2, 1), 0.0).astype(bf16)
    ki_ref[0] = jnp.where(low, _rope64(t_ref[0], c4, s4), 0.0).astype(bf16)


def _dsa_prep(main, tail, batch, seq, ts=256):
    ts = min(ts, seq)
    c, s = _rope_tables(seq, DSA_HEAD_DIM // 2)
    c2 = jnp.asarray(np.concatenate([c, c], axis=1), f32)
    s2 = jnp.asarray(np.concatenate([-s, s], axis=1), f32)
    c, s = _rope_tables(seq, IDX_DIM // 2)
    c4 = jnp.asarray(np.concatenate([c, c, c, c], axis=1), f32)
    s4 = jnp.asarray(np.concatenate([-s, s, -s, s], axis=1), f32)
    tab = pl.BlockSpec((ts, LANES), lambda b, i: (i, 0))
    out = lambda w: pl.BlockSpec((1, ts, w), lambda b, i: (b, i, 0))
    return pl.pallas_call(
        _dsa_prep_kernel,
        grid=(batch, seq // ts),
        in_specs=[out(DSA_MAIN), out(LANES), tab, tab, tab, tab],
        out_specs=[out(DSA_Q), out(DSA_KV), out(DSA_KV),
                   pl.BlockSpec((1, IDX_HEADS, ts, LANES), lambda b, i: (b, 0, i, 0)), out(LANES)],
        out_shape=[jax.ShapeDtypeStruct((batch, seq, DSA_Q), bf16),
                   jax.ShapeDtypeStruct((batch, seq, DSA_KV), bf16),
                   jax.ShapeDtypeStruct((batch, seq, DSA_KV), bf16),
                   jax.ShapeDtypeStruct((batch, IDX_HEADS, seq, LANES), bf16),
                   jax.ShapeDtypeStruct((batch, seq, LANES), bf16)],
        compiler_params=_params(("parallel", "parallel")),
        name="dsa_prep",
    )(main, tail, c2, s2, c4, s4)


SEL_ROWS = 256
SEL_CHUNK = 512
SEL_SUB = 256


def _select_kernel(qi_ref, ki_ref, t_ref, bias_ref, key_ref, wb_ref, hi_ref, lo_ref, *, n_sel, n_chunks):
    tq, kc = SEL_ROWS, SEL_CHUNK
    qt = pl.program_id(1)
    q0 = qt * tq
    n_act = lax.div(qt, kc // tq) + 1
    qpos = q0 + lax.broadcasted_iota(jnp.int32, (tq, 1), 0)
    wscale = IDX_HEADS ** -0.5

    for h in range(IDX_HEADS):
        wb_ref[h] = jnp.broadcast_to(t_ref[0, :, IDX_DIM + h:IDX_DIM + h + 1] * wscale, (tq, LANES))
    q_all = qi_ref[0].reshape(IDX_HEADS * tq, LANES)

    def score_chunk(c, carry):
        off = pl.multiple_of(c * kc, kc)
        for s in range(kc // SEL_SUB):
            ki = ki_ref[0, pl.ds(off + s * SEL_SUB, SEL_SUB), :]
            r = lax.dot_general(q_all, ki, NT_DIMS, preferred_element_type=f32)
            acc = jnp.zeros((tq, SEL_SUB), f32)
            for h in range(IDX_HEADS):
                w = jnp.concatenate([wb_ref[h]] * (SEL_SUB // LANES), axis=1)
                acc = acc + jnp.maximum(r[h * tq:(h + 1) * tq], 0.0) * w
            kpos = off + s * SEL_SUB + _lane_iota((tq, SEL_SUB))
            sc = jnp.where(kpos <= qpos, acc + 0.0, -jnp.inf)
            bits = pltpu.bitcast(sc, jnp.int32)
            key = bits ^ ((bits >> 31) & 0x7FFFFFFF)
            key_ref[c, :, s * SEL_SUB:(s + 1) * SEL_SUB] = key
            hi_ref[c, :, s * SEL_SUB:(s + 1) * SEL_SUB] = (key >> 16).astype(jnp.int16)
        return carry

    lax.fori_loop(0, n_act, score_chunk, 0)

    def count_ge(cand):
        def body(c, acc):
            hit = jnp.where(key_ref[c] >= cand, 1.0, 0.0)
            for s in range(kc // LANES):
                acc = acc + hit[:, s * LANES:(s + 1) * LANES]
            return acc
        acc = lax.fori_loop(0, n_act, body, jnp.zeros((tq, LANES), f32))
        return jnp.sum(acc, axis=1, keepdims=True)

    def count_ge16(ref, cand):
        cand = cand.astype(jnp.int16)

        def body(c, acc):
            hit = jnp.where(ref[c] >= cand, jnp.int16(1), jnp.int16(0))
            for s in range(kc // LANES):
                acc = acc + hit[:, s * LANES:(s + 1) * LANES]
            return acc
        acc = lax.fori_loop(0, n_act, body, jnp.zeros((tq, LANES), jnp.int16))
        return jnp.sum(acc.astype(f32), axis=1, keepdims=True)

    def rank_select16(ref, rank):
        ans = jnp.where(count_ge16(ref, jnp.zeros((tq, 1), jnp.int32)) >= rank, 0, I16_MIN).astype(jnp.int32)

        def bit_step(i, ans):
            cand = ans | jnp.left_shift(jnp.int32(1), 14 - i)
            return jnp.where(count_ge16(ref, cand) >= rank, cand, ans)

        return lax.fori_loop(0, 15, bit_step, ans)

    kf = float(n_sel)
    top = rank_select16(hi_ref, kf)
    above = jnp.where(top == I16_MAX, 0.0, count_ge16(hi_ref, jnp.minimum(top + 1, I16_MAX)))
    top16 = top.astype(jnp.int16)

    def low_chunk(c, carry):
        low = ((key_ref[c] & 0xFFFF) + I16_MIN).astype(jnp.int16)
        lo_ref[c] = jnp.where(hi_ref[c] == top16, low, jnp.int16(I16_MIN))
        return carry

    lax.fori_loop(0, n_act, low_chunk, 0)
    bottom = rank_select16(lo_ref, kf - above)
    ans = top * 65536 + (bottom - I16_MIN)
    need = kf - count_ge(ans + 1)
    ans = jnp.where(qpos < n_sel, INT_MIN, ans)

    upper = (lax.broadcasted_iota(jnp.int32, (SEL_SUB, SEL_SUB), 0)
             < lax.broadcasted_iota(jnp.int32, (SEL_SUB, SEL_SUB), 1)).astype(bf16)

    def emit_chunk(c, seen):
        off = pl.multiple_of(c * kc, kc)
        for s in range(kc // SEL_SUB):
            key = key_ref[c, :, s * SEL_SUB:(s + 1) * SEL_SUB]
            eq = key == ans
            eqf = jnp.where(eq, 1.0, 0.0)
            before = seen + jnp.dot(eqf.astype(bf16), upper, preferred_element_type=f32)
            keep = jnp.logical_or(key > ans, jnp.logical_and(eq, before < need))
            kpos = off + s * SEL_SUB + _lane_iota((tq, SEL_SUB))
            keep = jnp.logical_and(keep, kpos <= qpos)
            bias_ref[0, c, :, s * SEL_SUB:(s + 1) * SEL_SUB] = jnp.where(keep, 0.0, NEG_BIG).astype(bf16)
            seen = seen + jnp.sum(eqf, axis=1, keepdims=True)
        return seen

    lax.fori_loop(0, n_act, emit_chunk, jnp.zeros((tq, 1), f32))

    def fill_chunk(c, carry):
        bias_ref[0, c] = jnp.full((tq, kc), NEG_BIG, bf16)
        return carry

    lax.fori_loop(n_act, n_chunks, fill_chunk, 0)


def _select(qi, ki, tail, batch, seq, n_sel):
    n_chunks = seq // SEL_CHUNK
    assert SEL_CHUNK >= n_sel and SEL_CHUNK % SEL_ROWS == 0
    return pl.pallas_call(
        functools.partial(_select_kernel, n_sel=n_sel, n_chunks=n_chunks),
        grid=(batch, seq // SEL_ROWS),
        in_specs=[pl.BlockSpec((1, IDX_HEADS, SEL_ROWS, LANES), lambda b, i: (b, 0, i, 0)),
                  pl.BlockSpec((1, seq, LANES), lambda b, i: (b, 0, 0)),
                  pl.BlockSpec((1, SEL_ROWS, LANES), lambda b, i: (b, i, 0))],
        out_specs=pl.BlockSpec((1, n_chunks, SEL_ROWS, SEL_CHUNK), lambda b, i: (b, 0, i, 0)),
        out_shape=jax.ShapeDtypeStruct((batch, n_chunks, seq, SEL_CHUNK), bf16),
        scratch_shapes=[pltpu.VMEM((n_chunks, SEL_ROWS, SEL_CHUNK), jnp.int32),
                        pltpu.VMEM((IDX_HEADS, SEL_ROWS, LANES), f32),
                        pltpu.VMEM((n_chunks, SEL_ROWS, SEL_CHUNK), jnp.int16),
                        pltpu.VMEM((n_chunks, SEL_ROWS, SEL_CHUNK), jnp.int16)],
        compiler_params=_params(("parallel", "parallel")),
        name="dsa_select",
    )(qi, ki, tail)


ATT_TILE = SEL_CHUNK
ATT_GROUP = DSA_HEADS // DSA_KV_HEADS


def _attention_kernel(q_ref, k_ref, v_ref, b_ref, o_ref, m_ref, acc_ref):
    i = pl.program_id(2)
    j = pl.program_id(3)
    dh = DSA_HEAD_DIM
    t = ATT_TILE

    @pl.when(j == 0)
    def _():
        m_ref[...] = jnp.full(m_ref.shape, NEG_BIG, f32)
        acc_ref[...] = jnp.zeros_like(acc_ref)

    @pl.when(j <= i)
    def _():
        k = k_ref[0]
        v_ones = jnp.concatenate([v_ref[0], jnp.ones((t, dh), bf16)], axis=1)
        for h in range(ATT_GROUP):
            s = lax.dot_general(q_ref[0, :, h * dh:(h + 1) * dh], k, NT_DIMS, preferred_element_type=f32)
            s = s + b_ref[0, 0].astype(f32)
            m_old = m_ref[h]
            m_new = jnp.maximum(m_old, jnp.max(s, axis=1, keepdims=True))
            p = jnp.concatenate([jnp.exp2(s[:, c * LANES:(c + 1) * LANES] - m_new) for c in range(t // LANES)],
                                axis=1)
            scale = jnp.exp2(m_old - m_new)
            pv = jnp.dot(p.astype(bf16), v_ones, preferred_element_type=f32)
            acc_ref[h] = jnp.concatenate([scale, scale], axis=1) * acc_ref[h] + pv
            m_ref[h] = m_new

    @pl.when(j == i)
    def _():
        for h in range(ATT_GROUP):
            acc = acc_ref[h]
            o_ref[0, :, h * dh:(h + 1) * dh] = (acc[:, :dh] / acc[:, dh:]).astype(o_ref.dtype)


def _attention(q, k, v, bias, batch, seq):
    t = ATT_TILE
    n = seq // t
    dh = DSA_HEAD_DIM
    gw = ATT_GROUP * dh
    return pl.pallas_call(
        _attention_kernel,
        grid=(batch, DSA_KV_HEADS, n, n),
        in_specs=[pl.BlockSpec((1, t, gw), lambda b, g, i, j: (b, i, g)),
                  pl.BlockSpec((1, t, dh), lambda b, g, i, j: (b, jnp.minimum(i, j), g)),
                  pl.BlockSpec((1, t, dh), lambda b, g, i, j: (b, jnp.minimum(i, j), g)),
                  pl.BlockSpec((1, 1, t, t), lambda b, g, i, j: (b, jnp.minimum(i, j), i, 0))],
        out_specs=pl.BlockSpec((1, t, gw), lambda b, g, i, j: (b, i, g)),
        out_shape=jax.ShapeDtypeStruct((batch, seq, DSA_Q), bf16),
        scratch_shapes=[pltpu.VMEM((ATT_GROUP, t, LANES), f32), pltpu.VMEM((ATT_GROUP, t, 2 * dh), f32)],
        compiler_params=_params(("parallel", "parallel", "parallel", "arbitrary")),
        name="dsa_attention",
    )(q, k, v, bias)


def _even_mixer(hb, w_in, gn_g, conv_w, conv_b, conv_ln_g, conv_ln_b, batch, seq):
    proj = _matmul(hb, w_in, f32).reshape(batch, seq, -1)
    ret = _retention(proj, gn_g, batch, seq)
    conv = _conformer_conv(proj, conv_w, conv_b, conv_ln_g, conv_ln_b, batch, seq)
    return jnp.concatenate([ret, conv], axis=-1).reshape(batch * seq, -1)


def _odd_mixer(hb, w_in, batch, seq):
    wt = w_in.T
    main = _matmul_nt(hb, wt, f32, 0, DSA_MAIN).reshape(batch, seq, DSA_MAIN)
    tail = _matmul_nt(hb, wt, f32, DSA_MAIN, LANES).reshape(batch, seq, LANES)
    q, k, v, qi, ki = _dsa_prep(main, tail, batch, seq)
    bias = _select(qi, ki, tail, batch, seq, min(TOPK_MAX, seq // 4))
    return _attention(q, k, v, bias, batch, seq).reshape(batch * seq, DSA_Q)


def kernel(x, even_w_in, even_ret_gn_g, even_conv_w, even_conv_b, even_conv_ln_g, even_conv_ln_b, even_w_out,
           odd_w_in, odd_w_out, mix_ln_g, mix_ln_b, moe_w_gu, moe_w_down, ffn_ln_g, ffn_ln_b, router_w, router_b):
    batch, seq, d = x.shape
    depth = mix_ln_g.shape[0]
    alpha = (2 * depth) ** 0.25
    router_ws = _router_weights(router_w)
    h = x.reshape(batch * seq, d).astype(f32)
    hb = h
    for layer in range(depth):
        i = layer // 2
        if layer % 2 == 0:
            mixed = _even_mixer(hb, even_w_in[i], even_ret_gn_g[i], even_conv_w[i], even_conv_b[i],
                                even_conv_ln_g[i], even_conv_ln_b[i], batch, seq)
            w_out = even_w_out[i]
        else:
            mixed = _odd_mixer(hb, odd_w_in[i], batch, seq)
            w_out = odd_w_out[i]
        hn, hnb, ridx, rwgt = _proj_norm_router(mixed, w_out.astype(bf16), h, mix_ln_g[layer], mix_ln_b[layer],
                                                router_ws, router_b.astype(f32), alpha)
        h, hb = _moe(hn, hnb, ridx, rwgt, moe_w_gu, moe_w_down, layer,
                     ffn_ln_g[layer], ffn_ln_b[layer], alpha)
    return h.reshape(batch, seq, d).astype(x.dtype)
```

```python
import functools

import numpy as np
import jax
import jax.numpy as jnp
from jax import lax
from jax.experimental import pallas as pl
from jax.experimental.pallas import tpu as pltpu

f32 = jnp.float32
bf16 = jnp.bfloat16

ROPE_THETA = 10000.0
LN_EPS = 1e-5
RET_HEADS = 4
RET_HEAD_DIM = 256
RET_WIDTH = RET_HEADS * RET_HEAD_DIM
RET_BLOCK = 256
CONV_CHANNELS = 1024
CONV_TAPS = 31
DSA_HEADS = 16
DSA_HEAD_DIM = 128
DSA_KV_HEADS = 4
IDX_HEADS = 16
IDX_DIM = 64
TOPK_MAX = 256
N_EXPERTS = 16
N_GROUPS = 4
EXPERTS_PER_GROUP = N_EXPERTS // N_GROUPS
EXPERT_FF = 512

LANES = 128
SUBLANES = 8
PACKED_ROWS = 16
VMEM_LIMIT = 56 * 1024 * 1024
NEG_BIG = -1e30
LOG2E = 1.4426950408889634
INT_MIN = -2 ** 31
I16_MIN, I16_MAX = -2 ** 15, 2 ** 15 - 1

NT_DIMS = (((1,), (1,)), ((), ()))
TN_DIMS = (((0,), (0,)), ((), ()))


def _params(sem):
    return pltpu.CompilerParams(dimension_semantics=sem, vmem_limit_bytes=VMEM_LIMIT)


def _layer_norm(z, g, b):
    mu = jnp.mean(z, axis=-1, keepdims=True)
    zc = z - mu
    var = jnp.mean(zc * zc, axis=-1, keepdims=True)
    return zc * lax.rsqrt(var + LN_EPS) * g + b


def _mm_kernel(x_ref, w_ref, o_ref, wb_ref):
    @pl.when(pl.program_id(1) == 0)
    def _():
        wb_ref[...] = w_ref[...].astype(bf16)

    o_ref[...] = jnp.dot(x_ref[...].astype(bf16), wb_ref[...], preferred_element_type=f32).astype(o_ref.dtype)


def _matmul(x, w, out_dtype, n=None, tm=1024, tn=1024):
    m, k = x.shape
    n = w.shape[1] if n is None else n
    tm, tn = min(tm, m), min(tn, n)
    assert m % tm == 0 and n % tn == 0
    return pl.pallas_call(
        _mm_kernel,
        grid=(n // tn, m // tm),
        in_specs=[pl.BlockSpec((tm, k), lambda j, i: (i, 0)),
                  pl.BlockSpec((k, tn), lambda j, i: (0, j))],
        out_specs=pl.BlockSpec((tm, tn), lambda j, i: (i, j)),
        out_shape=jax.ShapeDtypeStruct((m, n), out_dtype),
        scratch_shapes=[pltpu.VMEM((k, tn), bf16)],
        compiler_params=_params(("arbitrary", "arbitrary")),
        name="matmul",
    )(x, w)


def _mm_nt_kernel(x_ref, wt_ref, o_ref, wb_ref, *, valid):
    @pl.when(pl.program_id(1) == 0)
    def _():
        w = wt_ref[...]
        if valid < wt_ref.shape[0]:
            w = jnp.where(lax.broadcasted_iota(jnp.int32, w.shape, 0) < valid, w, 0.0)
        wb_ref[...] = w.astype(bf16)

    o_ref[...] = lax.dot_general(x_ref[...].astype(bf16), wb_ref[...], NT_DIMS,
                                 preferred_element_type=f32).astype(o_ref.dtype)


def _matmul_nt(x, wt, out_dtype, row0, n, tm=1024, tn=1024):
    m, k = x.shape
    tm, tn = min(tm, m), min(tn, n)
    assert m % tm == 0 and n % tn == 0 and row0 % tn == 0
    valid = min(tn, wt.shape[0] - row0 - (n - tn))
    assert valid == tn or n == tn
    return pl.pallas_call(
        functools.partial(_mm_nt_kernel, valid=valid),
        grid=(n // tn, m // tm),
        in_specs=[pl.BlockSpec((tm, k), lambda j, i: (i, 0)),
                  pl.BlockSpec((tn, k), lambda j, i: (row0 // tn + j, 0))],
        out_specs=pl.BlockSpec((tm, tn), lambda j, i: (i, j)),
        out_shape=jax.ShapeDtypeStruct((m, n), out_dtype),
        scratch_shapes=[pltpu.VMEM((tn, k), bf16)],
        compiler_params=_params(("arbitrary", "arbitrary")),
        name="matmul_nt",
    )(x, wt)


def _retention_tables(chunk):
    h = np.arange(RET_HEADS, dtype=np.float64)
    log_g = np.log(1.0 - 2.0 ** (-5.0 - h))
    j = np.arange(chunk, dtype=np.float64)
    diff = j[:, None] - j[None, :]
    decay_in = np.where(diff[None] >= 0, np.exp(np.maximum(diff, 0.0)[None] * log_g[:, None, None]), 0.0)
    xi = np.exp((j[None, :] + 1.0) * log_g[:, None])
    zeta = np.exp((chunk - 1.0 - j[None, :]) * log_g[:, None])
    chunk_decay = np.exp(chunk * log_g)
    xi_b = np.broadcast_to(xi[:, :, None], (RET_HEADS, chunk, RET_HEAD_DIM))
    zeta_b = np.broadcast_to(zeta[:, :, None], (RET_HEADS, chunk, RET_HEAD_DIM))
    return jnp.asarray(decay_in, f32), jnp.asarray(xi_b, f32), jnp.asarray(zeta_b, f32), chunk_decay


def _rope_tables(seq, half):
    inv = ROPE_THETA ** (-np.arange(half, dtype=np.float64) / half)
    ang = np.arange(seq, dtype=np.float64)[:, None] * inv[None, :]
    return np.cos(ang), np.sin(ang)


def _retention_kernel(q_ref, k_ref, v_ref, g_ref, cos_ref, sin_ref, din_ref, xi_ref, zeta_ref,
                      gn_ref, o_ref, state_ref, *, chunk_decay):
    @pl.when(pl.program_id(1) == 0)
    def _():
        state_ref[...] = jnp.zeros_like(state_ref)

    cos = cos_ref[...]
    sin = sin_ref[...]
    dh = RET_HEAD_DIM
    half = dh // 2

    def rope(x):
        x1, x2 = x[:, :half], x[:, half:]
        return jnp.concatenate([x1 * cos - x2 * sin, x2 * cos + x1 * sin], axis=1)

    for h in range(RET_HEADS):
        cols = slice(h * dh, (h + 1) * dh)
        q = rope(q_ref[0, :, cols])
        k = rope(k_ref[0, :, cols]) * (dh ** -0.5)
        qb = q.astype(bf16)
        kb = k.astype(bf16)
        vb = v_ref[0, :, cols].astype(bf16)
        attn = lax.dot_general(qb, kb, NT_DIMS, preferred_element_type=f32) * din_ref[h]
        inner = jnp.dot(attn.astype(bf16), vb, preferred_element_type=f32)
        state = state_ref[h]
        cross = jnp.dot(qb, state.astype(bf16), preferred_element_type=f32) * xi_ref[h]
        kz = (k * zeta_ref[h]).astype(bf16)
        state_ref[h] = state * chunk_decay[h] + lax.dot_general(kz, vb, TN_DIMS, preferred_element_type=f32)
        y = inner + cross
        mu = jnp.mean(y, axis=-1, keepdims=True)
        yc = y - mu
        var = jnp.mean(yc * yc, axis=-1, keepdims=True)
        yn = yc * lax.rsqrt(var + LN_EPS) * gn_ref[:, cols]
        g = g_ref[0, :, cols]
        o_ref[0, :, cols] = (g * jax.nn.sigmoid(g) * yn).astype(o_ref.dtype)


def _retention(proj, gn_g, batch, seq):
    c = min(RET_BLOCK, seq)
    dh = RET_HEAD_DIM
    w = RET_WIDTH
    decay_in, xi_b, zeta_b, chunk_decay = _retention_tables(c)
    cos, sin = _rope_tables(seq, dh // 2)
    col = lambda j: pl.BlockSpec((1, c, w), lambda b, s, j=j: (b, s, j))
    whole = lambda shape: pl.BlockSpec(shape, lambda b, s: (0,) * len(shape))
    return pl.pallas_call(
        functools.partial(_retention_kernel, chunk_decay=[float(x) for x in chunk_decay]),
        grid=(batch, seq // c),
        in_specs=[col(0), col(1), col(2), col(3),
                  pl.BlockSpec((c, dh // 2), lambda b, s: (s, 0)),
                  pl.BlockSpec((c, dh // 2), lambda b, s: (s, 0)),
                  whole((RET_HEADS, c, c)), whole((RET_HEADS, c, dh)), whole((RET_HEADS, c, dh)),
                  whole((1, w))],
        out_specs=pl.BlockSpec((1, c, w), lambda b, s: (b, s, 0)),
        out_shape=jax.ShapeDtypeStruct((batch, seq, w), bf16),
        scratch_shapes=[pltpu.VMEM((RET_HEADS, dh, dh), f32)],
        compiler_params=_params(("parallel", "arbitrary")),
        name="retention",
    )(proj, proj, proj, proj, jnp.asarray(cos, f32), jnp.asarray(sin, f32),
      decay_in, xi_b, zeta_b, gn_g.reshape(1, w))


CONV_ROWS = 128
CONV_HIST = 32
CONV_SUB = 32


def _conv_kernel(ga_ref, gb_ref, w_ref, b_ref, lg_ref, lb_ref, o_ref, ush):
    total = CONV_HIST + CONV_ROWS

    @pl.when(pl.program_id(1) == 0)
    def _():
        ush[0, 0:CONV_HIST, :] = jnp.zeros((CONV_HIST, CONV_CHANNELS), f32)

    ush[0, CONV_HIST:total, :] = ga_ref[0] * jax.nn.sigmoid(gb_ref[0])
    for r in range(1, SUBLANES):
        ush[r, 0:total - SUBLANES, :] = ush[0, r:r + total - SUBLANES, :]
    first = CONV_HIST - (CONV_TAPS - 1)
    for r0 in range(0, CONV_ROWS, CONV_SUB):
        acc = jnp.zeros((CONV_SUB, CONV_CHANNELS), f32)
        for t in range(CONV_TAPS):
            d = first + t
            lo = r0 + d - d % SUBLANES
            acc = acc + w_ref[t:t + 1, :] * ush[d % SUBLANES, lo:lo + CONV_SUB, :]
        y = _layer_norm(acc + b_ref[...], lg_ref[...], lb_ref[...])
        o_ref[0, r0:r0 + CONV_SUB, :] = (y * jax.nn.sigmoid(y)).astype(o_ref.dtype)
    ush[0, 0:CONV_HIST, :] = ush[0, CONV_ROWS:total, :]


def _conformer_conv(proj, conv_w, conv_b, ln_g, ln_b, batch, seq):
    c = CONV_CHANNELS
    a_blk = 4 * RET_WIDTH // c
    row = pl.BlockSpec((1, c), lambda b, s: (0, 0))
    return pl.pallas_call(
        _conv_kernel,
        grid=(batch, seq // CONV_ROWS),
        in_specs=[pl.BlockSpec((1, CONV_ROWS, c), lambda b, s: (b, s, a_blk)),
                  pl.BlockSpec((1, CONV_ROWS, c), lambda b, s: (b, s, a_blk + 1)),
                  pl.BlockSpec((CONV_TAPS, c), lambda b, s: (0, 0)),
                  row, row, row],
        out_specs=pl.BlockSpec((1, CONV_ROWS, c), lambda b, s: (b, s, 0)),
        out_shape=jax.ShapeDtypeStruct((batch, seq, c), bf16),
        scratch_shapes=[pltpu.VMEM((SUBLANES, CONV_HIST + CONV_ROWS, c), f32)],
        compiler_params=_params(("parallel", "arbitrary")),
        name="conformer_conv",
    )(proj, proj, conv_w, conv_b.reshape(1, c), ln_g.reshape(1, c), ln_b.reshape(1, c))


def _route(sel, aff):
    epg = EXPERTS_PER_GROUP
    gscore = []
    for g in range(N_GROUPS):
        v = sel[g * epg:(g + 1) * epg]
        best = None
        for a in range(epg):
            for b in range(a + 1, epg):
                s = v[a] + v[b]
                best = s if best is None else jnp.maximum(best, s)
        gscore.append(best)
    gmax = functools.reduce(jnp.maximum, gscore)
    taken = None
    gsel = []
    for g in range(N_GROUPS):
        hit = gscore[g] == gmax
        if taken is not None:
            hit = jnp.logical_and(hit, jnp.logical_not(taken))
        taken = hit if taken is None else jnp.logical_or(taken, hit)
        gsel.append(hit)

    def pick(rows, j):
        out = rows[(N_GROUPS - 1) * epg + j]
        for g in range(N_GROUPS - 2, -1, -1):
            out = jnp.where(gsel[g], rows[g * epg + j], out)
        return out

    x = [pick(sel, j) for j in range(epg)]
    a = [pick(aff, j) for j in range(epg)]
    base = jnp.zeros_like(x[0], dtype=jnp.int32)
    for g in range(1, N_GROUPS):
        base = jnp.where(gsel[g], g * epg, base)

    def first_max(vals, excluded):
        vmax = functools.reduce(jnp.maximum, vals)
        taken = None
        hits = []
        for j in range(epg):
            hit = vals[j] == vmax
            if excluded is not None:
                hit = jnp.logical_and(hit, jnp.logical_not(excluded[j]))
            if taken is not None:
                hit = jnp.logical_and(hit, jnp.logical_not(taken))
            taken = hit if taken is None else jnp.logical_or(taken, hit)
            hits.append(hit)
        return hits

    h1 = first_max(x, None)
    x2 = [jnp.where(h1[j], -jnp.inf, x[j]) for j in range(epg)]
    h2 = first_max(x2, h1)

    def gather(hits):
        idx = base
        val = jnp.zeros_like(a[0])
        for j in range(epg):
            idx = jnp.where(hits[j], base + j, idx)
            val = jnp.where(hits[j], a[j], val)
        return idx, val

    e0, a0 = gather(h1)
    e1, a1 = gather(h2)
    tot = a0 + a1
    return e0, e1, a0 / tot, a1 / tot


def _router_weights(router_w):
    d, e = router_w.shape
    w = router_w.astype(f32)
    w_hi = w.astype(bf16)
    w_lo = (w - w_hi.astype(f32)).astype(bf16)
    top = jnp.concatenate([w_hi, w_lo, jnp.zeros((d, LANES - 2 * e), bf16)], axis=1)
    bot = jnp.concatenate([w_hi, jnp.zeros((d, LANES - e), bf16)], axis=1)
    return jnp.concatenate([top, bot], axis=0)


def _proj_norm_router_kernel(a_ref, w_ref, h_ref, g_ref, b_ref, rw_ref, rb_ref,
                             hn_ref, hb_ref, ridx_ref, rwgt_ref, *, alpha):
    m = jnp.dot(a_ref[...], w_ref[...], preferred_element_type=f32)
    hn = _layer_norm(alpha * h_ref[...] + m, g_ref[...], b_ref[...])
    hn_ref[...] = hn
    hi = hn.astype(bf16)
    hb_ref[...] = hi
    lo = (hn - hi.astype(f32)).astype(bf16)
    d = hn.shape[1]
    parts = (jnp.dot(hi, rw_ref[0:d, :], preferred_element_type=f32)
             + jnp.dot(lo, rw_ref[d:2 * d, :], preferred_element_type=f32))
    parts = parts.T
    logits = parts[0:N_EXPERTS] + parts[N_EXPERTS:2 * N_EXPERTS]
    aff = jax.nn.sigmoid(logits)
    sel = aff + rb_ref[...]
    e0, e1, w0, w1 = _route([sel[e:e + 1, :] for e in range(N_EXPERTS)],
                            [aff[e:e + 1, :] for e in range(N_EXPERTS)])
    ridx_ref[0:1, :] = e0
    ridx_ref[1:2, :] = e1
    rwgt_ref[0:1, :] = w0
    rwgt_ref[1:2, :] = w1


def _proj_norm_router(a, w, h, ln_g, ln_b, router_ws, router_b, alpha, tm=512):
    t, k = a.shape
    d = w.shape[1]
    tm = min(tm, t)
    row = pl.BlockSpec((1, d), lambda i: (0, 0))
    return pl.pallas_call(
        functools.partial(_proj_norm_router_kernel, alpha=alpha),
        grid=(t // tm,),
        in_specs=[pl.BlockSpec((tm, k), lambda i: (i, 0)),
                  pl.BlockSpec((k, d), lambda i: (0, 0)),
                  pl.BlockSpec((tm, d), lambda i: (i, 0)),
                  row, row,
                  pl.BlockSpec((2 * d, LANES), lambda i: (0, 0)),
                  pl.BlockSpec((N_EXPERTS, 1), lambda i: (0, 0))],
        out_specs=[pl.BlockSpec((tm, d), lambda i: (i, 0)),
                   pl.BlockSpec((tm, d), lambda i: (i, 0)),
                   pl.BlockSpec((2, tm), lambda i: (0, i)),
                   pl.BlockSpec((2, tm), lambda i: (0, i))],
        out_shape=[jax.ShapeDtypeStruct((t, d), f32), jax.ShapeDtypeStruct((t, d), bf16),
                   jax.ShapeDtypeStruct((2, t), jnp.int32), jax.ShapeDtypeStruct((2, t), f32)],
        compiler_params=_params(("parallel",)),
        name="proj_norm_router",
    )(a, w, h, ln_g.reshape(1, d), ln_b.reshape(1, d), router_ws, router_b.reshape(N_EXPERTS, 1))


MOE_TILE = 512


def _experts_kernel(te_ref, tv_ref, x_ref, wgu_ref, wd_ref, o_ref, wgu_b, wd_b):
    i = pl.program_id(0)
    prev = te_ref[jnp.maximum(i - 1, 0)]

    @pl.when(jnp.logical_or(i == 0, te_ref[i] != prev))
    def _():
        wgu_b[...] = wgu_ref[0].astype(bf16)
        wd_b[...] = wd_ref[0].astype(bf16)

    @pl.when(tv_ref[i] > 0)
    def _():
        hgu = jnp.dot(x_ref[...], wgu_b[...], preferred_element_type=f32)
        a, b = hgu[:, :EXPERT_FF], hgu[:, EXPERT_FF:]
        act = (a * jax.nn.sigmoid(a) * b).astype(bf16)
        o_ref[...] = jnp.dot(act, wd_b[...], preferred_element_type=f32).astype(o_ref.dtype)

    @pl.when(tv_ref[i] == 0)
    def _():
        o_ref[...] = jnp.zeros_like(o_ref)


def _experts(xs, w_gu, w_down, layer, tile_expert, tile_valid):
    r, d = xs.shape
    ff2 = w_gu.shape[3]
    grid_spec = pltpu.PrefetchScalarGridSpec(
        num_scalar_prefetch=2,
        grid=(r // MOE_TILE,),
        in_specs=[pl.BlockSpec((MOE_TILE, d), lambda i, te, tv: (i, 0)),
                  pl.BlockSpec((None, 1, d, ff2), lambda i, te, tv: (layer, te[i], 0, 0)),
                  pl.BlockSpec((None, 1, ff2 // 2, d), lambda i, te, tv: (layer, te[i], 0, 0))],
        out_specs=pl.BlockSpec((MOE_TILE, d), lambda i, te, tv: (i, 0)),
        scratch_shapes=[pltpu.VMEM((d, ff2), bf16), pltpu.VMEM((ff2 // 2, d), bf16)],
    )
    return pl.pallas_call(
        _experts_kernel,
        grid_spec=grid_spec,
        out_shape=jax.ShapeDtypeStruct((r, d), bf16),
        compiler_params=_params(("arbitrary",)),
        name="experts",
    )(tile_expert, tile_valid, xs, w_gu, w_down)


def _dispatch(ridx):
    t = ridx.shape[1]
    slots = 2 * t
    es = ridx.T.reshape(slots)
    experts = jnp.arange(N_EXPERTS, dtype=jnp.int32)
    onehot = (es[:, None] == experts[None, :]).astype(jnp.int32)
    csum = jnp.cumsum(onehot, axis=0)
    rank = jnp.sum((csum - onehot) * onehot, axis=1)
    counts = csum[-1]
    tiles_per = (counts + MOE_TILE - 1) // MOE_TILE
    tile_end = jnp.cumsum(tiles_per)
    start = (tile_end - tiles_per) * MOE_TILE
    dest = start[es] + rank
    n_tiles = slots // MOE_TILE + N_EXPERTS
    tile_ids = jnp.arange(n_tiles, dtype=jnp.int32)
    tile_valid = (tile_ids < tile_end[-1]).astype(jnp.int32)
    tile_expert = jnp.sum((tile_ids[:, None] >= tile_end[None, :]).astype(jnp.int32), axis=1)
    last_used = jnp.max(jnp.where(counts > 0, experts, 0))
    tile_expert = jnp.minimum(tile_expert, last_used)
    rows = n_tiles * MOE_TILE
    tok = (jnp.arange(rows, dtype=jnp.int32) % t).at[dest].set(jnp.arange(slots, dtype=jnp.int32) // 2)
    return tok, dest, tile_expert, tile_valid


def _combine_norm_kernel(h_ref, y0_ref, y1_ref, w0_ref, w1_ref, g_ref, b_ref, o_ref, ob_ref, *, alpha):
    f = w0_ref[...] * y0_ref[...].astype(f32) + w1_ref[...] * y1_ref[...].astype(f32)
    hn = _layer_norm(alpha * h_ref[...] + f, g_ref[...], b_ref[...])
    o_ref[...] = hn
    ob_ref[...] = hn.astype(bf16)


def _combine_norm(h, y, w0, w1, ln_g, ln_b, alpha, tm=512):
    t, d = h.shape
    tm = min(tm, t)
    nt = t // tm
    blk = pl.BlockSpec((tm, d), lambda i: (i, 0))
    col = pl.BlockSpec((tm, 1), lambda i: (i, 0))
    row = pl.BlockSpec((1, d), lambda i: (0, 0))
    return pl.pallas_call(
        functools.partial(_combine_norm_kernel, alpha=alpha),
        grid=(nt,),
        in_specs=[blk, blk, pl.BlockSpec((tm, d), lambda i: (i + nt, 0)), col, col, row, row],
        out_specs=[blk, blk],
        out_shape=[jax.ShapeDtypeStruct((t, d), f32), jax.ShapeDtypeStruct((t, d), bf16)],
        compiler_params=_params(("parallel",)),
        name="combine_norm",
    )(h, y, y, w0, w1, ln_g.reshape(1, d), ln_b.reshape(1, d))


def _moe(hn, hb, ridx, rwgt, w_gu, w_down, layer, ln_g, ln_b, alpha):
    t, d = hn.shape
    tok, dest, tile_expert, tile_valid = _dispatch(ridx)
    xs = hb.at[tok].get(mode="promise_in_bounds")
    ys = _experts(xs, w_gu, w_down, layer, tile_expert, tile_valid)
    y = ys.at[dest.reshape(t, 2).T.reshape(2 * t)].get(mode="promise_in_bounds")
    return _combine_norm(hn, y, rwgt[0].reshape(t, 1), rwgt[1].reshape(t, 1), ln_g, ln_b, alpha)


DSA_Q = DSA_HEADS * DSA_HEAD_DIM
DSA_KV = DSA_KV_HEADS * DSA_HEAD_DIM
DSA_QI = IDX_HEADS * IDX_DIM
DSA_MAIN = DSA_Q + 2 * DSA_KV + DSA_QI


def _lane_iota(shape):
    return lax.broadcasted_iota(jnp.int32, shape, 1)


def _rope128(x, cos2, sin2):
    return x * cos2 + pltpu.roll(x, LANES // 2, 1) * sin2


def _rope64(x, cos4, sin4):
    lane = _lane_iota(x.shape)
    partner = jnp.where(lane % IDX_DIM < IDX_DIM // 2,
                        pltpu.roll(x, LANES - IDX_DIM // 2, 1), pltpu.roll(x, IDX_DIM // 2, 1))
    return x * cos4 + partner * sin4


def _dsa_prep_kernel(p_ref, t_ref, c2_ref, s2_ref, c4_ref, s4_ref,
                     q_ref, k_ref, v_ref, qi_ref, ki_ref):
    c2, s2, c4, s4 = c2_ref[...], s2_ref[...], c4_ref[...], s4_ref[...]
    qscale = LOG2E * DSA_HEAD_DIM ** -0.5
    for j in range(DSA_Q // LANES):
        x = p_ref[0, :, j * LANES:(j + 1) * LANES]
        q_ref[0, :, j * LANES:(j + 1) * LANES] = (_rope128(x, c2, s2) * qscale).astype(bf16)
    for j in range(DSA_KV // LANES):
        o = DSA_Q + j * LANES
        k_ref[0, :, j * LANES:(j + 1) * LANES] = _rope128(p_ref[0, :, o:o + LANES], c2, s2).astype(bf16)
        o = DSA_Q + DSA_KV + j * LANES
        v_ref[0, :, j * LANES:(j + 1) * LANES] = p_ref[0, :, o:o + LANES].astype(bf16)
    lane = _lane_iota((p_ref.shape[1], LANES))
    low = lane < IDX_DIM
    iscale = IDX_DIM ** -0.5
    for j in range(DSA_QI // LANES):
        o = DSA_Q + 2 * DSA_KV + j * LANES
        r = _rope64(p_ref[0, :, o:o + LANES], c4, s4) * iscale
        qi_ref[0, 2 * j] = jnp.where(low, r, 0.0).astype(bf16)
        qi_ref[0, 2 * j + 1] = jnp.where(low, pltpu.roll(r, LANES // 2, 1), 0.0).astype(bf16)
    ki_ref[0] = jnp.where(low, _rope64(t_ref[0], c4, s4), 0.0).astype(bf16)


def _dsa_prep(main, tail, batch, seq, ts=256):
    ts = min(ts, seq)
    c, s = _rope_tables(seq, DSA_HEAD_DIM // 2)
    c2 = jnp.asarray(np.concatenate([c, c], axis=1), f32)
    s2 = jnp.asarray(np.concatenate([-s, s], axis=1), f32)
    c, s = _rope_tables(seq, IDX_DIM // 2)
    c4 = jnp.asarray(np.concatenate([c, c, c, c], axis=1), f32)
    s4 = jnp.asarray(np.concatenate([-s, s, -s, s], axis=1), f32)
    tab = pl.BlockSpec((ts, LANES), lambda b, i: (i, 0))
    out = lambda w: pl.BlockSpec((1, ts, w), lambda b, i: (b, i, 0))
    return pl.pallas_call(
        _dsa_prep_kernel,
        grid=(batch, seq // ts),
        in_specs=[out(DSA_MAIN), out(LANES), tab, tab, tab, tab],
        out_specs=[out(DSA_Q), out(DSA_KV), out(DSA_KV),
                   pl.BlockSpec((1, IDX_HEADS, ts, LANES), lambda b, i: (b, 0, i, 0)), out(LANES)],
        out_shape=[jax.ShapeDtypeStruct((batch, seq, DSA_Q), bf16),
                   jax.ShapeDtypeStruct((batch, seq, DSA_KV), bf16),
                   jax.ShapeDtypeStruct((batch, seq, DSA_KV), bf16),
                   jax.ShapeDtypeStruct((batch, IDX_HEADS, seq, LANES), bf16),
                   jax.ShapeDtypeStruct((batch, seq, LANES), bf16)],
        compiler_params=_params(("parallel", "parallel")),
        name="dsa_prep",
    )(main, tail, c2, s2, c4, s4)


SEL_ROWS = 256
SEL_CHUNK = 512
SEL_SUB = 256


def _order_key(x):
    bits = pltpu.bitcast(x, jnp.int32)
    return bits ^ ((bits >> 31) & 0x7FFFFFFF)


def _select_kernel(qi_ref, ki_ref, t_ref, bias_ref, key_ref, wb_ref, hi_ref, lo_ref, *, n_sel, n_chunks):
    tq, kc = SEL_ROWS, SEL_CHUNK
    n_sub = kc // SEL_SUB
    qt = pl.program_id(1)
    q0 = qt * tq
    n_act = lax.div(qt, kc // tq) + 1
    qpos = q0 + lax.broadcasted_iota(jnp.int32, (tq, 1), 0)
    wscale = IDX_HEADS ** -0.5

    for h in range(IDX_HEADS):
        wb_ref[h] = jnp.broadcast_to(t_ref[0, :, IDX_DIM + h:IDX_DIM + h + 1] * wscale, (tq, LANES))
    q_all = qi_ref[0].reshape(IDX_HEADS * tq, LANES)

    def score_chunk(c, carry):
        off = pl.multiple_of(c * kc, kc)
        for s in range(kc // SEL_SUB):
            ki = ki_ref[0, pl.ds(off + s * SEL_SUB, SEL_SUB), :]
            r = lax.dot_general(q_all, ki, NT_DIMS, preferred_element_type=f32)
            acc = jnp.zeros((tq, SEL_SUB), f32)
            for h in range(IDX_HEADS):
                w = jnp.concatenate([wb_ref[h]] * (SEL_SUB // LANES), axis=1)
                acc = acc + jnp.maximum(r[h * tq:(h + 1) * tq], 0.0) * w
            kpos = off + s * SEL_SUB + _lane_iota((tq, SEL_SUB))
            sc = jnp.where(kpos <= qpos, acc + 0.0, -jnp.inf)
            key_ref[c, :, s * SEL_SUB:(s + 1) * SEL_SUB] = _order_key(sc)
            kt = _order_key(sc.T)
            hi_ref[c * n_sub + s] = (kt >> 16).astype(jnp.int16)
            lo_ref[c * n_sub + s] = ((kt & 0xFFFF) + I16_MIN).astype(jnp.int16)
        return carry

    lax.fori_loop(0, n_act, score_chunk, 0)
    n_blk = n_act * n_sub

    def count_ge16(ref, cand):
        cand = cand.astype(jnp.int16)

        def body(c, acc):
            for s in range(n_sub):
                hit = jnp.where(ref[c * n_sub + s] >= cand, jnp.int16(1), jnp.int16(0))
                for r in range(SEL_SUB // PACKED_ROWS):
                    acc = acc + hit[r * PACKED_ROWS:(r + 1) * PACKED_ROWS, :]
            return acc
        acc = lax.fori_loop(0, n_act, body, jnp.zeros((PACKED_ROWS, tq), jnp.int16))
        return jnp.sum(acc.astype(f32), axis=0, keepdims=True)

    def rank_select16(ref, rank):
        ans = jnp.where(count_ge16(ref, jnp.zeros((1, tq), jnp.int32)) >= rank, 0, I16_MIN).astype(jnp.int32)

        def bit_step(i, ans):
            cand = ans | jnp.left_shift(jnp.int32(1), 14 - i)
            return jnp.where(count_ge16(ref, cand) >= rank, cand, ans)

        return lax.fori_loop(0, 15, bit_step, ans)

    def count_gt16(ref, val):
        return jnp.where(val == I16_MAX, 0.0, count_ge16(ref, jnp.minimum(val + 1, I16_MAX)))

    kf = float(n_sel)
    top = rank_select16(hi_ref, kf)
    above = count_gt16(hi_ref, top)
    top16 = top.astype(jnp.int16)

    def low_block(j, carry):
        lo_ref[j] = jnp.where(hi_ref[j] == top16, lo_ref[j], jnp.int16(I16_MIN))
        return carry

    lax.fori_loop(0, n_blk, low_block, 0)
    bottom = rank_select16(lo_ref, kf - above)
    need_row = kf - above - count_gt16(lo_ref, bottom)

    rows = jnp.concatenate([top.astype(f32), bottom.astype(f32), need_row,
                            jnp.zeros((LANES - 3, tq), f32)], axis=0).T
    ans = rows[:, 0:1].astype(jnp.int32) * 65536 + (rows[:, 1:2].astype(jnp.int32) - I16_MIN)
    need = rows[:, 2:3]
    ans = jnp.where(qpos < n_sel, INT_MIN, ans)

    upper = (lax.broadcasted_iota(jnp.int32, (SEL_SUB, SEL_SUB), 0)
             < lax.broadcasted_iota(jnp.int32, (SEL_SUB, SEL_SUB), 1)).astype(bf16)

    def emit_chunk(c, seen):
        off = pl.multiple_of(c * kc, kc)
        for s in range(kc // SEL_SUB):
            key = key_ref[c, :, s * SEL_SUB:(s + 1) * SEL_SUB]
            eq = key == ans
            eqf = jnp.where(eq, 1.0, 0.0)
            before = seen + jnp.dot(eqf.astype(bf16), upper, preferred_element_type=f32)
            keep = jnp.logical_or(key > ans, jnp.logical_and(eq, before < need))
            kpos = off + s * SEL_SUB + _lane_iota((tq, SEL_SUB))
            keep = jnp.logical_and(keep, kpos <= qpos)
            bias_ref[0, c, :, s * SEL_SUB:(s + 1) * SEL_SUB] = jnp.where(keep, 0.0, NEG_BIG).astype(bf16)
            seen = seen + jnp.sum(eqf, axis=1, keepdims=True)
        return seen

    lax.fori_loop(0, n_act, emit_chunk, jnp.zeros((tq, 1), f32))

    def fill_chunk(c, carry):
        bias_ref[0, c] = jnp.full((tq, kc), NEG_BIG, bf16)
        return carry

    lax.fori_loop(n_act, n_chunks, fill_chunk, 0)


def _select(qi, ki, tail, batch, seq, n_sel):
    n_chunks = seq // SEL_CHUNK
    assert SEL_CHUNK >= n_sel and SEL_CHUNK % SEL_ROWS == 0
    return pl.pallas_call(
        functools.partial(_select_kernel, n_sel=n_sel, n_chunks=n_chunks),
        grid=(batch, seq // SEL_ROWS),
        in_specs=[pl.BlockSpec((1, IDX_HEADS, SEL_ROWS, LANES), lambda b, i: (b, 0, i, 0)),
                  pl.BlockSpec((1, seq, LANES), lambda b, i: (b, 0, 0)),
                  pl.BlockSpec((1, SEL_ROWS, LANES), lambda b, i: (b, i, 0))],
        out_specs=pl.BlockSpec((1, n_chunks, SEL_ROWS, SEL_CHUNK), lambda b, i: (b, 0, i, 0)),
        out_shape=jax.ShapeDtypeStruct((batch, n_chunks, seq, SEL_CHUNK), bf16),
        scratch_shapes=[pltpu.VMEM((n_chunks, SEL_ROWS, SEL_CHUNK), jnp.int32),
                        pltpu.VMEM((IDX_HEADS, SEL_ROWS, LANES), f32),
                        pltpu.VMEM((seq // SEL_SUB, SEL_SUB, SEL_ROWS), jnp.int16),
                        pltpu.VMEM((seq // SEL_SUB, SEL_SUB, SEL_ROWS), jnp.int16)],
        compiler_params=_params(("parallel", "parallel")),
        name="dsa_select",
    )(qi, ki, tail)


ATT_TILE = SEL_CHUNK
ATT_GROUP = DSA_HEADS // DSA_KV_HEADS


def _attention_kernel(qt_ref, kt_ref, q_ref, k_ref, v_ref, b_ref, o_ref, m_ref, acc_ref):
    p_id = pl.program_id(2)
    i = qt_ref[p_id]
    j = kt_ref[p_id]
    dh = DSA_HEAD_DIM
    t = ATT_TILE

    @pl.when(j == 0)
    def _():
        m_ref[...] = jnp.full(m_ref.shape, NEG_BIG, f32)
        acc_ref[...] = jnp.zeros_like(acc_ref)

    k = k_ref[0]
    v_ones = jnp.concatenate([v_ref[0], jnp.ones((t, dh), bf16)], axis=1)
    for h in range(ATT_GROUP):
        s = lax.dot_general(q_ref[0, :, h * dh:(h + 1) * dh], k, NT_DIMS, preferred_element_type=f32)
        s = s + b_ref[0, 0].astype(f32)
        m_old = m_ref[h]
        m_new = jnp.maximum(m_old, jnp.max(s, axis=1, keepdims=True))
        p = jnp.concatenate([jnp.exp2(s[:, c * LANES:(c + 1) * LANES] - m_new) for c in range(t // LANES)],
                            axis=1)
        scale = jnp.exp2(m_old - m_new)
        pv = jnp.dot(p.astype(bf16), v_ones, preferred_element_type=f32)
        acc_ref[h] = jnp.concatenate([scale, scale], axis=1) * acc_ref[h] + pv
        m_ref[h] = m_new

    @pl.when(j == i)
    def _():
        for h in range(ATT_GROUP):
            acc = acc_ref[h]
            o_ref[0, :, h * dh:(h + 1) * dh] = (acc[:, :dh] / acc[:, dh:]).astype(o_ref.dtype)


def _attention(q, k, v, bias, batch, seq):
    t = ATT_TILE
    n = seq // t
    dh = DSA_HEAD_DIM
    gw = ATT_GROUP * dh
    pairs = [(i, j) for i in range(n) for j in range(i + 1)]
    q_tile = jnp.asarray([p[0] for p in pairs], jnp.int32)
    k_tile = jnp.asarray([p[1] for p in pairs], jnp.int32)
    grid_spec = pltpu.PrefetchScalarGridSpec(
        num_scalar_prefetch=2,
        grid=(batch, DSA_KV_HEADS, len(pairs)),
        in_specs=[pl.BlockSpec((1, t, gw), lambda b, g, p, qt, kt: (b, qt[p], g)),
                  pl.BlockSpec((1, t, dh), lambda b, g, p, qt, kt: (b, kt[p], g)),
                  pl.BlockSpec((1, t, dh), lambda b, g, p, qt, kt: (b, kt[p], g)),
                  pl.BlockSpec((1, 1, t, t), lambda b, g, p, qt, kt: (b, kt[p], qt[p], 0))],
        out_specs=pl.BlockSpec((1, t, gw), lambda b, g, p, qt, kt: (b, qt[p], g)),
        scratch_shapes=[pltpu.VMEM((ATT_GROUP, t, LANES), f32), pltpu.VMEM((ATT_GROUP, t, 2 * dh), f32)],
    )
    return pl.pallas_call(
        _attention_kernel,
        grid_spec=grid_spec,
        out_shape=jax.ShapeDtypeStruct((batch, seq, DSA_Q), bf16),
        compiler_params=_params(("parallel", "parallel", "arbitrary")),
        name="dsa_attention",
    )(q_tile, k_tile, q, k, v, bias)


def _even_mixer(hb, w_in, gn_g, conv_w, conv_b, conv_ln_g, conv_ln_b, batch, seq):
    proj = _matmul(hb, w_in, f32).reshape(batch, seq, -1)
    ret = _retention(proj, gn_g, batch, seq)
    conv = _conformer_conv(proj, conv_w, conv_b, conv_ln_g, conv_ln_b, batch, seq)
    return jnp.concatenate([ret, conv], axis=-1).reshape(batch * seq, -1)


def _odd_mixer(hb, w_in, batch, seq):
    wt = w_in.T
    main = _matmul_nt(hb, wt, f32, 0, DSA_MAIN).reshape(batch, seq, DSA_MAIN)
    tail = _matmul_nt(hb, wt, f32, DSA_MAIN, LANES).reshape(batch, seq, LANES)
    q, k, v, qi, ki = _dsa_prep(main, tail, batch, seq)
    bias = _select(qi, ki, tail, batch, seq, min(TOPK_MAX, seq // 4))
    return _attention(q, k, v, bias, batch, seq).reshape(batch * seq, DSA_Q)


def kernel(x, even_w_in, even_ret_gn_g, even_conv_w, even_conv_b, even_conv_ln_g, even_conv_ln_b, even_w_out,
           odd_w_in, odd_w_out, mix_ln_g, mix_ln_b, moe_w_gu, moe_w_down, ffn_ln_g, ffn_ln_b, router_w, router_b):
    batch, seq, d = x.shape
    depth = mix_ln_g.shape[0]
    alpha = (2 * depth) ** 0.25
    router_ws = _router_weights(router_w)
    h = x.reshape(batch * seq, d).astype(f32)
    hb = h
    for layer in range(depth):
        i = layer // 2
        if layer % 2 == 0:
            mixed = _even_mixer(hb, even_w_in[i], even_ret_gn_g[i], even_conv_w[i], even_conv_b[i],
                                even_conv_ln_g[i], even_conv_ln_b[i], batch, seq)
            w_out = even_w_out[i]
        else:
            mixed = _odd_mixer(hb, odd_w_in[i], batch, seq)
            w_out = odd_w_out[i]
        hn, hnb, ridx, rwgt = _proj_norm_router(mixed, w_out.astype(bf16), h, mix_ln_g[layer], mix_ln_b[layer],
                                                router_ws, router_b.astype(f32), alpha)
        h, hb = _moe(hn, hnb, ridx, rwgt, moe_w_gu, moe_w_down, layer,
                     ffn_ln_g[layer], ffn_ln_b[layer], alpha)
    return h.reshape(batch, seq, d).astype(x.dtype)
```

```python
import functools

import numpy as np
import jax
import jax.numpy as jnp
from jax import lax
from jax.experimental import pallas as pl
from jax.experimental.pallas import tpu as pltpu

f32 = jnp.float32
bf16 = jnp.bfloat16

ROPE_THETA = 10000.0
LN_EPS = 1e-5
RET_HEADS = 4
RET_HEAD_DIM = 256
RET_WIDTH = RET_HEADS * RET_HEAD_DIM
RET_BLOCK = 256
CONV_CHANNELS = 1024
CONV_TAPS = 31
DSA_HEADS = 16
DSA_HEAD_DIM = 128
DSA_KV_HEADS = 4
IDX_HEADS = 16
IDX_DIM = 64
TOPK_MAX = 256
N_EXPERTS = 16
N_GROUPS = 4
EXPERTS_PER_GROUP = N_EXPERTS // N_GROUPS
EXPERT_FF = 512

LANES = 128
SUBLANES = 8
PACKED_ROWS = 16
VMEM_LIMIT = 56 * 1024 * 1024
NEG_BIG = -1e30
LOG2E = 1.4426950408889634
INT_MIN = -2 ** 31
I16_MIN, I16_MAX = -2 ** 15, 2 ** 15 - 1

NT_DIMS = (((1,), (1,)), ((), ()))
TN_DIMS = (((0,), (0,)), ((), ()))


def _params(sem):
    return pltpu.CompilerParams(dimension_semantics=sem, vmem_limit_bytes=VMEM_LIMIT)


def _layer_norm(z, g, b):
    mu = jnp.mean(z, axis=-1, keepdims=True)
    zc = z - mu
    var = jnp.mean(zc * zc, axis=-1, keepdims=True)
    return zc * lax.rsqrt(var + LN_EPS) * g + b


def _mm_kernel(x_ref, w_ref, o_ref, wb_ref):
    @pl.when(pl.program_id(1) == 0)
    def _():
        wb_ref[...] = w_ref[...].astype(bf16)

    o_ref[...] = jnp.dot(x_ref[...].astype(bf16), wb_ref[...], preferred_element_type=f32).astype(o_ref.dtype)


def _matmul(x, w, out_dtype, n=None, tm=1024, tn=1024):
    m, k = x.shape
    n = w.shape[1] if n is None else n
    tm, tn = min(tm, m), min(tn, n)
    assert m % tm == 0 and n % tn == 0
    return pl.pallas_call(
        _mm_kernel,
        grid=(n // tn, m // tm),
        in_specs=[pl.BlockSpec((tm, k), lambda j, i: (i, 0)),
                  pl.BlockSpec((k, tn), lambda j, i: (0, j))],
        out_specs=pl.BlockSpec((tm, tn), lambda j, i: (i, j)),
        out_shape=jax.ShapeDtypeStruct((m, n), out_dtype),
        scratch_shapes=[pltpu.VMEM((k, tn), bf16)],
        compiler_params=_params(("arbitrary", "arbitrary")),
        name="matmul",
    )(x, w)


def _mm_nt_kernel(x_ref, wt_ref, o_ref, wb_ref, *, valid):
    @pl.when(pl.program_id(1) == 0)
    def _():
        w = wt_ref[...]
        if valid < wt_ref.shape[0]:
            w = jnp.where(lax.broadcasted_iota(jnp.int32, w.shape, 0) < valid, w, 0.0)
        wb_ref[...] = w.astype(bf16)

    o_ref[...] = lax.dot_general(x_ref[...].astype(bf16), wb_ref[...], NT_DIMS,
                                 preferred_element_type=f32).astype(o_ref.dtype)


def _matmul_nt(x, wt, out_dtype, row0, n, tm=1024, tn=1024):
    m, k = x.shape
    tm, tn = min(tm, m), min(tn, n)
    assert m % tm == 0 and n % tn == 0 and row0 % tn == 0
    valid = min(tn, wt.shape[0] - row0 - (n - tn))
    assert valid == tn or n == tn
    return pl.pallas_call(
        functools.partial(_mm_nt_kernel, valid=valid),
        grid=(n // tn, m // tm),
        in_specs=[pl.BlockSpec((tm, k), lambda j, i: (i, 0)),
                  pl.BlockSpec((tn, k), lambda j, i: (row0 // tn + j, 0))],
        out_specs=pl.BlockSpec((tm, tn), lambda j, i: (i, j)),
        out_shape=jax.ShapeDtypeStruct((m, n), out_dtype),
        scratch_shapes=[pltpu.VMEM((tn, k), bf16)],
        compiler_params=_params(("arbitrary", "arbitrary")),
        name="matmul_nt",
    )(x, wt)


def _retention_tables(chunk):
    h = np.arange(RET_HEADS, dtype=np.float64)
    log_g = np.log(1.0 - 2.0 ** (-5.0 - h))
    j = np.arange(chunk, dtype=np.float64)
    diff = j[:, None] - j[None, :]
    decay_in = np.where(diff[None] >= 0, np.exp(np.maximum(diff, 0.0)[None] * log_g[:, None, None]), 0.0)
    xi = np.exp((j[None, :] + 1.0) * log_g[:, None])
    zeta = np.exp((chunk - 1.0 - j[None, :]) * log_g[:, None])
    chunk_decay = np.exp(chunk * log_g)
    xi_b = np.broadcast_to(xi[:, :, None], (RET_HEADS, chunk, RET_HEAD_DIM))
    zeta_b = np.broadcast_to(zeta[:, :, None], (RET_HEADS, chunk, RET_HEAD_DIM))
    return jnp.asarray(decay_in, f32), jnp.asarray(xi_b, f32), jnp.asarray(zeta_b, f32), chunk_decay


def _rope_tables(seq, half):
    inv = ROPE_THETA ** (-np.arange(half, dtype=np.float64) / half)
    ang = np.arange(seq, dtype=np.float64)[:, None] * inv[None, :]
    return np.cos(ang), np.sin(ang)


def _retention_kernel(q_ref, k_ref, v_ref, g_ref, cos_ref, sin_ref, din_ref, xi_ref, zeta_ref,
                      gn_ref, o_ref, state_ref, *, chunk_decay):
    @pl.when(pl.program_id(1) == 0)
    def _():
        state_ref[...] = jnp.zeros_like(state_ref)

    cos = cos_ref[...]
    sin = sin_ref[...]
    dh = RET_HEAD_DIM
    half = dh // 2

    def rope(x):
        x1, x2 = x[:, :half], x[:, half:]
        return jnp.concatenate([x1 * cos - x2 * sin, x2 * cos + x1 * sin], axis=1)

    for h in range(RET_HEADS):
        cols = slice(h * dh, (h + 1) * dh)
        q = rope(q_ref[0, :, cols])
        k = rope(k_ref[0, :, cols]) * (dh ** -0.5)
        qb = q.astype(bf16)
        kb = k.astype(bf16)
        vb = v_ref[0, :, cols].astype(bf16)
        attn = lax.dot_general(qb, kb, NT_DIMS, preferred_element_type=f32) * din_ref[h]
        inner = jnp.dot(attn.astype(bf16), vb, preferred_element_type=f32)
        state = state_ref[h]
        cross = jnp.dot(qb, state.astype(bf16), preferred_element_type=f32) * xi_ref[h]
        kz = (k * zeta_ref[h]).astype(bf16)
        state_ref[h] = state * chunk_decay[h] + lax.dot_general(kz, vb, TN_DIMS, preferred_element_type=f32)
        y = inner + cross
        mu = jnp.mean(y, axis=-1, keepdims=True)
        yc = y - mu
        var = jnp.mean(yc * yc, axis=-1, keepdims=True)
        yn = yc * lax.rsqrt(var + LN_EPS) * gn_ref[:, cols]
        g = g_ref[0, :, cols]
        o_ref[0, :, cols] = (g * jax.nn.sigmoid(g) * yn).astype(o_ref.dtype)


def _retention(proj, gn_g, batch, seq):
    c = min(RET_BLOCK, seq)
    dh = RET_HEAD_DIM
    w = RET_WIDTH
    decay_in, xi_b, zeta_b, chunk_decay = _retention_tables(c)
    cos, sin = _rope_tables(seq, dh // 2)
    col = lambda j: pl.BlockSpec((1, c, w), lambda b, s, j=j: (b, s, j))
    whole = lambda shape: pl.BlockSpec(shape, lambda b, s: (0,) * len(shape))
    return pl.pallas_call(
        functools.partial(_retention_kernel, chunk_decay=[float(x) for x in chunk_decay]),
        grid=(batch, seq // c),
        in_specs=[col(0), col(1), col(2), col(3),
                  pl.BlockSpec((c, dh // 2), lambda b, s: (s, 0)),
                  pl.BlockSpec((c, dh // 2), lambda b, s: (s, 0)),
                  whole((RET_HEADS, c, c)), whole((RET_HEADS, c, dh)), whole((RET_HEADS, c, dh)),
                  whole((1, w))],
        out_specs=pl.BlockSpec((1, c, w), lambda b, s: (b, s, 0)),
        out_shape=jax.ShapeDtypeStruct((batch, seq, w), bf16),
        scratch_shapes=[pltpu.VMEM((RET_HEADS, dh, dh), f32)],
        compiler_params=_params(("parallel", "arbitrary")),
        name="retention",
    )(proj, proj, proj, proj, jnp.asarray(cos, f32), jnp.asarray(sin, f32),
      decay_in, xi_b, zeta_b, gn_g.reshape(1, w))


CONV_ROWS = 128
CONV_HIST = 32
CONV_SUB = 32


def _conv_kernel(ga_ref, gb_ref, w_ref, b_ref, lg_ref, lb_ref, o_ref, ush):
    total = CONV_HIST + CONV_ROWS

    @pl.when(pl.program_id(1) == 0)
    def _():
        ush[0, 0:CONV_HIST, :] = jnp.zeros((CONV_HIST, CONV_CHANNELS), f32)

    ush[0, CONV_HIST:total, :] = ga_ref[0] * jax.nn.sigmoid(gb_ref[0])
    for r in range(1, SUBLANES):
        ush[r, 0:total - SUBLANES, :] = ush[0, r:r + total - SUBLANES, :]
    first = CONV_HIST - (CONV_TAPS - 1)
    for r0 in range(0, CONV_ROWS, CONV_SUB):
        acc = jnp.zeros((CONV_SUB, CONV_CHANNELS), f32)
        for t in range(CONV_TAPS):
            d = first + t
            lo = r0 + d - d % SUBLANES
            acc = acc + w_ref[t:t + 1, :] * ush[d % SUBLANES, lo:lo + CONV_SUB, :]
        y = _layer_norm(acc + b_ref[...], lg_ref[...], lb_ref[...])
        o_ref[0, r0:r0 + CONV_SUB, :] = (y * jax.nn.sigmoid(y)).astype(o_ref.dtype)
    ush[0, 0:CONV_HIST, :] = ush[0, CONV_ROWS:total, :]


def _conformer_conv(proj, conv_w, conv_b, ln_g, ln_b, batch, seq):
    c = CONV_CHANNELS
    a_blk = 4 * RET_WIDTH // c
    row = pl.BlockSpec((1, c), lambda b, s: (0, 0))
    return pl.pallas_call(
        _conv_kernel,
        grid=(batch, seq // CONV_ROWS),
        in_specs=[pl.BlockSpec((1, CONV_ROWS, c), lambda b, s: (b, s, a_blk)),
                  pl.BlockSpec((1, CONV_ROWS, c), lambda b, s: (b, s, a_blk + 1)),
                  pl.BlockSpec((CONV_TAPS, c), lambda b, s: (0, 0)),
                  row, row, row],
        out_specs=pl.BlockSpec((1, CONV_ROWS, c), lambda b, s: (b, s, 0)),
        out_shape=jax.ShapeDtypeStruct((batch, seq, c), bf16),
        scratch_shapes=[pltpu.VMEM((SUBLANES, CONV_HIST + CONV_ROWS, c), f32)],
        compiler_params=_params(("parallel", "arbitrary")),
        name="conformer_conv",
    )(proj, proj, conv_w, conv_b.reshape(1, c), ln_g.reshape(1, c), ln_b.reshape(1, c))


def _route(sel, aff):
    epg = EXPERTS_PER_GROUP
    gscore = []
    for g in range(N_GROUPS):
        v = sel[g * epg:(g + 1) * epg]
        best = None
        for a in range(epg):
            for b in range(a + 1, epg):
                s = v[a] + v[b]
                best = s if best is None else jnp.maximum(best, s)
        gscore.append(best)
    gmax = functools.reduce(jnp.maximum, gscore)
    taken = None
    gsel = []
    for g in range(N_GROUPS):
        hit = gscore[g] == gmax
        if taken is not None:
            hit = jnp.logical_and(hit, jnp.logical_not(taken))
        taken = hit if taken is None else jnp.logical_or(taken, hit)
        gsel.append(hit)

    def pick(rows, j):
        out = rows[(N_GROUPS - 1) * epg + j]
        for g in range(N_GROUPS - 2, -1, -1):
            out = jnp.where(gsel[g], rows[g * epg + j], out)
        return out

    x = [pick(sel, j) for j in range(epg)]
    a = [pick(aff, j) for j in range(epg)]
    base = jnp.zeros_like(x[0], dtype=jnp.int32)
    for g in range(1, N_GROUPS):
        base = jnp.where(gsel[g], g * epg, base)

    def first_max(vals, excluded):
        vmax = functools.reduce(jnp.maximum, vals)
        taken = None
        hits = []
        for j in range(epg):
            hit = vals[j] == vmax
            if excluded is not None:
                hit = jnp.logical_and(hit, jnp.logical_not(excluded[j]))
            if taken is not None:
                hit = jnp.logical_and(hit, jnp.logical_not(taken))
            taken = hit if taken is None else jnp.logical_or(taken, hit)
            hits.append(hit)
        return hits

    h1 = first_max(x, None)
    x2 = [jnp.where(h1[j], -jnp.inf, x[j]) for j in range(epg)]
    h2 = first_max(x2, h1)

    def gather(hits):
        idx = base
        val = jnp.zeros_like(a[0])
        for j in range(epg):
            idx = jnp.where(hits[j], base + j, idx)
            val = jnp.where(hits[j], a[j], val)
        return idx, val

    e0, a0 = gather(h1)
    e1, a1 = gather(h2)
    tot = a0 + a1
    return e0, e1, a0 / tot, a1 / tot


def _router_weights(router_w):
    d, e = router_w.shape
    w = router_w.astype(f32)
    w_hi = w.astype(bf16)
    w_lo = (w - w_hi.astype(f32)).astype(bf16)
    top = jnp.concatenate([w_hi, w_lo, jnp.zeros((d, LANES - 2 * e), bf16)], axis=1)
    bot = jnp.concatenate([w_hi, jnp.zeros((d, LANES - e), bf16)], axis=1)
    return jnp.concatenate([top, bot], axis=0)


def _proj_norm_router_kernel(a_ref, w_ref, h_ref, g_ref, b_ref, rw_ref, rb_ref,
                             hn_ref, hb_ref, ridx_ref, rwgt_ref, *, alpha):
    m = jnp.dot(a_ref[...], w_ref[...], preferred_element_type=f32)
    hn = _layer_norm(alpha * h_ref[...] + m, g_ref[...], b_ref[...])
    hn_ref[...] = hn
    hi = hn.astype(bf16)
    hb_ref[...] = hi
    lo = (hn - hi.astype(f32)).astype(bf16)
    d = hn.shape[1]
    parts = (jnp.dot(hi, rw_ref[0:d, :], preferred_element_type=f32)
             + jnp.dot(lo, rw_ref[d:2 * d, :], preferred_element_type=f32))
    parts = parts.T
    logits = parts[0:N_EXPERTS] + parts[N_EXPERTS:2 * N_EXPERTS]
    aff = jax.nn.sigmoid(logits)
    sel = aff + rb_ref[...]
    e0, e1, w0, w1 = _route([sel[e:e + 1, :] for e in range(N_EXPERTS)],
                            [aff[e:e + 1, :] for e in range(N_EXPERTS)])
    ridx_ref[0:1, :] = e0
    ridx_ref[1:2, :] = e1
    rwgt_ref[0:1, :] = w0
    rwgt_ref[1:2, :] = w1


def _proj_norm_router(a, w, h, ln_g, ln_b, router_ws, router_b, alpha, tm=512):
    t, k = a.shape
    d = w.shape[1]
    tm = min(tm, t)
    row = pl.BlockSpec((1, d), lambda i: (0, 0))
    return pl.pallas_call(
        functools.partial(_proj_norm_router_kernel, alpha=alpha),
        grid=(t // tm,),
        in_specs=[pl.BlockSpec((tm, k), lambda i: (i, 0)),
                  pl.BlockSpec((k, d), lambda i: (0, 0)),
                  pl.BlockSpec((tm, d), lambda i: (i, 0)),
                  row, row,
                  pl.BlockSpec((2 * d, LANES), lambda i: (0, 0)),
                  pl.BlockSpec((N_EXPERTS, 1), lambda i: (0, 0))],
        out_specs=[pl.BlockSpec((tm, d), lambda i: (i, 0)),
                   pl.BlockSpec((tm, d), lambda i: (i, 0)),
                   pl.BlockSpec((2, tm), lambda i: (0, i)),
                   pl.BlockSpec((2, tm), lambda i: (0, i))],
        out_shape=[jax.ShapeDtypeStruct((t, d), f32), jax.ShapeDtypeStruct((t, d), bf16),
                   jax.ShapeDtypeStruct((2, t), jnp.int32), jax.ShapeDtypeStruct((2, t), f32)],
        compiler_params=_params(("parallel",)),
        name="proj_norm_router",
    )(a, w, h, ln_g.reshape(1, d), ln_b.reshape(1, d), router_ws, router_b.reshape(N_EXPERTS, 1))


MOE_TILE = 512


def _experts_kernel(te_ref, tv_ref, x_ref, wgu_ref, wd_ref, o_ref, wgu_b, wd_b):
    i = pl.program_id(0)
    prev = te_ref[jnp.maximum(i - 1, 0)]

    @pl.when(jnp.logical_or(i == 0, te_ref[i] != prev))
    def _():
        wgu_b[...] = wgu_ref[0].astype(bf16)
        wd_b[...] = wd_ref[0].astype(bf16)

    @pl.when(tv_ref[i] > 0)
    def _():
        hgu = jnp.dot(x_ref[...], wgu_b[...], preferred_element_type=f32)
        a, b = hgu[:, :EXPERT_FF], hgu[:, EXPERT_FF:]
        act = (a * jax.nn.sigmoid(a) * b).astype(bf16)
        o_ref[...] = jnp.dot(act, wd_b[...], preferred_element_type=f32).astype(o_ref.dtype)

    @pl.when(tv_ref[i] == 0)
    def _():
        o_ref[...] = jnp.zeros_like(o_ref)


def _experts(xs, w_gu, w_down, layer, tile_expert, tile_valid):
    r, d = xs.shape
    ff2 = w_gu.shape[3]
    grid_spec = pltpu.PrefetchScalarGridSpec(
        num_scalar_prefetch=2,
        grid=(r // MOE_TILE,),
        in_specs=[pl.BlockSpec((MOE_TILE, d), lambda i, te, tv: (i, 0)),
                  pl.BlockSpec((None, 1, d, ff2), lambda i, te, tv: (layer, te[i], 0, 0)),
                  pl.BlockSpec((None, 1, ff2 // 2, d), lambda i, te, tv: (layer, te[i], 0, 0))],
        out_specs=pl.BlockSpec((MOE_TILE, d), lambda i, te, tv: (i, 0)),
        scratch_shapes=[pltpu.VMEM((d, ff2), bf16), pltpu.VMEM((ff2 // 2, d), bf16)],
    )
    return pl.pallas_call(
        _experts_kernel,
        grid_spec=grid_spec,
        out_shape=jax.ShapeDtypeStruct((r, d), bf16),
        compiler_params=_params(("arbitrary",)),
        name="experts",
    )(tile_expert, tile_valid, xs, w_gu, w_down)


def _dispatch(ridx):
    t = ridx.shape[1]
    slots = 2 * t
    es = ridx.T.reshape(slots)
    experts = jnp.arange(N_EXPERTS, dtype=jnp.int32)
    onehot = (es[:, None] == experts[None, :]).astype(jnp.int32)
    csum = jnp.cumsum(onehot, axis=0)
    rank = jnp.sum((csum - onehot) * onehot, axis=1)
    counts = csum[-1]
    tiles_per = (counts + MOE_TILE - 1) // MOE_TILE
    tile_end = jnp.cumsum(tiles_per)
    start = (tile_end - tiles_per) * MOE_TILE
    dest = start[es] + rank
    n_tiles = slots // MOE_TILE + N_EXPERTS
    tile_ids = jnp.arange(n_tiles, dtype=jnp.int32)
    tile_valid = (tile_ids < tile_end[-1]).astype(jnp.int32)
    tile_owner = jnp.minimum(jnp.sum((tile_ids[:, None] >= tile_end[None, :]).astype(jnp.int32), axis=1),
                             N_EXPERTS - 1)
    last_used = jnp.max(jnp.where(counts > 0, experts, 0))
    tile_expert = jnp.minimum(tile_owner, last_used)
    rows = n_tiles * MOE_TILE
    order = jnp.argsort(es, stable=True).astype(jnp.int32)
    first_sorted = jnp.cumsum(counts) - counts
    per_row = lambda table: jnp.repeat(table[tile_owner], MOE_TILE)
    row = jnp.arange(rows, dtype=jnp.int32)
    within = row - per_row(start)
    src = jnp.clip(per_row(first_sorted) + within, 0, slots - 1)
    tok = jnp.where(within < per_row(counts), order[src] // 2, row % t)
    return tok, dest, tile_expert, tile_valid


def _combine_norm_kernel(h_ref, y0_ref, y1_ref, w0_ref, w1_ref, g_ref, b_ref, o_ref, ob_ref, *, alpha):
    f = w0_ref[...] * y0_ref[...].astype(f32) + w1_ref[...] * y1_ref[...].astype(f32)
    hn = _layer_norm(alpha * h_ref[...] + f, g_ref[...], b_ref[...])
    o_ref[...] = hn
    ob_ref[...] = hn.astype(bf16)


def _combine_norm(h, y, w0, w1, ln_g, ln_b, alpha, tm=512):
    t, d = h.shape
    tm = min(tm, t)
    nt = t // tm
    blk = pl.BlockSpec((tm, d), lambda i: (i, 0))
    col = pl.BlockSpec((tm, 1), lambda i: (i, 0))
    row = pl.BlockSpec((1, d), lambda i: (0, 0))
    return pl.pallas_call(
        functools.partial(_combine_norm_kernel, alpha=alpha),
        grid=(nt,),
        in_specs=[blk, blk, pl.BlockSpec((tm, d), lambda i: (i + nt, 0)), col, col, row, row],
        out_specs=[blk, blk],
        out_shape=[jax.ShapeDtypeStruct((t, d), f32), jax.ShapeDtypeStruct((t, d), bf16)],
        compiler_params=_params(("parallel",)),
        name="combine_norm",
    )(h, y, y, w0, w1, ln_g.reshape(1, d), ln_b.reshape(1, d))


def _moe(hn, hb, ridx, rwgt, w_gu, w_down, layer, ln_g, ln_b, alpha):
    t, d = hn.shape
    tok, dest, tile_expert, tile_valid = _dispatch(ridx)
    xs = hb.at[tok].get(mode="promise_in_bounds")
    ys = _experts(xs, w_gu, w_down, layer, tile_expert, tile_valid)
    y = ys.at[dest.reshape(t, 2).T.reshape(2 * t)].get(mode="promise_in_bounds")
    return _combine_norm(hn, y, rwgt[0].reshape(t, 1), rwgt[1].reshape(t, 1), ln_g, ln_b, alpha)


DSA_Q = DSA_HEADS * DSA_HEAD_DIM
DSA_KV = DSA_KV_HEADS * DSA_HEAD_DIM
DSA_QI = IDX_HEADS * IDX_DIM
DSA_MAIN = DSA_Q + 2 * DSA_KV + DSA_QI


def _lane_iota(shape):
    return lax.broadcasted_iota(jnp.int32, shape, 1)


def _rope128(x, cos2, sin2):
    return x * cos2 + pltpu.roll(x, LANES // 2, 1) * sin2


def _rope64(x, cos4, sin4):
    lane = _lane_iota(x.shape)
    partner = jnp.where(lane % IDX_DIM < IDX_DIM // 2,
                        pltpu.roll(x, LANES - IDX_DIM // 2, 1), pltpu.roll(x, IDX_DIM // 2, 1))
    return x * cos4 + partner * sin4


def _dsa_prep_kernel(p_ref, t_ref, c2_ref, s2_ref, c4_ref, s4_ref,
                     q_ref, k_ref, v_ref, qi_ref, ki_ref):
    c2, s2, c4, s4 = c2_ref[...], s2_ref[...], c4_ref[...], s4_ref[...]
    qscale = LOG2E * DSA_HEAD_DIM ** -0.5
    for j in range(DSA_Q // LANES):
        x = p_ref[0, :, j * LANES:(j + 1) * LANES]
        q_ref[0, :, j * LANES:(j + 1) * LANES] = (_rope128(x, c2, s2) * qscale).astype(bf16)
    for j in range(DSA_KV // LANES):
        o = DSA_Q + j * LANES
        k_ref[0, :, j * LANES:(j + 1) * LANES] = _rope128(p_ref[0, :, o:o + LANES], c2, s2).astype(bf16)
        o = DSA_Q + DSA_KV + j * LANES
        v_ref[0, :, j * LANES:(j + 1) * LANES] = p_ref[0, :, o:o + LANES].astype(bf16)
    lane = _lane_iota((p_ref.shape[1], LANES))
    low = lane < IDX_DIM
    iscale = IDX_DIM ** -0.5
    for j in range(DSA_QI // LANES):
        o = DSA_Q + 2 * DSA_KV + j * LANES
        r = _rope64(p_ref[0, :, o:o + LANES], c4, s4) * iscale
        qi_ref[0, 2 * j] = jnp.where(low, r, 0.0).astype(bf16)
        qi_ref[0, 2 * j + 1] = jnp.where(low, pltpu.roll(r, LANES // 2, 1), 0.0).astype(bf16)
    ki_ref[0] = jnp.where(low, _rope64(t_ref[0], c4, s4), 0.0).astype(bf16)


def _dsa_prep(main, tail, batch, seq, ts=256):
    ts = min(ts, seq)
    c, s = _rope_tables(seq, DSA_HEAD_DIM // 2)
    c2 = jnp.asarray(np.concatenate([c, c], axis=1), f32)
    s2 = jnp.asarray(np.concatenate([-s, s], axis=1), f32)
    c, s = _rope_tables(seq, IDX_DIM // 2)
    c4 = jnp.asarray(np.concatenate([c, c, c, c], axis=1), f32)
    s4 = jnp.asarray(np.concatenate([-s, s, -s, s], axis=1), f32)
    tab = pl.BlockSpec((ts, LANES), lambda b, i: (i, 0))
    out = lambda w: pl.BlockSpec((1, ts, w), lambda b, i: (b, i, 0))
    return pl.pallas_call(
        _dsa_prep_kernel,
        grid=(batch, seq // ts),
        in_specs=[out(DSA_MAIN), out(LANES), tab, tab, tab, tab],
        out_specs=[out(DSA_Q), out(DSA_KV), out(DSA_KV),
                   pl.BlockSpec((1, IDX_HEADS, ts, LANES), lambda b, i: (b, 0, i, 0)), out(LANES)],
        out_shape=[jax.ShapeDtypeStruct((batch, seq, DSA_Q), bf16),
                   jax.ShapeDtypeStruct((batch, seq, DSA_KV), bf16),
                   jax.ShapeDtypeStruct((batch, seq, DSA_KV), bf16),
                   jax.ShapeDtypeStruct((batch, IDX_HEADS, seq, LANES), bf16),
                   jax.ShapeDtypeStruct((batch, seq, LANES), bf16)],
        compiler_params=_params(("parallel", "parallel")),
        name="dsa_prep",
    )(main, tail, c2, s2, c4, s4)


SEL_ROWS = 256
SEL_CHUNK = 512
SEL_SUB = 256


def _order_key(x):
    bits = pltpu.bitcast(x, jnp.int32)
    return bits ^ ((bits >> 31) & 0x7FFFFFFF)


def _select_kernel(qi_ref, ki_ref, t_ref, bias_ref, key_ref, wb_ref, hi_ref, lo_ref, *, n_sel, n_chunks):
    tq, kc = SEL_ROWS, SEL_CHUNK
    n_sub = kc // SEL_SUB
    qt = pl.program_id(1)
    q0 = qt * tq
    n_act = lax.div(qt, kc // tq) + 1
    qpos = q0 + lax.broadcasted_iota(jnp.int32, (tq, 1), 0)
    wscale = IDX_HEADS ** -0.5

    for h in range(IDX_HEADS):
        wb_ref[h] = jnp.broadcast_to(t_ref[0, :, IDX_DIM + h:IDX_DIM + h + 1] * wscale, (tq, LANES))
    q_all = qi_ref[0].reshape(IDX_HEADS * tq, LANES)

    def score_chunk(c, carry):
        off = pl.multiple_of(c * kc, kc)
        for s in range(kc // SEL_SUB):
            ki = ki_ref[0, pl.ds(off + s * SEL_SUB, SEL_SUB), :]
            r = lax.dot_general(q_all, ki, NT_DIMS, preferred_element_type=f32)
            acc = jnp.zeros((tq, SEL_SUB), f32)
            for h in range(IDX_HEADS):
                w = jnp.concatenate([wb_ref[h]] * (SEL_SUB // LANES), axis=1)
                acc = acc + jnp.maximum(r[h * tq:(h + 1) * tq], 0.0) * w
            kpos = off + s * SEL_SUB + _lane_iota((tq, SEL_SUB))
            sc = jnp.where(kpos <= qpos, acc + 0.0, -jnp.inf)
            key_ref[c, :, s * SEL_SUB:(s + 1) * SEL_SUB] = _order_key(sc)
            kt = _order_key(sc.T)
            hi_ref[c * n_sub + s] = (kt >> 16).astype(jnp.int16)
            lo_ref[c * n_sub + s] = ((kt & 0xFFFF) + I16_MIN).astype(jnp.int16)
        return carry

    lax.fori_loop(0, n_act, score_chunk, 0)
    n_blk = n_act * n_sub

    def count_ge16(ref, cand):
        cand = cand.astype(jnp.int16)

        def body(c, acc):
            for s in range(n_sub):
                hit = jnp.where(ref[c * n_sub + s] >= cand, jnp.int16(1), jnp.int16(0))
                for r in range(SEL_SUB // PACKED_ROWS):
                    acc = acc + hit[r * PACKED_ROWS:(r + 1) * PACKED_ROWS, :]
            return acc
        acc = lax.fori_loop(0, n_act, body, jnp.zeros((PACKED_ROWS, tq), jnp.int16))
        return jnp.sum(acc.astype(f32), axis=0, keepdims=True)

    def rank_select16(ref, rank):
        ans = jnp.where(count_ge16(ref, jnp.zeros((1, tq), jnp.int32)) >= rank, 0, I16_MIN).astype(jnp.int32)

        def bit_step(i, ans):
            cand = ans | jnp.left_shift(jnp.int32(1), 14 - i)
            return jnp.where(count_ge16(ref, cand) >= rank, cand, ans)

        return lax.fori_loop(0, 15, bit_step, ans)

    def count_gt16(ref, val):
        return jnp.where(val == I16_MAX, 0.0, count_ge16(ref, jnp.minimum(val + 1, I16_MAX)))

    kf = float(n_sel)
    top = rank_select16(hi_ref, kf)
    above = count_gt16(hi_ref, top)
    top16 = top.astype(jnp.int16)

    def low_block(j, carry):
        lo_ref[j] = jnp.where(hi_ref[j] == top16, lo_ref[j], jnp.int16(I16_MIN))
        return carry

    lax.fori_loop(0, n_blk, low_block, 0)
    bottom = rank_select16(lo_ref, kf - above)
    need_row = kf - above - count_gt16(lo_ref, bottom)

    rows = jnp.concatenate([top.astype(f32), bottom.astype(f32), need_row,
                            jnp.zeros((LANES - 3, tq), f32)], axis=0).T
    ans = rows[:, 0:1].astype(jnp.int32) * 65536 + (rows[:, 1:2].astype(jnp.int32) - I16_MIN)
    need = rows[:, 2:3]
    ans = jnp.where(qpos < n_sel, INT_MIN, ans)

    upper = (lax.broadcasted_iota(jnp.int32, (SEL_SUB, SEL_SUB), 0)
             < lax.broadcasted_iota(jnp.int32, (SEL_SUB, SEL_SUB), 1)).astype(bf16)

    def emit_chunk(c, seen):
        off = pl.multiple_of(c * kc, kc)
        for s in range(kc // SEL_SUB):
            key = key_ref[c, :, s * SEL_SUB:(s + 1) * SEL_SUB]
            eq = key == ans
            eqf = jnp.where(eq, 1.0, 0.0)
            before = seen + jnp.dot(eqf.astype(bf16), upper, preferred_element_type=f32)
            keep = jnp.logical_or(key > ans, jnp.logical_and(eq, before < need))
            kpos = off + s * SEL_SUB + _lane_iota((tq, SEL_SUB))
            keep = jnp.logical_and(keep, kpos <= qpos)
            bias_ref[0, c, :, s * SEL_SUB:(s + 1) * SEL_SUB] = jnp.where(keep, 0.0, NEG_BIG).astype(bf16)
            seen = seen + jnp.sum(eqf, axis=1, keepdims=True)
        return seen

    lax.fori_loop(0, n_act, emit_chunk, jnp.zeros((tq, 1), f32))

    def fill_chunk(c, carry):
        bias_ref[0, c] = jnp.full((tq, kc), NEG_BIG, bf16)
        return carry

    lax.fori_loop(n_act, n_chunks, fill_chunk, 0)


def _select(qi, ki, tail, batch, seq, n_sel):
    n_chunks = seq // SEL_CHUNK
    assert SEL_CHUNK >= n_sel and SEL_CHUNK % SEL_ROWS == 0
    return pl.pallas_call(
        functools.partial(_select_kernel, n_sel=n_sel, n_chunks=n_chunks),
        grid=(batch, seq // SEL_ROWS),
        in_specs=[pl.BlockSpec((1, IDX_HEADS, SEL_ROWS, LANES), lambda b, i: (b, 0, i, 0)),
                  pl.BlockSpec((1, seq, LANES), lambda b, i: (b, 0, 0)),
                  pl.BlockSpec((1, SEL_ROWS, LANES), lambda b, i: (b, i, 0))],
        out_specs=pl.BlockSpec((1, n_chunks, SEL_ROWS, SEL_CHUNK), lambda b, i: (b, 0, i, 0)),
        out_shape=jax.ShapeDtypeStruct((batch, n_chunks, seq, SEL_CHUNK), bf16),
        scratch_shapes=[pltpu.VMEM((n_chunks, SEL_ROWS, SEL_CHUNK), jnp.int32),
                        pltpu.VMEM((IDX_HEADS, SEL_ROWS, LANES), f32),
                        pltpu.VMEM((seq // SEL_SUB, SEL_SUB, SEL_ROWS), jnp.int16),
                        pltpu.VMEM((seq // SEL_SUB, SEL_SUB, SEL_ROWS), jnp.int16)],
        compiler_params=_params(("parallel", "parallel")),
        name="dsa_select",
    )(qi, ki, tail)


ATT_TILE = SEL_CHUNK
ATT_GROUP = DSA_HEADS // DSA_KV_HEADS


def _attention_kernel(qt_ref, kt_ref, q_ref, k_ref, v_ref, b_ref, o_ref, m_ref, acc_ref):
    p_id = pl.program_id(2)
    i = qt_ref[p_id]
    j = kt_ref[p_id]
    dh = DSA_HEAD_DIM
    t = ATT_TILE

    @pl.when(j == 0)
    def _():
        m_ref[...] = jnp.full(m_ref.shape, NEG_BIG, f32)
        acc_ref[...] = jnp.zeros_like(acc_ref)

    k = k_ref[0]
    v_ones = jnp.concatenate([v_ref[0], jnp.ones((t, dh), bf16)], axis=1)
    for h in range(ATT_GROUP):
        s = lax.dot_general(q_ref[0, :, h * dh:(h + 1) * dh], k, NT_DIMS, preferred_element_type=f32)
        s = s + b_ref[0, 0].astype(f32)
        m_old = m_ref[h]
        m_new = jnp.maximum(m_old, jnp.max(s, axis=1, keepdims=True))
        p = jnp.concatenate([jnp.exp2(s[:, c * LANES:(c + 1) * LANES] - m_new) for c in range(t // LANES)],
                            axis=1)
        scale = jnp.exp2(m_old - m_new)
        pv = jnp.dot(p.astype(bf16), v_ones, preferred_element_type=f32)
        acc_ref[h] = jnp.concatenate([scale, scale], axis=1) * acc_ref[h] + pv
        m_ref[h] = m_new

    @pl.when(j == i)
    def _():
        for h in range(ATT_GROUP):
            acc = acc_ref[h]
            o_ref[0, :, h * dh:(h + 1) * dh] = (acc[:, :dh] / acc[:, dh:]).astype(o_ref.dtype)


def _attention(q, k, v, bias, batch, seq):
    t = ATT_TILE
    n = seq // t
    dh = DSA_HEAD_DIM
    gw = ATT_GROUP * dh
    pairs = [(i, j) for i in range(n) for j in range(i + 1)]
    q_tile = jnp.asarray([p[0] for p in pairs], jnp.int32)
    k_tile = jnp.asarray([p[1] for p in pairs], jnp.int32)
    grid_spec = pltpu.PrefetchScalarGridSpec(
        num_scalar_prefetch=2,
        grid=(batch, DSA_KV_HEADS, len(pairs)),
        in_specs=[pl.BlockSpec((1, t, gw), lambda b, g, p, qt, kt: (b, qt[p], g)),
                  pl.BlockSpec((1, t, dh), lambda b, g, p, qt, kt: (b, kt[p], g)),
                  pl.BlockSpec((1, t, dh), lambda b, g, p, qt, kt: (b, kt[p], g)),
                  pl.BlockSpec((1, 1, t, t), lambda b, g, p, qt, kt: (b, kt[p], qt[p], 0))],
        out_specs=pl.BlockSpec((1, t, gw), lambda b, g, p, qt, kt: (b, qt[p], g)),
        scratch_shapes=[pltpu.VMEM((ATT_GROUP, t, LANES), f32), pltpu.VMEM((ATT_GROUP, t, 2 * dh), f32)],
    )
    return pl.pallas_call(
        _attention_kernel,
        grid_spec=grid_spec,
        out_shape=jax.ShapeDtypeStruct((batch, seq, DSA_Q), bf16),
        compiler_params=_params(("parallel", "parallel", "arbitrary")),
        name="dsa_attention",
    )(q_tile, k_tile, q, k, v, bias)


def _even_mixer(hb, w_in, gn_g, conv_w, conv_b, conv_ln_g, conv_ln_b, batch, seq):
    proj = _matmul(hb, w_in, f32).reshape(batch, seq, -1)
    ret = _retention(proj, gn_g, batch, seq)
    conv = _conformer_conv(proj, conv_w, conv_b, conv_ln_g, conv_ln_b, batch, seq)
    return jnp.concatenate([ret, conv], axis=-1).reshape(batch * seq, -1)


def _odd_mixer(hb, w_in, batch, seq):
    wt = w_in.T
    main = _matmul_nt(hb, wt, f32, 0, DSA_MAIN).reshape(batch, seq, DSA_MAIN)
    tail = _matmul_nt(hb, wt, f32, DSA_MAIN, LANES).reshape(batch, seq, LANES)
    q, k, v, qi, ki = _dsa_prep(main, tail, batch, seq)
    bias = _select(qi, ki, tail, batch, seq, min(TOPK_MAX, seq // 4))
    return _attention(q, k, v, bias, batch, seq).reshape(batch * seq, DSA_Q)


def kernel(x, even_w_in, even_ret_gn_g, even_conv_w, even_conv_b, even_conv_ln_g, even_conv_ln_b, even_w_out,
           odd_w_in, odd_w_out, mix_ln_g, mix_ln_b, moe_w_gu, moe_w_down, ffn_ln_g, ffn_ln_b, router_w, router_b):
    batch, seq, d = x.shape
    depth = mix_ln_g.shape[0]
    alpha = (2 * depth) ** 0.25
    router_ws = _router_weights(router_w)
    h = x.reshape(batch * seq, d).astype(f32)
    hb = h
    for layer in range(depth):
        i = layer // 2
        if layer % 2 == 0:
            mixed = _even_mixer(hb, even_w_in[i], even_ret_gn_g[i], even_conv_w[i], even_conv_b[i],
                                even_conv_ln_g[i], even_conv_ln_b[i], batch, seq)
            w_out = even_w_out[i]
        else:
            mixed = _odd_mixer(hb, odd_w_in[i], batch, seq)
            w_out = odd_w_out[i]
        hn, hnb, ridx, rwgt = _proj_norm_router(mixed, w_out.astype(bf16), h, mix_ln_g[layer], mix_ln_b[layer],
                                                router_ws, router_b.astype(f32), alpha)
        h, hb = _moe(hn, hnb, ridx, rwgt, moe_w_gu, moe_w_down, layer,
                     ffn_ln_g[layer], ffn_ln_b[layer], alpha)
    return h.reshape(batch, seq, d).astype(x.dtype)
```

```python
import functools

import numpy as np
import jax
import jax.numpy as jnp
from jax import lax
from jax.experimental import pallas as pl
from jax.experimental.pallas import tpu as pltpu

f32 = jnp.float32
bf16 = jnp.bfloat16

ROPE_THETA = 10000.0
LN_EPS = 1e-5
RET_HEADS = 4
RET_HEAD_DIM = 256
RET_WIDTH = RET_HEADS * RET_HEAD_DIM
RET_BLOCK = 256
CONV_CHANNELS = 1024
CONV_TAPS = 31
DSA_HEADS = 16
DSA_HEAD_DIM = 128
DSA_KV_HEADS = 4
IDX_HEADS = 16
IDX_DIM = 64
TOPK_MAX = 256
N_EXPERTS = 16
N_GROUPS = 4
EXPERTS_PER_GROUP = N_EXPERTS // N_GROUPS
EXPERT_FF = 512

LANES = 128
SUBLANES = 8
PACKED_ROWS = 16
VMEM_LIMIT = 56 * 1024 * 1024
NEG_BIG = -1e30
LOG2E = 1.4426950408889634
INT_MIN = -2 ** 31
I16_MIN, I16_MAX = -2 ** 15, 2 ** 15 - 1

NT_DIMS = (((1,), (1,)), ((), ()))
TN_DIMS = (((0,), (0,)), ((), ()))


def _params(sem):
    return pltpu.CompilerParams(dimension_semantics=sem, vmem_limit_bytes=VMEM_LIMIT)


def _layer_norm(z, g, b):
    mu = jnp.mean(z, axis=-1, keepdims=True)
    zc = z - mu
    var = jnp.mean(zc * zc, axis=-1, keepdims=True)
    return zc * lax.rsqrt(var + LN_EPS) * g + b


def _mm_kernel(x_ref, w_ref, o_ref, wb_ref):
    @pl.when(pl.program_id(1) == 0)
    def _():
        wb_ref[...] = w_ref[...].astype(bf16)

    o_ref[...] = jnp.dot(x_ref[...].astype(bf16), wb_ref[...], preferred_element_type=f32).astype(o_ref.dtype)


def _matmul(x, w, out_dtype, n=None, tm=1024, tn=1024):
    m, k = x.shape
    n = w.shape[1] if n is None else n
    tm, tn = min(tm, m), min(tn, n)
    assert m % tm == 0 and n % tn == 0
    return pl.pallas_call(
        _mm_kernel,
        grid=(n // tn, m // tm),
        in_specs=[pl.BlockSpec((tm, k), lambda j, i: (i, 0)),
                  pl.BlockSpec((k, tn), lambda j, i: (0, j))],
        out_specs=pl.BlockSpec((tm, tn), lambda j, i: (i, j)),
        out_shape=jax.ShapeDtypeStruct((m, n), out_dtype),
        scratch_shapes=[pltpu.VMEM((k, tn), bf16)],
        compiler_params=_params(("arbitrary", "arbitrary")),
        name="matmul",
    )(x, w)


def _mm_nt_kernel(x_ref, wt_ref, o_ref, wb_ref, *, valid):
    @pl.when(pl.program_id(1) == 0)
    def _():
        w = wt_ref[...]
        if valid < wt_ref.shape[0]:
            w = jnp.where(lax.broadcasted_iota(jnp.int32, w.shape, 0) < valid, w, 0.0)
        wb_ref[...] = w.astype(bf16)

    o_ref[...] = lax.dot_general(x_ref[...].astype(bf16), wb_ref[...], NT_DIMS,
                                 preferred_element_type=f32).astype(o_ref.dtype)


def _matmul_nt(x, wt, out_dtype, row0, n, tm=1024, tn=1024):
    m, k = x.shape
    tm, tn = min(tm, m), min(tn, n)
    assert m % tm == 0 and n % tn == 0 and row0 % tn == 0
    valid = min(tn, wt.shape[0] - row0 - (n - tn))
    assert valid == tn or n == tn
    return pl.pallas_call(
        functools.partial(_mm_nt_kernel, valid=valid),
        grid=(n // tn, m // tm),
        in_specs=[pl.BlockSpec((tm, k), lambda j, i: (i, 0)),
                  pl.BlockSpec((tn, k), lambda j, i: (row0 // tn + j, 0))],
        out_specs=pl.BlockSpec((tm, tn), lambda j, i: (i, j)),
        out_shape=jax.ShapeDtypeStruct((m, n), out_dtype),
        scratch_shapes=[pltpu.VMEM((tn, k), bf16)],
        compiler_params=_params(("arbitrary", "arbitrary")),
        name="matmul_nt",
    )(x, wt)


def _retention_tables(chunk):
    h = np.arange(RET_HEADS, dtype=np.float64)
    log_g = np.log(1.0 - 2.0 ** (-5.0 - h))
    j = np.arange(chunk, dtype=np.float64)
    diff = j[:, None] - j[None, :]
    decay_in = np.where(diff[None] >= 0, np.exp(np.maximum(diff, 0.0)[None] * log_g[:, None, None]), 0.0)
    xi = np.exp((j[None, :] + 1.0) * log_g[:, None])
    zeta = np.exp((chunk - 1.0 - j[None, :]) * log_g[:, None])
    chunk_decay = np.exp(chunk * log_g)
    xi_b = np.broadcast_to(xi[:, :, None], (RET_HEADS, chunk, RET_HEAD_DIM))
    zeta_b = np.broadcast_to(zeta[:, :, None], (RET_HEADS, chunk, RET_HEAD_DIM))
    return jnp.asarray(decay_in, f32), jnp.asarray(xi_b, f32), jnp.asarray(zeta_b, f32), chunk_decay


def _rope_tables(seq, half):
    inv = ROPE_THETA ** (-np.arange(half, dtype=np.float64) / half)
    ang = np.arange(seq, dtype=np.float64)[:, None] * inv[None, :]
    return np.cos(ang), np.sin(ang)


def _retention_kernel(q_ref, k_ref, v_ref, g_ref, cos_ref, sin_ref, din_ref, xi_ref, zeta_ref,
                      gn_ref, o_ref, state_ref, *, chunk_decay):
    @pl.when(pl.program_id(1) == 0)
    def _():
        state_ref[...] = jnp.zeros_like(state_ref)

    cos = cos_ref[...]
    sin = sin_ref[...]
    dh = RET_HEAD_DIM
    half = dh // 2

    def rope(x):
        x1, x2 = x[:, :half], x[:, half:]
        return jnp.concatenate([x1 * cos - x2 * sin, x2 * cos + x1 * sin], axis=1)

    for h in range(RET_HEADS):
        cols = slice(h * dh, (h + 1) * dh)
        q = rope(q_ref[0, :, cols].astype(f32))
        k = rope(k_ref[0, :, cols].astype(f32)) * (dh ** -0.5)
        qb = q.astype(bf16)
        kb = k.astype(bf16)
        vb = v_ref[0, :, cols].astype(bf16)
        attn = lax.dot_general(qb, kb, NT_DIMS, preferred_element_type=f32) * din_ref[h]
        inner = jnp.dot(attn.astype(bf16), vb, preferred_element_type=f32)
        state = state_ref[h]
        cross = jnp.dot(qb, state.astype(bf16), preferred_element_type=f32) * xi_ref[h]
        kz = (k * zeta_ref[h]).astype(bf16)
        state_ref[h] = state * chunk_decay[h] + lax.dot_general(kz, vb, TN_DIMS, preferred_element_type=f32)
        y = inner + cross
        mu = jnp.mean(y, axis=-1, keepdims=True)
        yc = y - mu
        var = jnp.mean(yc * yc, axis=-1, keepdims=True)
        yn = yc * lax.rsqrt(var + LN_EPS) * gn_ref[:, cols]
        g = g_ref[0, :, cols].astype(f32)
        o_ref[0, :, cols] = (g * jax.nn.sigmoid(g) * yn).astype(o_ref.dtype)


def _retention(proj, gn_g, batch, seq):
    c = min(RET_BLOCK, seq)
    dh = RET_HEAD_DIM
    w = RET_WIDTH
    decay_in, xi_b, zeta_b, chunk_decay = _retention_tables(c)
    cos, sin = _rope_tables(seq, dh // 2)
    col = lambda j: pl.BlockSpec((1, c, w), lambda b, s, j=j: (b, s, j))
    whole = lambda shape: pl.BlockSpec(shape, lambda b, s: (0,) * len(shape))
    return pl.pallas_call(
        functools.partial(_retention_kernel, chunk_decay=[float(x) for x in chunk_decay]),
        grid=(batch, seq // c),
        in_specs=[col(0), col(1), col(2), col(3),
                  pl.BlockSpec((c, dh // 2), lambda b, s: (s, 0)),
                  pl.BlockSpec((c, dh // 2), lambda b, s: (s, 0)),
                  whole((RET_HEADS, c, c)), whole((RET_HEADS, c, dh)), whole((RET_HEADS, c, dh)),
                  whole((1, w))],
        out_specs=pl.BlockSpec((1, c, w), lambda b, s: (b, s, 0)),
        out_shape=jax.ShapeDtypeStruct((batch, seq, w), bf16),
        scratch_shapes=[pltpu.VMEM((RET_HEADS, dh, dh), f32)],
        compiler_params=_params(("parallel", "arbitrary")),
        name="retention",
    )(proj, proj, proj, proj, jnp.asarray(cos, f32), jnp.asarray(sin, f32),
      decay_in, xi_b, zeta_b, gn_g.reshape(1, w))


CONV_ROWS = 128
CONV_HIST = 32
CONV_SUB = 32


def _conv_kernel(ga_ref, gb_ref, w_ref, b_ref, lg_ref, lb_ref, o_ref, ush):
    total = CONV_HIST + CONV_ROWS

    @pl.when(pl.program_id(1) == 0)
    def _():
        ush[0, 0:CONV_HIST, :] = jnp.zeros((CONV_HIST, CONV_CHANNELS), f32)

    ush[0, CONV_HIST:total, :] = ga_ref[0].astype(f32) * jax.nn.sigmoid(gb_ref[0].astype(f32))
    for r in range(1, SUBLANES):
        ush[r, 0:total - SUBLANES, :] = ush[0, r:r + total - SUBLANES, :]
    first = CONV_HIST - (CONV_TAPS - 1)
    for r0 in range(0, CONV_ROWS, CONV_SUB):
        acc = jnp.zeros((CONV_SUB, CONV_CHANNELS), f32)
        for t in range(CONV_TAPS):
            d = first + t
            lo = r0 + d - d % SUBLANES
            acc = acc + w_ref[t:t + 1, :] * ush[d % SUBLANES, lo:lo + CONV_SUB, :]
        y = _layer_norm(acc + b_ref[...], lg_ref[...], lb_ref[...])
        o_ref[0, r0:r0 + CONV_SUB, :] = (y * jax.nn.sigmoid(y)).astype(o_ref.dtype)
    ush[0, 0:CONV_HIST, :] = ush[0, CONV_ROWS:total, :]


def _conformer_conv(proj, conv_w, conv_b, ln_g, ln_b, batch, seq):
    c = CONV_CHANNELS
    a_blk = 4 * RET_WIDTH // c
    row = pl.BlockSpec((1, c), lambda b, s: (0, 0))
    return pl.pallas_call(
        _conv_kernel,
        grid=(batch, seq // CONV_ROWS),
        in_specs=[pl.BlockSpec((1, CONV_ROWS, c), lambda b, s: (b, s, a_blk)),
                  pl.BlockSpec((1, CONV_ROWS, c), lambda b, s: (b, s, a_blk + 1)),
                  pl.BlockSpec((CONV_TAPS, c), lambda b, s: (0, 0)),
                  row, row, row],
        out_specs=pl.BlockSpec((1, CONV_ROWS, c), lambda b, s: (b, s, 0)),
        out_shape=jax.ShapeDtypeStruct((batch, seq, c), bf16),
        scratch_shapes=[pltpu.VMEM((SUBLANES, CONV_HIST + CONV_ROWS, c), f32)],
        compiler_params=_params(("parallel", "arbitrary")),
        name="conformer_conv",
    )(proj, proj, conv_w, conv_b.reshape(1, c), ln_g.reshape(1, c), ln_b.reshape(1, c))


def _route(sel, aff):
    epg = EXPERTS_PER_GROUP
    gscore = []
    for g in range(N_GROUPS):
        v = sel[g * epg:(g + 1) * epg]
        best = None
        for a in range(epg):
            for b in range(a + 1, epg):
                s = v[a] + v[b]
                best = s if best is None else jnp.maximum(best, s)
        gscore.append(best)
    gmax = functools.reduce(jnp.maximum, gscore)
    taken = None
    gsel = []
    for g in range(N_GROUPS):
        hit = gscore[g] == gmax
        if taken is not None:
            hit = jnp.logical_and(hit, jnp.logical_not(taken))
        taken = hit if taken is None else jnp.logical_or(taken, hit)
        gsel.append(hit)

    def pick(rows, j):
        out = rows[(N_GROUPS - 1) * epg + j]
        for g in range(N_GROUPS - 2, -1, -1):
            out = jnp.where(gsel[g], rows[g * epg + j], out)
        return out

    x = [pick(sel, j) for j in range(epg)]
    a = [pick(aff, j) for j in range(epg)]
    base = jnp.zeros_like(x[0], dtype=jnp.int32)
    for g in range(1, N_GROUPS):
        base = jnp.where(gsel[g], g * epg, base)

    def first_max(vals, excluded):
        vmax = functools.reduce(jnp.maximum, vals)
        taken = None
        hits = []
        for j in range(epg):
            hit = vals[j] == vmax
            if excluded is not None:
                hit = jnp.logical_and(hit, jnp.logical_not(excluded[j]))
            if taken is not None:
                hit = jnp.logical_and(hit, jnp.logical_not(taken))
            taken = hit if taken is None else jnp.logical_or(taken, hit)
            hits.append(hit)
        return hits

    h1 = first_max(x, None)
    x2 = [jnp.where(h1[j], -jnp.inf, x[j]) for j in range(epg)]
    h2 = first_max(x2, h1)

    def gather(hits):
        idx = base
        val = jnp.zeros_like(a[0])
        for j in range(epg):
            idx = jnp.where(hits[j], base + j, idx)
            val = jnp.where(hits[j], a[j], val)
        return idx, val

    e0, a0 = gather(h1)
    e1, a1 = gather(h2)
    tot = a0 + a1
    return e0, e1, a0 / tot, a1 / tot


def _router_weights(router_w):
    d, e = router_w.shape
    w = router_w.astype(f32)
    w_hi = w.astype(bf16)
    w_lo = (w - w_hi.astype(f32)).astype(bf16)
    top = jnp.concatenate([w_hi, w_lo, jnp.zeros((d, LANES - 2 * e), bf16)], axis=1)
    bot = jnp.concatenate([w_hi, jnp.zeros((d, LANES - e), bf16)], axis=1)
    return jnp.concatenate([top, bot], axis=0)


def _proj_norm_router_kernel(a_ref, w_ref, h_ref, g_ref, b_ref, rw_ref, rb_ref,
                             hn_ref, hb_ref, ridx_ref, rwgt_ref, *, alpha):
    m = jnp.dot(a_ref[...], w_ref[...], preferred_element_type=f32)
    hn = _layer_norm(alpha * h_ref[...] + m, g_ref[...], b_ref[...])
    hn_ref[...] = hn
    hi = hn.astype(bf16)
    hb_ref[...] = hi
    lo = (hn - hi.astype(f32)).astype(bf16)
    d = hn.shape[1]
    parts = (jnp.dot(hi, rw_ref[0:d, :], preferred_element_type=f32)
             + jnp.dot(lo, rw_ref[d:2 * d, :], preferred_element_type=f32))
    parts = parts.T
    logits = parts[0:N_EXPERTS] + parts[N_EXPERTS:2 * N_EXPERTS]
    aff = jax.nn.sigmoid(logits)
    sel = aff + rb_ref[...]
    e0, e1, w0, w1 = _route([sel[e:e + 1, :] for e in range(N_EXPERTS)],
                            [aff[e:e + 1, :] for e in range(N_EXPERTS)])
    ridx_ref[0:1, :] = e0
    ridx_ref[1:2, :] = e1
    rwgt_ref[0:1, :] = w0
    rwgt_ref[1:2, :] = w1


def _proj_norm_router(a, w, h, ln_g, ln_b, router_ws, router_b, alpha, tm=512):
    t, k = a.shape
    d = w.shape[1]
    tm = min(tm, t)
    row = pl.BlockSpec((1, d), lambda i: (0, 0))
    return pl.pallas_call(
        functools.partial(_proj_norm_router_kernel, alpha=alpha),
        grid=(t // tm,),
        in_specs=[pl.BlockSpec((tm, k), lambda i: (i, 0)),
                  pl.BlockSpec((k, d), lambda i: (0, 0)),
                  pl.BlockSpec((tm, d), lambda i: (i, 0)),
                  row, row,
                  pl.BlockSpec((2 * d, LANES), lambda i: (0, 0)),
                  pl.BlockSpec((N_EXPERTS, 1), lambda i: (0, 0))],
        out_specs=[pl.BlockSpec((tm, d), lambda i: (i, 0)),
                   pl.BlockSpec((tm, d), lambda i: (i, 0)),
                   pl.BlockSpec((2, tm), lambda i: (0, i)),
                   pl.BlockSpec((2, tm), lambda i: (0, i))],
        out_shape=[jax.ShapeDtypeStruct((t, d), f32), jax.ShapeDtypeStruct((t, d), bf16),
                   jax.ShapeDtypeStruct((2, t), jnp.int32), jax.ShapeDtypeStruct((2, t), f32)],
        compiler_params=_params(("parallel",)),
        name="proj_norm_router",
    )(a, w, h, ln_g.reshape(1, d), ln_b.reshape(1, d), router_ws, router_b.reshape(N_EXPERTS, 1))


MOE_TILE = 512


def _experts_kernel(te_ref, tv_ref, x_ref, wgu_ref, wd_ref, o_ref, wgu_b, wd_b):
    i = pl.program_id(0)
    prev = te_ref[jnp.maximum(i - 1, 0)]

    @pl.when(jnp.logical_or(i == 0, te_ref[i] != prev))
    def _():
        wgu_b[...] = wgu_ref[0].astype(bf16)
        wd_b[...] = wd_ref[0].astype(bf16)

    @pl.when(tv_ref[i] > 0)
    def _():
        hgu = jnp.dot(x_ref[...], wgu_b[...], preferred_element_type=f32)
        a, b = hgu[:, :EXPERT_FF], hgu[:, EXPERT_FF:]
        act = (a * jax.nn.sigmoid(a) * b).astype(bf16)
        o_ref[...] = jnp.dot(act, wd_b[...], preferred_element_type=f32).astype(o_ref.dtype)

    @pl.when(tv_ref[i] == 0)
    def _():
        o_ref[...] = jnp.zeros_like(o_ref)


def _experts(xs, w_gu, w_down, layer, tile_expert, tile_valid):
    r, d = xs.shape
    ff2 = w_gu.shape[3]
    grid_spec = pltpu.PrefetchScalarGridSpec(
        num_scalar_prefetch=2,
        grid=(r // MOE_TILE,),
        in_specs=[pl.BlockSpec((MOE_TILE, d), lambda i, te, tv: (i, 0)),
                  pl.BlockSpec((None, 1, d, ff2), lambda i, te, tv: (layer, te[i], 0, 0)),
                  pl.BlockSpec((None, 1, ff2 // 2, d), lambda i, te, tv: (layer, te[i], 0, 0))],
        out_specs=pl.BlockSpec((MOE_TILE, d), lambda i, te, tv: (i, 0)),
        scratch_shapes=[pltpu.VMEM((d, ff2), bf16), pltpu.VMEM((ff2 // 2, d), bf16)],
    )
    return pl.pallas_call(
        _experts_kernel,
        grid_spec=grid_spec,
        out_shape=jax.ShapeDtypeStruct((r, d), bf16),
        compiler_params=_params(("arbitrary",)),
        name="experts",
    )(tile_expert, tile_valid, xs, w_gu, w_down)


def _dispatch(ridx):
    t = ridx.shape[1]
    slots = 2 * t
    es = ridx.T.reshape(slots)
    experts = jnp.arange(N_EXPERTS, dtype=jnp.int32)
    onehot = (es[:, None] == experts[None, :]).astype(jnp.int32)
    csum = jnp.cumsum(onehot, axis=0)
    rank = jnp.sum((csum - onehot) * onehot, axis=1)
    counts = csum[-1]
    tiles_per = (counts + MOE_TILE - 1) // MOE_TILE
    tile_end = jnp.cumsum(tiles_per)
    start = (tile_end - tiles_per) * MOE_TILE
    dest = start[es] + rank
    n_tiles = slots // MOE_TILE + N_EXPERTS
    tile_ids = jnp.arange(n_tiles, dtype=jnp.int32)
    tile_valid = (tile_ids < tile_end[-1]).astype(jnp.int32)
    tile_owner = jnp.minimum(jnp.sum((tile_ids[:, None] >= tile_end[None, :]).astype(jnp.int32), axis=1),
                             N_EXPERTS - 1)
    last_used = jnp.max(jnp.where(counts > 0, experts, 0))
    tile_expert = jnp.minimum(tile_owner, last_used)
    rows = n_tiles * MOE_TILE
    order = jnp.argsort(es, stable=True).astype(jnp.int32)
    first_sorted = jnp.cumsum(counts) - counts
    per_row = lambda table: jnp.repeat(table[tile_owner], MOE_TILE)
    row = jnp.arange(rows, dtype=jnp.int32)
    within = row - per_row(start)
    src = jnp.clip(per_row(first_sorted) + within, 0, slots - 1)
    tok = jnp.where(within < per_row(counts), order[src] // 2, row % t)
    return tok, dest, tile_expert, tile_valid


def _combine_norm_kernel(h_ref, y0_ref, y1_ref, w0_ref, w1_ref, g_ref, b_ref, o_ref, ob_ref, *, alpha):
    f = w0_ref[...] * y0_ref[...].astype(f32) + w1_ref[...] * y1_ref[...].astype(f32)
    hn = _layer_norm(alpha * h_ref[...] + f, g_ref[...], b_ref[...])
    o_ref[...] = hn
    ob_ref[...] = hn.astype(bf16)


def _combine_norm(h, y, w0, w1, ln_g, ln_b, alpha, tm=512):
    t, d = h.shape
    tm = min(tm, t)
    nt = t // tm
    blk = pl.BlockSpec((tm, d), lambda i: (i, 0))
    col = pl.BlockSpec((tm, 1), lambda i: (i, 0))
    row = pl.BlockSpec((1, d), lambda i: (0, 0))
    return pl.pallas_call(
        functools.partial(_combine_norm_kernel, alpha=alpha),
        grid=(nt,),
        in_specs=[blk, blk, pl.BlockSpec((tm, d), lambda i: (i + nt, 0)), col, col, row, row],
        out_specs=[blk, blk],
        out_shape=[jax.ShapeDtypeStruct((t, d), f32), jax.ShapeDtypeStruct((t, d), bf16)],
        compiler_params=_params(("parallel",)),
        name="combine_norm",
    )(h, y, y, w0, w1, ln_g.reshape(1, d), ln_b.reshape(1, d))


def _moe(hn, hb, ridx, rwgt, w_gu, w_down, layer, ln_g, ln_b, alpha):
    t, d = hn.shape
    tok, dest, tile_expert, tile_valid = _dispatch(ridx)
    xs = hb.at[tok].get(mode="promise_in_bounds")
    ys = _experts(xs, w_gu, w_down, layer, tile_expert, tile_valid)
    y = ys.at[dest.reshape(t, 2).T.reshape(2 * t)].get(mode="promise_in_bounds")
    return _combine_norm(hn, y, rwgt[0].reshape(t, 1), rwgt[1].reshape(t, 1), ln_g, ln_b, alpha)


DSA_Q = DSA_HEADS * DSA_HEAD_DIM
DSA_KV = DSA_KV_HEADS * DSA_HEAD_DIM
DSA_QI = IDX_HEADS * IDX_DIM
DSA_MAIN = DSA_Q + 2 * DSA_KV + DSA_QI


def _lane_iota(shape):
    return lax.broadcasted_iota(jnp.int32, shape, 1)


def _rope128(x, cos2, sin2):
    return x * cos2 + pltpu.roll(x, LANES // 2, 1) * sin2


def _rope64(x, cos4, sin4):
    lane = _lane_iota(x.shape)
    partner = jnp.where(lane % IDX_DIM < IDX_DIM // 2,
                        pltpu.roll(x, LANES - IDX_DIM // 2, 1), pltpu.roll(x, IDX_DIM // 2, 1))
    return x * cos4 + partner * sin4


def _dsa_prep_kernel(p_ref, t_ref, c2_ref, s2_ref, c4_ref, s4_ref,
                     q_ref, k_ref, v_ref, qi_ref, ki_ref):
    c2, s2, c4, s4 = c2_ref[...], s2_ref[...], c4_ref[...], s4_ref[...]
    qscale = LOG2E * DSA_HEAD_DIM ** -0.5
    for j in range(DSA_Q // LANES):
        x = p_ref[0, :, j * LANES:(j + 1) * LANES].astype(f32)
        q_ref[0, :, j * LANES:(j + 1) * LANES] = (_rope128(x, c2, s2) * qscale).astype(bf16)
    for j in range(DSA_KV // LANES):
        o = DSA_Q + j * LANES
        k_ref[0, :, j * LANES:(j + 1) * LANES] = _rope128(p_ref[0, :, o:o + LANES].astype(f32), c2, s2).astype(bf16)
        o = DSA_Q + DSA_KV + j * LANES
        v_ref[0, :, j * LANES:(j + 1) * LANES] = p_ref[0, :, o:o + LANES].astype(bf16)
    lane = _lane_iota((p_ref.shape[1], LANES))
    low = lane < IDX_DIM
    iscale = IDX_DIM ** -0.5
    for j in range(DSA_QI // LANES):
        o = DSA_Q + 2 * DSA_KV + j * LANES
        r = _rope64(p_ref[0, :, o:o + LANES].astype(f32), c4, s4) * iscale
        qi_ref[0, 2 * j] = jnp.where(low, r, 0.0).astype(bf16)
        qi_ref[0, 2 * j + 1] = jnp.where(low, pltpu.roll(r, LANES // 2, 1), 0.0).astype(bf16)
    ki_ref[0] = jnp.where(low, _rope64(t_ref[0], c4, s4), 0.0).astype(bf16)


def _dsa_prep(main, tail, batch, seq, ts=256):
    ts = min(ts, seq)
    c, s = _rope_tables(seq, DSA_HEAD_DIM // 2)
    c2 = jnp.asarray(np.concatenate([c, c], axis=1), f32)
    s2 = jnp.asarray(np.concatenate([-s, s], axis=1), f32)
    c, s = _rope_tables(seq, IDX_DIM // 2)
    c4 = jnp.asarray(np.concatenate([c, c, c, c], axis=1), f32)
    s4 = jnp.asarray(np.concatenate([-s, s, -s, s], axis=1), f32)
    tab = pl.BlockSpec((ts, LANES), lambda b, i: (i, 0))
    out = lambda w: pl.BlockSpec((1, ts, w), lambda b, i: (b, i, 0))
    return pl.pallas_call(
        _dsa_prep_kernel,
        grid=(batch, seq // ts),
        in_specs=[out(DSA_MAIN), out(LANES), tab, tab, tab, tab],
        out_specs=[out(DSA_Q), out(DSA_KV), out(DSA_KV),
                   pl.BlockSpec((1, IDX_HEADS, ts, LANES), lambda b, i: (b, 0, i, 0)), out(LANES)],
        out_shape=[jax.ShapeDtypeStruct((batch, seq, DSA_Q), bf16),
                   jax.ShapeDtypeStruct((batch, seq, DSA_KV), bf16),
                   jax.ShapeDtypeStruct((batch, seq, DSA_KV), bf16),
                   jax.ShapeDtypeStruct((batch, IDX_HEADS, seq, LANES), bf16),
                   jax.ShapeDtypeStruct((batch, seq, LANES), bf16)],
        compiler_params=_params(("parallel", "parallel")),
        name="dsa_prep",
    )(main, tail, c2, s2, c4, s4)


SEL_ROWS = 256
SEL_CHUNK = 512
SEL_SUB = 256


def _order_key(x):
    bits = pltpu.bitcast(x, jnp.int32)
    return bits ^ ((bits >> 31) & 0x7FFFFFFF)


def _select_kernel(qi_ref, ki_ref, t_ref, bias_ref, key_ref, wb_ref, hi_ref, lo_ref, *, n_sel, n_chunks):
    tq, kc = SEL_ROWS, SEL_CHUNK
    n_sub = kc // SEL_SUB
    qt = pl.program_id(1)
    q0 = qt * tq
    n_act = lax.div(qt, kc // tq) + 1
    qpos = q0 + lax.broadcasted_iota(jnp.int32, (tq, 1), 0)
    wscale = IDX_HEADS ** -0.5

    for h in range(IDX_HEADS):
        wb_ref[h] = jnp.broadcast_to(t_ref[0, :, IDX_DIM + h:IDX_DIM + h + 1] * wscale, (tq, LANES))
    q_all = qi_ref[0].reshape(IDX_HEADS * tq, LANES)

    def score_chunk(c, carry):
        off = pl.multiple_of(c * kc, kc)
        for s in range(kc // SEL_SUB):
            ki = ki_ref[0, pl.ds(off + s * SEL_SUB, SEL_SUB), :]
            r = lax.dot_general(q_all, ki, NT_DIMS, preferred_element_type=f32)
            acc = jnp.zeros((tq, SEL_SUB), f32)
            for h in range(IDX_HEADS):
                w = jnp.concatenate([wb_ref[h]] * (SEL_SUB // LANES), axis=1)
                acc = acc + jnp.maximum(r[h * tq:(h + 1) * tq], 0.0) * w
            kpos = off + s * SEL_SUB + _lane_iota((tq, SEL_SUB))
            sc = jnp.where(kpos <= qpos, acc + 0.0, -jnp.inf)
            key_ref[c, :, s * SEL_SUB:(s + 1) * SEL_SUB] = _order_key(sc)
            kt = _order_key(sc.T)
            hi_ref[c * n_sub + s] = (kt >> 16).astype(jnp.int16)
            lo_ref[c * n_sub + s] = ((kt & 0xFFFF) + I16_MIN).astype(jnp.int16)
        return carry

    lax.fori_loop(0, n_act, score_chunk, 0)
    n_blk = n_act * n_sub

    def count_ge16(ref, cand):
        cand = cand.astype(jnp.int16)

        def body(c, acc):
            for s in range(n_sub):
                hit = jnp.where(ref[c * n_sub + s] >= cand, jnp.int16(1), jnp.int16(0))
                for r in range(SEL_SUB // PACKED_ROWS):
                    acc = acc + hit[r * PACKED_ROWS:(r + 1) * PACKED_ROWS, :]
            return acc
        acc = lax.fori_loop(0, n_act, body, jnp.zeros((PACKED_ROWS, tq), jnp.int16))
        return jnp.sum(acc.astype(f32), axis=0, keepdims=True)

    def rank_select16(ref, rank):
        ans = jnp.where(count_ge16(ref, jnp.zeros((1, tq), jnp.int32)) >= rank, 0, I16_MIN).astype(jnp.int32)

        def bit_step(i, ans):
            cand = ans | jnp.left_shift(jnp.int32(1), 14 - i)
            return jnp.where(count_ge16(ref, cand) >= rank, cand, ans)

        return lax.fori_loop(0, 15, bit_step, ans)

    def count_gt16(ref, val):
        return jnp.where(val == I16_MAX, 0.0, count_ge16(ref, jnp.minimum(val + 1, I16_MAX)))

    kf = float(n_sel)
    top = rank_select16(hi_ref, kf)
    above = count_gt16(hi_ref, top)
    top16 = top.astype(jnp.int16)

    def low_block(j, carry):
        lo_ref[j] = jnp.where(hi_ref[j] == top16, lo_ref[j], jnp.int16(I16_MIN))
        return carry

    lax.fori_loop(0, n_blk, low_block, 0)
    bottom = rank_select16(lo_ref, kf - above)
    need_row = kf - above - count_gt16(lo_ref, bottom)

    rows = jnp.concatenate([top.astype(f32), bottom.astype(f32), need_row,
                            jnp.zeros((LANES - 3, tq), f32)], axis=0).T
    ans = rows[:, 0:1].astype(jnp.int32) * 65536 + (rows[:, 1:2].astype(jnp.int32) - I16_MIN)
    need = rows[:, 2:3]
    ans = jnp.where(qpos < n_sel, INT_MIN, ans)

    upper = (lax.broadcasted_iota(jnp.int32, (SEL_SUB, SEL_SUB), 0)
             < lax.broadcasted_iota(jnp.int32, (SEL_SUB, SEL_SUB), 1)).astype(bf16)

    def emit_chunk(c, seen):
        off = pl.multiple_of(c * kc, kc)
        for s in range(kc // SEL_SUB):
            key = key_ref[c, :, s * SEL_SUB:(s + 1) * SEL_SUB]
            eq = key == ans
            eqf = jnp.where(eq, 1.0, 0.0)
            before = seen + jnp.dot(eqf.astype(bf16), upper, preferred_element_type=f32)
            keep = jnp.logical_or(key > ans, jnp.logical_and(eq, before < need))
            kpos = off + s * SEL_SUB + _lane_iota((tq, SEL_SUB))
            keep = jnp.logical_and(keep, kpos <= qpos)
            bias_ref[0, c, :, s * SEL_SUB:(s + 1) * SEL_SUB] = jnp.where(keep, 0.0, NEG_BIG).astype(bf16)
            seen = seen + jnp.sum(eqf, axis=1, keepdims=True)
        return seen

    lax.fori_loop(0, n_act, emit_chunk, jnp.zeros((tq, 1), f32))

    def fill_chunk(c, carry):
        bias_ref[0, c] = jnp.full((tq, kc), NEG_BIG, bf16)
        return carry

    lax.fori_loop(n_act, n_chunks, fill_chunk, 0)


def _select(qi, ki, tail, batch, seq, n_sel):
    n_chunks = seq // SEL_CHUNK
    assert SEL_CHUNK >= n_sel and SEL_CHUNK % SEL_ROWS == 0
    assert seq <= I16_MAX
    return pl.pallas_call(
        functools.partial(_select_kernel, n_sel=n_sel, n_chunks=n_chunks),
        grid=(batch, seq // SEL_ROWS),
        in_specs=[pl.BlockSpec((1, IDX_HEADS, SEL_ROWS, LANES), lambda b, i: (b, 0, i, 0)),
                  pl.BlockSpec((1, seq, LANES), lambda b, i: (b, 0, 0)),
                  pl.BlockSpec((1, SEL_ROWS, LANES), lambda b, i: (b, i, 0))],
        out_specs=pl.BlockSpec((1, n_chunks, SEL_ROWS, SEL_CHUNK), lambda b, i: (b, 0, i, 0)),
        out_shape=jax.ShapeDtypeStruct((batch, n_chunks, seq, SEL_CHUNK), bf16),
        scratch_shapes=[pltpu.VMEM((n_chunks, SEL_ROWS, SEL_CHUNK), jnp.int32),
                        pltpu.VMEM((IDX_HEADS, SEL_ROWS, LANES), f32),
                        pltpu.VMEM((seq // SEL_SUB, SEL_SUB, SEL_ROWS), jnp.int16),
                        pltpu.VMEM((seq // SEL_SUB, SEL_SUB, SEL_ROWS), jnp.int16)],
        compiler_params=_params(("parallel", "parallel")),
        name="dsa_select",
    )(qi, ki, tail)


ATT_TILE = SEL_CHUNK
ATT_GROUP = DSA_HEADS // DSA_KV_HEADS


def _attention_kernel(qt_ref, kt_ref, q_ref, k_ref, v_ref, b_ref, o_ref, m_ref, acc_ref):
    p_id = pl.program_id(2)
    i = qt_ref[p_id]
    j = kt_ref[p_id]
    dh = DSA_HEAD_DIM
    t = ATT_TILE

    @pl.when(j == 0)
    def _():
        m_ref[...] = jnp.full(m_ref.shape, NEG_BIG, f32)
        acc_ref[...] = jnp.zeros_like(acc_ref)

    k = k_ref[0]
    v_ones = jnp.concatenate([v_ref[0], jnp.ones((t, dh), bf16)], axis=1)
    for h in range(ATT_GROUP):
        s = lax.dot_general(q_ref[0, :, h * dh:(h + 1) * dh], k, NT_DIMS, preferred_element_type=f32)
        s = s + b_ref[0, 0].astype(f32)
        m_old = m_ref[h]
        m_new = jnp.maximum(m_old, jnp.max(s, axis=1, keepdims=True))
        p = jnp.concatenate([jnp.exp2(s[:, c * LANES:(c + 1) * LANES] - m_new) for c in range(t // LANES)],
                            axis=1)
        scale = jnp.exp2(m_old - m_new)
        pv = jnp.dot(p.astype(bf16), v_ones, preferred_element_type=f32)
        acc_ref[h] = jnp.concatenate([scale, scale], axis=1) * acc_ref[h] + pv
        m_ref[h] = m_new

    @pl.when(j == i)
    def _():
        for h in range(ATT_GROUP):
            acc = acc_ref[h]
            o_ref[0, :, h * dh:(h + 1) * dh] = (acc[:, :dh] / acc[:, dh:]).astype(o_ref.dtype)


def _attention(q, k, v, bias, batch, seq):
    t = ATT_TILE
    n = seq // t
    dh = DSA_HEAD_DIM
    gw = ATT_GROUP * dh
    pairs = [(i, j) for i in range(n) for j in range(i + 1)]
    q_tile = jnp.asarray([p[0] for p in pairs], jnp.int32)
    k_tile = jnp.asarray([p[1] for p in pairs], jnp.int32)
    grid_spec = pltpu.PrefetchScalarGridSpec(
        num_scalar_prefetch=2,
        grid=(batch, DSA_KV_HEADS, len(pairs)),
        in_specs=[pl.BlockSpec((1, t, gw), lambda b, g, p, qt, kt: (b, qt[p], g)),
                  pl.BlockSpec((1, t, dh), lambda b, g, p, qt, kt: (b, kt[p], g)),
                  pl.BlockSpec((1, t, dh), lambda b, g, p, qt, kt: (b, kt[p], g)),
                  pl.BlockSpec((1, 1, t, t), lambda b, g, p, qt, kt: (b, kt[p], qt[p], 0))],
        out_specs=pl.BlockSpec((1, t, gw), lambda b, g, p, qt, kt: (b, qt[p], g)),
        scratch_shapes=[pltpu.VMEM((ATT_GROUP, t, LANES), f32), pltpu.VMEM((ATT_GROUP, t, 2 * dh), f32)],
    )
    return pl.pallas_call(
        _attention_kernel,
        grid_spec=grid_spec,
        out_shape=jax.ShapeDtypeStruct((batch, seq, DSA_Q), bf16),
        compiler_params=_params(("parallel", "parallel", "arbitrary")),
        name="dsa_attention",
    )(q_tile, k_tile, q, k, v, bias)


def _even_mixer(hb, w_in, gn_g, conv_w, conv_b, conv_ln_g, conv_ln_b, batch, seq):
    proj = _matmul(hb, w_in, bf16).reshape(batch, seq, -1)
    ret = _retention(proj, gn_g, batch, seq)
    conv = _conformer_conv(proj, conv_w, conv_b, conv_ln_g, conv_ln_b, batch, seq)
    return jnp.concatenate([ret, conv], axis=-1).reshape(batch * seq, -1)


def _odd_mixer(hb, w_in, batch, seq):
    wt = w_in.T
    main = _matmul_nt(hb, wt, bf16, 0, DSA_MAIN).reshape(batch, seq, DSA_MAIN)
    tail = _matmul_nt(hb, wt, f32, DSA_MAIN, LANES).reshape(batch, seq, LANES)
    q, k, v, qi, ki = _dsa_prep(main, tail, batch, seq)
    bias = _select(qi, ki, tail, batch, seq, min(TOPK_MAX, seq // 4))
    return _attention(q, k, v, bias, batch, seq).reshape(batch * seq, DSA_Q)


def kernel(x, even_w_in, even_ret_gn_g, even_conv_w, even_conv_b, even_conv_ln_g, even_conv_ln_b, even_w_out,
           odd_w_in, odd_w_out, mix_ln_g, mix_ln_b, moe_w_gu, moe_w_down, ffn_ln_g, ffn_ln_b, router_w, router_b):
    batch, seq, d = x.shape
    depth = mix_ln_g.shape[0]
    alpha = (2 * depth) ** 0.25
    router_ws = _router_weights(router_w)
    h = x.reshape(batch * seq, d).astype(f32)
    hb = h
    for layer in range(depth):
        i = layer // 2
        if layer % 2 == 0:
            mixed = _even_mixer(hb, even_w_in[i], even_ret_gn_g[i], even_conv_w[i], even_conv_b[i],
                                even_conv_ln_g[i], even_conv_ln_b[i], batch, seq)
            w_out = even_w_out[i]
        else:
            mixed = _odd_mixer(hb, odd_w_in[i], batch, seq)
            w_out = odd_w_out[i]
        hn, hnb, ridx, rwgt = _proj_norm_router(mixed, w_out.astype(bf16), h, mix_ln_g[layer], mix_ln_b[layer],
                                                router_ws, router_b.astype(f32), alpha)
        h, hb = _moe(hn, hnb, ridx, rwgt, moe_w_gu, moe_w_down, layer,
                     ffn_ln_g[layer], ffn_ln_b[layer], alpha)
    return h.reshape(batch, seq, d).astype(x.dtype)
```

```python
import functools

import numpy as np
import jax
import jax.numpy as jnp
from jax import lax
from jax.experimental import pallas as pl
from jax.experimental.pallas import tpu as pltpu

f32 = jnp.float32
bf16 = jnp.bfloat16

ROPE_THETA = 10000.0
LN_EPS = 1e-5
RET_HEADS = 4
RET_HEAD_DIM = 256
RET_WIDTH = RET_HEADS * RET_HEAD_DIM
RET_BLOCK = 256
CONV_CHANNELS = 1024
CONV_TAPS = 31
DSA_HEADS = 16
DSA_HEAD_DIM = 128
DSA_KV_HEADS = 4
IDX_HEADS = 16
IDX_DIM = 64
TOPK_MAX = 256
N_EXPERTS = 16
N_GROUPS = 4
EXPERTS_PER_GROUP = N_EXPERTS // N_GROUPS
EXPERT_FF = 512

LANES = 128
SUBLANES = 8
PACKED_ROWS = 16
VMEM_LIMIT = 56 * 1024 * 1024
NEG_BIG = -1e30
LOG2E = 1.4426950408889634
INT_MIN = -2 ** 31
I16_MIN, I16_MAX = -2 ** 15, 2 ** 15 - 1

NT_DIMS = (((1,), (1,)), ((), ()))
TN_DIMS = (((0,), (0,)), ((), ()))


def _params(sem):
    return pltpu.CompilerParams(dimension_semantics=sem, vmem_limit_bytes=VMEM_LIMIT)


def _layer_norm(z, g, b):
    mu = jnp.mean(z, axis=-1, keepdims=True)
    zc = z - mu
    var = jnp.mean(zc * zc, axis=-1, keepdims=True)
    return zc * lax.rsqrt(var + LN_EPS) * g + b


def _mm_kernel(x_ref, w_ref, o_ref, wb_ref):
    @pl.when(pl.program_id(1) == 0)
    def _():
        wb_ref[...] = w_ref[...].astype(bf16)

    o_ref[...] = jnp.dot(x_ref[...].astype(bf16), wb_ref[...], preferred_element_type=f32).astype(o_ref.dtype)


def _matmul(x, w, out_dtype, n=None, tm=1024, tn=1024):
    m, k = x.shape
    n = w.shape[1] if n is None else n
    tm, tn = min(tm, m), min(tn, n)
    assert m % tm == 0 and n % tn == 0
    return pl.pallas_call(
        _mm_kernel,
        grid=(n // tn, m // tm),
        in_specs=[pl.BlockSpec((tm, k), lambda j, i: (i, 0)),
                  pl.BlockSpec((k, tn), lambda j, i: (0, j))],
        out_specs=pl.BlockSpec((tm, tn), lambda j, i: (i, j)),
        out_shape=jax.ShapeDtypeStruct((m, n), out_dtype),
        scratch_shapes=[pltpu.VMEM((k, tn), bf16)],
        compiler_params=_params(("arbitrary", "arbitrary")),
        name="matmul",
    )(x, w)


def _mm_nt_kernel(x_ref, wt_ref, o_ref, wb_ref, *, valid):
    @pl.when(pl.program_id(1) == 0)
    def _():
        w = wt_ref[...]
        if valid < wt_ref.shape[0]:
            w = jnp.where(lax.broadcasted_iota(jnp.int32, w.shape, 0) < valid, w, 0.0)
        wb_ref[...] = w.astype(bf16)

    o_ref[...] = lax.dot_general(x_ref[...].astype(bf16), wb_ref[...], NT_DIMS,
                                 preferred_element_type=f32).astype(o_ref.dtype)


def _matmul_nt(x, wt, out_dtype, row0, n, tm=1024, tn=1024):
    m, k = x.shape
    tm, tn = min(tm, m), min(tn, n)
    assert m % tm == 0 and n % tn == 0 and row0 % tn == 0
    valid = min(tn, wt.shape[0] - row0 - (n - tn))
    assert valid == tn or n == tn
    return pl.pallas_call(
        functools.partial(_mm_nt_kernel, valid=valid),
        grid=(n // tn, m // tm),
        in_specs=[pl.BlockSpec((tm, k), lambda j, i: (i, 0)),
                  pl.BlockSpec((tn, k), lambda j, i: (row0 // tn + j, 0))],
        out_specs=pl.BlockSpec((tm, tn), lambda j, i: (i, j)),
        out_shape=jax.ShapeDtypeStruct((m, n), out_dtype),
        scratch_shapes=[pltpu.VMEM((tn, k), bf16)],
        compiler_params=_params(("arbitrary", "arbitrary")),
        name="matmul_nt",
    )(x, wt)


def _retention_tables(chunk):
    h = np.arange(RET_HEADS, dtype=np.float64)
    log_g = np.log(1.0 - 2.0 ** (-5.0 - h))
    j = np.arange(chunk, dtype=np.float64)
    diff = j[:, None] - j[None, :]
    decay_in = np.where(diff[None] >= 0, np.exp(np.maximum(diff, 0.0)[None] * log_g[:, None, None]), 0.0)
    xi = np.exp((j[None, :] + 1.0) * log_g[:, None])
    zeta = np.exp((chunk - 1.0 - j[None, :]) * log_g[:, None])
    chunk_decay = np.exp(chunk * log_g)
    xi_b = np.broadcast_to(xi[:, :, None], (RET_HEADS, chunk, RET_HEAD_DIM))
    zeta_b = np.broadcast_to(zeta[:, :, None], (RET_HEADS, chunk, RET_HEAD_DIM))
    return jnp.asarray(decay_in, f32), jnp.asarray(xi_b, f32), jnp.asarray(zeta_b, f32), chunk_decay


def _rope_tables(seq, half):
    inv = ROPE_THETA ** (-np.arange(half, dtype=np.float64) / half)
    ang = np.arange(seq, dtype=np.float64)[:, None] * inv[None, :]
    return np.cos(ang), np.sin(ang)


def _retention_kernel(q_ref, k_ref, v_ref, g_ref, cos_ref, sin_ref, din_ref, xi_ref, zeta_ref,
                      gn_ref, o_ref, state_ref, *, chunk_decay):
    @pl.when(pl.program_id(1) == 0)
    def _():
        state_ref[...] = jnp.zeros_like(state_ref)

    cos = cos_ref[...]
    sin = sin_ref[...]
    dh = RET_HEAD_DIM
    half = dh // 2

    def rope(x):
        x1, x2 = x[:, :half], x[:, half:]
        return jnp.concatenate([x1 * cos - x2 * sin, x2 * cos + x1 * sin], axis=1)

    for h in range(RET_HEADS):
        cols = slice(h * dh, (h + 1) * dh)
        q = rope(q_ref[0, :, cols].astype(f32))
        k = rope(k_ref[0, :, cols].astype(f32)) * (dh ** -0.5)
        qb = q.astype(bf16)
        kb = k.astype(bf16)
        vb = v_ref[0, :, cols].astype(bf16)
        attn = lax.dot_general(qb, kb, NT_DIMS, preferred_element_type=f32) * din_ref[h]
        inner = jnp.dot(attn.astype(bf16), vb, preferred_element_type=f32)
        state = state_ref[h]
        cross = jnp.dot(qb, state.astype(bf16), preferred_element_type=f32) * xi_ref[h]
        kz = (k * zeta_ref[h]).astype(bf16)
        state_ref[h] = state * chunk_decay[h] + lax.dot_general(kz, vb, TN_DIMS, preferred_element_type=f32)
        y = inner + cross
        mu = jnp.mean(y, axis=-1, keepdims=True)
        yc = y - mu
        var = jnp.mean(yc * yc, axis=-1, keepdims=True)
        yn = yc * lax.rsqrt(var + LN_EPS) * gn_ref[:, cols]
        g = g_ref[0, :, cols].astype(f32)
        o_ref[0, :, cols] = (g * jax.nn.sigmoid(g) * yn).astype(o_ref.dtype)


def _retention(proj, gn_g, batch, seq):
    c = min(RET_BLOCK, seq)
    dh = RET_HEAD_DIM
    w = RET_WIDTH
    decay_in, xi_b, zeta_b, chunk_decay = _retention_tables(c)
    cos, sin = _rope_tables(seq, dh // 2)
    col = lambda j: pl.BlockSpec((1, c, w), lambda b, s, j=j: (b, s, j))
    whole = lambda shape: pl.BlockSpec(shape, lambda b, s: (0,) * len(shape))
    return pl.pallas_call(
        functools.partial(_retention_kernel, chunk_decay=[float(x) for x in chunk_decay]),
        grid=(batch, seq // c),
        in_specs=[col(0), col(1), col(2), col(3),
                  pl.BlockSpec((c, dh // 2), lambda b, s: (s, 0)),
                  pl.BlockSpec((c, dh // 2), lambda b, s: (s, 0)),
                  whole((RET_HEADS, c, c)), whole((RET_HEADS, c, dh)), whole((RET_HEADS, c, dh)),
                  whole((1, w))],
        out_specs=pl.BlockSpec((1, c, w), lambda b, s: (b, s, 0)),
        out_shape=jax.ShapeDtypeStruct((batch, seq, w), bf16),
        scratch_shapes=[pltpu.VMEM((RET_HEADS, dh, dh), f32)],
        compiler_params=_params(("parallel", "arbitrary")),
        name="retention",
    )(proj, proj, proj, proj, jnp.asarray(cos, f32), jnp.asarray(sin, f32),
      decay_in, xi_b, zeta_b, gn_g.reshape(1, w))


CONV_ROWS = 128
CONV_HIST = 32
CONV_SUB = 32


def _conv_kernel(ga_ref, gb_ref, w_ref, b_ref, lg_ref, lb_ref, o_ref, ush):
    total = CONV_HIST + CONV_ROWS

    @pl.when(pl.program_id(1) == 0)
    def _():
        ush[0, 0:CONV_HIST, :] = jnp.zeros((CONV_HIST, CONV_CHANNELS), f32)

    ush[0, CONV_HIST:total, :] = ga_ref[0].astype(f32) * jax.nn.sigmoid(gb_ref[0].astype(f32))
    for r in range(1, SUBLANES):
        ush[r, 0:total - SUBLANES, :] = ush[0, r:r + total - SUBLANES, :]
    first = CONV_HIST - (CONV_TAPS - 1)
    for r0 in range(0, CONV_ROWS, CONV_SUB):
        acc = jnp.zeros((CONV_SUB, CONV_CHANNELS), f32)
        for t in range(CONV_TAPS):
            d = first + t
            lo = r0 + d - d % SUBLANES
            acc = acc + w_ref[t:t + 1, :] * ush[d % SUBLANES, lo:lo + CONV_SUB, :]
        y = _layer_norm(acc + b_ref[...], lg_ref[...], lb_ref[...])
        o_ref[0, r0:r0 + CONV_SUB, :] = (y * jax.nn.sigmoid(y)).astype(o_ref.dtype)
    ush[0, 0:CONV_HIST, :] = ush[0, CONV_ROWS:total, :]


def _conformer_conv(proj, conv_w, conv_b, ln_g, ln_b, batch, seq):
    c = CONV_CHANNELS
    a_blk = 4 * RET_WIDTH // c
    row = pl.BlockSpec((1, c), lambda b, s: (0, 0))
    return pl.pallas_call(
        _conv_kernel,
        grid=(batch, seq // CONV_ROWS),
        in_specs=[pl.BlockSpec((1, CONV_ROWS, c), lambda b, s: (b, s, a_blk)),
                  pl.BlockSpec((1, CONV_ROWS, c), lambda b, s: (b, s, a_blk + 1)),
                  pl.BlockSpec((CONV_TAPS, c), lambda b, s: (0, 0)),
                  row, row, row],
        out_specs=pl.BlockSpec((1, CONV_ROWS, c), lambda b, s: (b, s, 0)),
        out_shape=jax.ShapeDtypeStruct((batch, seq, c), bf16),
        scratch_shapes=[pltpu.VMEM((SUBLANES, CONV_HIST + CONV_ROWS, c), f32)],
        compiler_params=_params(("parallel", "arbitrary")),
        name="conformer_conv",
    )(proj, proj, conv_w, conv_b.reshape(1, c), ln_g.reshape(1, c), ln_b.reshape(1, c))


def _route(sel, aff):
    epg = EXPERTS_PER_GROUP
    gscore = []
    for g in range(N_GROUPS):
        v = sel[g * epg:(g + 1) * epg]
        best = None
        for a in range(epg):
            for b in range(a + 1, epg):
                s = v[a] + v[b]
                best = s if best is None else jnp.maximum(best, s)
        gscore.append(best)
    gmax = functools.reduce(jnp.maximum, gscore)
    taken = None
    gsel = []
    for g in range(N_GROUPS):
        hit = gscore[g] == gmax
        if taken is not None:
            hit = jnp.logical_and(hit, jnp.logical_not(taken))
        taken = hit if taken is None else jnp.logical_or(taken, hit)
        gsel.append(hit)

    def pick(rows, j):
        out = rows[(N_GROUPS - 1) * epg + j]
        for g in range(N_GROUPS - 2, -1, -1):
            out = jnp.where(gsel[g], rows[g * epg + j], out)
        return out

    x = [pick(sel, j) for j in range(epg)]
    a = [pick(aff, j) for j in range(epg)]
    base = jnp.zeros_like(x[0], dtype=jnp.int32)
    for g in range(1, N_GROUPS):
        base = jnp.where(gsel[g], g * epg, base)

    def first_max(vals, excluded):
        vmax = functools.reduce(jnp.maximum, vals)
        taken = None
        hits = []
        for j in range(epg):
            hit = vals[j] == vmax
            if excluded is not None:
                hit = jnp.logical_and(hit, jnp.logical_not(excluded[j]))
            if taken is not None:
                hit = jnp.logical_and(hit, jnp.logical_not(taken))
            taken = hit if taken is None else jnp.logical_or(taken, hit)
            hits.append(hit)
        return hits

    h1 = first_max(x, None)
    x2 = [jnp.where(h1[j], -jnp.inf, x[j]) for j in range(epg)]
    h2 = first_max(x2, h1)

    def gather(hits):
        idx = base
        val = jnp.zeros_like(a[0])
        for j in range(epg):
            idx = jnp.where(hits[j], base + j, idx)
            val = jnp.where(hits[j], a[j], val)
        return idx, val

    e0, a0 = gather(h1)
    e1, a1 = gather(h2)
    tot = a0 + a1
    return e0, e1, a0 / tot, a1 / tot


def _router_weights(router_w):
    d, e = router_w.shape
    w = router_w.astype(f32)
    w_hi = w.astype(bf16)
    w_lo = (w - w_hi.astype(f32)).astype(bf16)
    top = jnp.concatenate([w_hi, w_lo, jnp.zeros((d, LANES - 2 * e), bf16)], axis=1)
    bot = jnp.concatenate([w_hi, jnp.zeros((d, LANES - e), bf16)], axis=1)
    return jnp.concatenate([top, bot], axis=0)


def _proj_norm_router_kernel(a_ref, w_ref, h_ref, g_ref, b_ref, rw_ref, rb_ref,
                             hn_ref, hb_ref, ridx_ref, rwgt_ref, *, alpha):
    m = jnp.dot(a_ref[...], w_ref[...], preferred_element_type=f32)
    hn = _layer_norm(alpha * h_ref[...] + m, g_ref[...], b_ref[...])
    hn_ref[...] = hn
    hi = hn.astype(bf16)
    hb_ref[...] = hi
    lo = (hn - hi.astype(f32)).astype(bf16)
    d = hn.shape[1]
    parts = (jnp.dot(hi, rw_ref[0:d, :], preferred_element_type=f32)
             + jnp.dot(lo, rw_ref[d:2 * d, :], preferred_element_type=f32))
    parts = parts.T
    logits = parts[0:N_EXPERTS] + parts[N_EXPERTS:2 * N_EXPERTS]
    aff = jax.nn.sigmoid(logits)
    sel = aff + rb_ref[...]
    e0, e1, w0, w1 = _route([sel[e:e + 1, :] for e in range(N_EXPERTS)],
                            [aff[e:e + 1, :] for e in range(N_EXPERTS)])
    ridx_ref[0:1, :] = e0
    ridx_ref[1:2, :] = e1
    rwgt_ref[0:1, :] = w0
    rwgt_ref[1:2, :] = w1


def _proj_norm_router(a, w, h, ln_g, ln_b, router_ws, router_b, alpha, tm=512):
    t, k = a.shape
    d = w.shape[1]
    tm = min(tm, t)
    row = pl.BlockSpec((1, d), lambda i: (0, 0))
    return pl.pallas_call(
        functools.partial(_proj_norm_router_kernel, alpha=alpha),
        grid=(t // tm,),
        in_specs=[pl.BlockSpec((tm, k), lambda i: (i, 0)),
                  pl.BlockSpec((k, d), lambda i: (0, 0)),
                  pl.BlockSpec((tm, d), lambda i: (i, 0)),
                  row, row,
                  pl.BlockSpec((2 * d, LANES), lambda i: (0, 0)),
                  pl.BlockSpec((N_EXPERTS, 1), lambda i: (0, 0))],
        out_specs=[pl.BlockSpec((tm, d), lambda i: (i, 0)),
                   pl.BlockSpec((tm, d), lambda i: (i, 0)),
                   pl.BlockSpec((2, tm), lambda i: (0, i)),
                   pl.BlockSpec((2, tm), lambda i: (0, i))],
        out_shape=[jax.ShapeDtypeStruct((t, d), f32), jax.ShapeDtypeStruct((t, d), bf16),
                   jax.ShapeDtypeStruct((2, t), jnp.int32), jax.ShapeDtypeStruct((2, t), f32)],
        compiler_params=_params(("parallel",)),
        name="proj_norm_router",
    )(a, w, h, ln_g.reshape(1, d), ln_b.reshape(1, d), router_ws, router_b.reshape(N_EXPERTS, 1))


MOE_TILE = 512


def _experts_kernel(te_ref, tv_ref, x_ref, wgu_ref, wd_ref, *refs, has_prev):
    o_ref, wgu_b, wd_b = refs[1:] if has_prev else refs
    i = pl.program_id(0)
    prev = te_ref[jnp.maximum(i - 1, 0)]

    @pl.when(jnp.logical_or(i == 0, te_ref[i] != prev))
    def _():
        wgu_b[...] = wgu_ref[0].astype(bf16)
        wd_b[...] = wd_ref[0].astype(bf16)

    @pl.when(tv_ref[i] > 0)
    def _():
        hgu = jnp.dot(x_ref[...], wgu_b[...], preferred_element_type=f32)
        a, b = hgu[:, :EXPERT_FF], hgu[:, EXPERT_FF:]
        act = (a * jax.nn.sigmoid(a) * b).astype(bf16)
        o_ref[...] = jnp.dot(act, wd_b[...], preferred_element_type=f32).astype(o_ref.dtype)

    @pl.when(tv_ref[i] == 0)
    def _():
        o_ref[...] = jnp.zeros_like(o_ref)


def _experts(xs, w_gu, w_down, layer, tile_expert, tile_valid, tile0, rows, prev=None):
    d = xs.shape[1]
    ff2 = w_gu.shape[3]
    in_specs = [pl.BlockSpec((MOE_TILE, d), lambda i, te, tv: (i, 0)),
                pl.BlockSpec((None, 1, d, ff2), lambda i, te, tv: (layer, te[i], 0, 0)),
                pl.BlockSpec((None, 1, ff2 // 2, d), lambda i, te, tv: (layer, te[i], 0, 0))]
    args = [tile_expert, tile_valid, xs, w_gu, w_down]
    if prev is not None:
        in_specs.append(pl.BlockSpec(memory_space=pl.ANY))
        args.append(prev)
    grid_spec = pltpu.PrefetchScalarGridSpec(
        num_scalar_prefetch=2,
        grid=(xs.shape[0] // MOE_TILE,),
        in_specs=in_specs,
        out_specs=pl.BlockSpec((MOE_TILE, d), lambda i, te, tv: (i + tile0, 0)),
        scratch_shapes=[pltpu.VMEM((d, ff2), bf16), pltpu.VMEM((ff2 // 2, d), bf16)],
    )
    return pl.pallas_call(
        functools.partial(_experts_kernel, has_prev=prev is not None),
        grid_spec=grid_spec,
        out_shape=jax.ShapeDtypeStruct((rows, d), bf16),
        input_output_aliases={} if prev is None else {len(args) - 1: 0},
        compiler_params=_params(("arbitrary",)),
        name="experts",
    )(*args)


def _dispatch(ridx):
    t = ridx.shape[1]
    slots = 2 * t
    es = ridx.T.reshape(slots)
    experts = jnp.arange(N_EXPERTS, dtype=jnp.int32)
    onehot = (es[:, None] == experts[None, :]).astype(jnp.int32)
    csum = jnp.cumsum(onehot, axis=0)
    rank = jnp.sum((csum - onehot) * onehot, axis=1)
    counts = csum[-1]
    tiles_per = (counts + MOE_TILE - 1) // MOE_TILE
    tile_end = jnp.cumsum(tiles_per)
    start = (tile_end - tiles_per) * MOE_TILE
    dest = start[es] + rank
    n_tiles = slots // MOE_TILE + N_EXPERTS
    tile_ids = jnp.arange(n_tiles, dtype=jnp.int32)
    tile_valid = (tile_ids < tile_end[-1]).astype(jnp.int32)
    tile_owner = jnp.minimum(jnp.sum((tile_ids[:, None] >= tile_end[None, :]).astype(jnp.int32), axis=1),
                             N_EXPERTS - 1)
    last_used = jnp.max(jnp.where(counts > 0, experts, 0))
    tile_expert = jnp.minimum(tile_owner, last_used)
    rows = n_tiles * MOE_TILE
    order = jnp.argsort(es, stable=True).astype(jnp.int32)
    first_sorted = jnp.cumsum(counts) - counts
    per_row = lambda table: jnp.repeat(table[tile_owner], MOE_TILE)
    row = jnp.arange(rows, dtype=jnp.int32)
    within = row - per_row(start)
    src = jnp.clip(per_row(first_sorted) + within, 0, slots - 1)
    tok = jnp.where(within < per_row(counts), order[src] // 2, row % t)
    return tok, dest, tile_expert, tile_valid


def _combine_norm_kernel(h_ref, y0_ref, y1_ref, w0_ref, w1_ref, g_ref, b_ref, o_ref, ob_ref, *, alpha):
    f = w0_ref[...] * y0_ref[...].astype(f32) + w1_ref[...] * y1_ref[...].astype(f32)
    hn = _layer_norm(alpha * h_ref[...] + f, g_ref[...], b_ref[...])
    o_ref[...] = hn
    ob_ref[...] = hn.astype(bf16)


def _combine_norm(h, y, w0, w1, ln_g, ln_b, alpha, tm=512):
    t, d = h.shape
    tm = min(tm, t)
    nt = t // tm
    blk = pl.BlockSpec((tm, d), lambda i: (i, 0))
    col = pl.BlockSpec((tm, 1), lambda i: (i, 0))
    row = pl.BlockSpec((1, d), lambda i: (0, 0))
    return pl.pallas_call(
        functools.partial(_combine_norm_kernel, alpha=alpha),
        grid=(nt,),
        in_specs=[blk, blk, pl.BlockSpec((tm, d), lambda i: (i + nt, 0)), col, col, row, row],
        out_specs=[blk, blk],
        out_shape=[jax.ShapeDtypeStruct((t, d), f32), jax.ShapeDtypeStruct((t, d), bf16)],
        compiler_params=_params(("parallel",)),
        name="combine_norm",
    )(h, y, y, w0, w1, ln_g.reshape(1, d), ln_b.reshape(1, d))


def _moe(hn, hb, ridx, rwgt, w_gu, w_down, layer, ln_g, ln_b, alpha):
    t, d = hn.shape
    tok, dest, tile_expert, tile_valid = _dispatch(ridx)
    n_tiles = tile_expert.shape[0]
    half = n_tiles // 2
    cut = half * MOE_TILE
    rows = n_tiles * MOE_TILE
    ys = _experts(hb.at[tok[:cut]].get(mode="promise_in_bounds"), w_gu, w_down, layer,
                  tile_expert[:half], tile_valid[:half], 0, rows)
    ys = _experts(hb.at[tok[cut:]].get(mode="promise_in_bounds"), w_gu, w_down, layer,
                  tile_expert[half:], tile_valid[half:], half, rows, prev=ys)
    y = ys.at[dest.reshape(t, 2).T.reshape(2 * t)].get(mode="promise_in_bounds")
    return _combine_norm(hn, y, rwgt[0].reshape(t, 1), rwgt[1].reshape(t, 1), ln_g, ln_b, alpha)


DSA_Q = DSA_HEADS * DSA_HEAD_DIM
DSA_KV = DSA_KV_HEADS * DSA_HEAD_DIM
DSA_QI = IDX_HEADS * IDX_DIM
DSA_MAIN = DSA_Q + 2 * DSA_KV + DSA_QI


def _lane_iota(shape):
    return lax.broadcasted_iota(jnp.int32, shape, 1)


def _rope128(x, cos2, sin2):
    return x * cos2 + pltpu.roll(x, LANES // 2, 1) * sin2


def _rope64(x, cos4, sin4):
    lane = _lane_iota(x.shape)
    partner = jnp.where(lane % IDX_DIM < IDX_DIM // 2,
                        pltpu.roll(x, LANES - IDX_DIM // 2, 1), pltpu.roll(x, IDX_DIM // 2, 1))
    return x * cos4 + partner * sin4


def _dsa_prep_kernel(p_ref, t_ref, c2_ref, s2_ref, c4_ref, s4_ref,
                     q_ref, k_ref, v_ref, qi_ref, ki_ref):
    c2, s2, c4, s4 = c2_ref[...], s2_ref[...], c4_ref[...], s4_ref[...]
    qscale = LOG2E * DSA_HEAD_DIM ** -0.5
    for j in range(DSA_Q // LANES):
        x = p_ref[0, :, j * LANES:(j + 1) * LANES].astype(f32)
        q_ref[0, :, j * LANES:(j + 1) * LANES] = (_rope128(x, c2, s2) * qscale).astype(bf16)
    for j in range(DSA_KV // LANES):
        o = DSA_Q + j * LANES
        k_ref[0, :, j * LANES:(j + 1) * LANES] = _rope128(p_ref[0, :, o:o + LANES].astype(f32), c2, s2).astype(bf16)
        o = DSA_Q + DSA_KV + j * LANES
        v_ref[0, :, j * LANES:(j + 1) * LANES] = p_ref[0, :, o:o + LANES].astype(bf16)
    lane = _lane_iota((p_ref.shape[1], LANES))
    low = lane < IDX_DIM
    iscale = IDX_DIM ** -0.5
    for j in range(DSA_QI // LANES):
        o = DSA_Q + 2 * DSA_KV + j * LANES
        r = _rope64(p_ref[0, :, o:o + LANES].astype(f32), c4, s4) * iscale
        qi_ref[0, 2 * j] = jnp.where(low, r, 0.0).astype(bf16)
        qi_ref[0, 2 * j + 1] = jnp.where(low, pltpu.roll(r, LANES // 2, 1), 0.0).astype(bf16)
    ki_ref[0] = jnp.where(low, _rope64(t_ref[0], c4, s4), 0.0).astype(bf16)


def _dsa_prep(main, tail, batch, seq, ts=256):
    ts = min(ts, seq)
    c, s = _rope_tables(seq, DSA_HEAD_DIM // 2)
    c2 = jnp.asarray(np.concatenate([c, c], axis=1), f32)
    s2 = jnp.asarray(np.concatenate([-s, s], axis=1), f32)
    c, s = _rope_tables(seq, IDX_DIM // 2)
    c4 = jnp.asarray(np.concatenate([c, c, c, c], axis=1), f32)
    s4 = jnp.asarray(np.concatenate([-s, s, -s, s], axis=1), f32)
    tab = pl.BlockSpec((ts, LANES), lambda b, i: (i, 0))
    out = lambda w: pl.BlockSpec((1, ts, w), lambda b, i: (b, i, 0))
    return pl.pallas_call(
        _dsa_prep_kernel,
        grid=(batch, seq // ts),
        in_specs=[out(DSA_MAIN), out(LANES), tab, tab, tab, tab],
        out_specs=[out(DSA_Q), out(DSA_KV), out(DSA_KV),
                   pl.BlockSpec((1, IDX_HEADS, ts, LANES), lambda b, i: (b, 0, i, 0)), out(LANES)],
        out_shape=[jax.ShapeDtypeStruct((batch, seq, DSA_Q), bf16),
                   jax.ShapeDtypeStruct((batch, seq, DSA_KV), bf16),
                   jax.ShapeDtypeStruct((batch, seq, DSA_KV), bf16),
                   jax.ShapeDtypeStruct((batch, IDX_HEADS, seq, LANES), bf16),
                   jax.ShapeDtypeStruct((batch, seq, LANES), bf16)],
        compiler_params=_params(("parallel", "parallel")),
        name="dsa_prep",
    )(main, tail, c2, s2, c4, s4)


SEL_ROWS = 256
SEL_CHUNK = 512
SEL_SUB = 256


def _order_key(x):
    bits = pltpu.bitcast(x, jnp.int32)
    return bits ^ ((bits >> 31) & 0x7FFFFFFF)


def _select_kernel(qi_ref, ki_ref, t_ref, bias_ref, key_ref, wb_ref, hi_ref, lo_ref, *, n_sel, n_chunks):
    tq, kc = SEL_ROWS, SEL_CHUNK
    n_sub = kc // SEL_SUB
    qt = pl.program_id(1)
    q0 = qt * tq
    n_act = lax.div(qt, kc // tq) + 1
    qpos = q0 + lax.broadcasted_iota(jnp.int32, (tq, 1), 0)
    wscale = IDX_HEADS ** -0.5

    for h in range(IDX_HEADS):
        wb_ref[h] = jnp.broadcast_to(t_ref[0, :, IDX_DIM + h:IDX_DIM + h + 1] * wscale, (tq, LANES))
    q_all = qi_ref[0].reshape(IDX_HEADS * tq, LANES)

    def score_chunk(c, carry):
        off = pl.multiple_of(c * kc, kc)
        for s in range(kc // SEL_SUB):
            ki = ki_ref[0, pl.ds(off + s * SEL_SUB, SEL_SUB), :]
            r = lax.dot_general(q_all, ki, NT_DIMS, preferred_element_type=f32)
            acc = jnp.zeros((tq, SEL_SUB), f32)
            for h in range(IDX_HEADS):
                w = jnp.concatenate([wb_ref[h]] * (SEL_SUB // LANES), axis=1)
                acc = acc + jnp.maximum(r[h * tq:(h + 1) * tq], 0.0) * w
            kpos = off + s * SEL_SUB + _lane_iota((tq, SEL_SUB))
            sc = jnp.where(kpos <= qpos, acc + 0.0, -jnp.inf)
            key_ref[c, :, s * SEL_SUB:(s + 1) * SEL_SUB] = _order_key(sc)
            kt = _order_key(sc.T)
            hi_ref[c * n_sub + s] = (kt >> 16).astype(jnp.int16)
            lo_ref[c * n_sub + s] = ((kt & 0xFFFF) + I16_MIN).astype(jnp.int16)
        return carry

    lax.fori_loop(0, n_act, score_chunk, 0)
    n_blk = n_act * n_sub

    def count_ge16(ref, cand):
        cand = cand.astype(jnp.int16)

        def body(c, acc):
            for s in range(n_sub):
                hit = jnp.where(ref[c * n_sub + s] >= cand, jnp.int16(1), jnp.int16(0))
                for r in range(SEL_SUB // PACKED_ROWS):
                    acc = acc + hit[r * PACKED_ROWS:(r + 1) * PACKED_ROWS, :]
            return acc
        acc = lax.fori_loop(0, n_act, body, jnp.zeros((PACKED_ROWS, tq), jnp.int16))
        return jnp.sum(acc.astype(f32), axis=0, keepdims=True)

    def rank_select16(ref, rank):
        ans = jnp.where(count_ge16(ref, jnp.zeros((1, tq), jnp.int32)) >= rank, 0, I16_MIN).astype(jnp.int32)

        def bit_step(i, ans):
            cand = ans | jnp.left_shift(jnp.int32(1), 14 - i)
            return jnp.where(count_ge16(ref, cand) >= rank, cand, ans)

        return lax.fori_loop(0, 15, bit_step, ans)

    def count_gt16(ref, val):
        return jnp.where(val == I16_MAX, 0.0, count_ge16(ref, jnp.minimum(val + 1, I16_MAX)))

    kf = float(n_sel)
    top = rank_select16(hi_ref, kf)
    above = count_gt16(hi_ref, top)
    top16 = top.astype(jnp.int16)

    def low_block(j, carry):
        lo_ref[j] = jnp.where(hi_ref[j] == top16, lo_ref[j], jnp.int16(I16_MIN))
        return carry

    lax.fori_loop(0, n_blk, low_block, 0)
    bottom = rank_select16(lo_ref, kf - above)
    need_row = kf - above - count_gt16(lo_ref, bottom)

    rows = jnp.concatenate([top.astype(f32), bottom.astype(f32), need_row,
                            jnp.zeros((LANES - 3, tq), f32)], axis=0).T
    ans = rows[:, 0:1].astype(jnp.int32) * 65536 + (rows[:, 1:2].astype(jnp.int32) - I16_MIN)
    need = rows[:, 2:3]
    ans = jnp.where(qpos < n_sel, INT_MIN, ans)

    upper = (lax.broadcasted_iota(jnp.int32, (SEL_SUB, SEL_SUB), 0)
             < lax.broadcasted_iota(jnp.int32, (SEL_SUB, SEL_SUB), 1)).astype(bf16)

    def emit_chunk(c, seen):
        off = pl.multiple_of(c * kc, kc)
        for s in range(kc // SEL_SUB):
            key = key_ref[c, :, s * SEL_SUB:(s + 1) * SEL_SUB]
            eq = key == ans
            eqf = jnp.where(eq, 1.0, 0.0)
            before = seen + jnp.dot(eqf.astype(bf16), upper, preferred_element_type=f32)
            keep = jnp.logical_or(key > ans, jnp.logical_and(eq, before < need))
            kpos = off + s * SEL_SUB + _lane_iota((tq, SEL_SUB))
            keep = jnp.logical_and(keep, kpos <= qpos)
            bias_ref[0, c, :, s * SEL_SUB:(s + 1) * SEL_SUB] = jnp.where(keep, 0.0, NEG_BIG).astype(bf16)
            seen = seen + jnp.sum(eqf, axis=1, keepdims=True)
        return seen

    lax.fori_loop(0, n_act, emit_chunk, jnp.zeros((tq, 1), f32))

    def fill_chunk(c, carry):
        bias_ref[0, c] = jnp.full((tq, kc), NEG_BIG, bf16)
        return carry

    lax.fori_loop(n_act, n_chunks, fill_chunk, 0)


def _select(qi, ki, tail, batch, seq, n_sel):
    n_chunks = seq // SEL_CHUNK
    assert SEL_CHUNK >= n_sel and SEL_CHUNK % SEL_ROWS == 0
    assert seq <= I16_MAX
    return pl.pallas_call(
        functools.partial(_select_kernel, n_sel=n_sel, n_chunks=n_chunks),
        grid=(batch, seq // SEL_ROWS),
        in_specs=[pl.BlockSpec((1, IDX_HEADS, SEL_ROWS, LANES), lambda b, i: (b, 0, i, 0)),
                  pl.BlockSpec((1, seq, LANES), lambda b, i: (b, 0, 0)),
                  pl.BlockSpec((1, SEL_ROWS, LANES), lambda b, i: (b, i, 0))],
        out_specs=pl.BlockSpec((1, n_chunks, SEL_ROWS, SEL_CHUNK), lambda b, i: (b, 0, i, 0)),
        out_shape=jax.ShapeDtypeStruct((batch, n_chunks, seq, SEL_CHUNK), bf16),
        scratch_shapes=[pltpu.VMEM((n_chunks, SEL_ROWS, SEL_CHUNK), jnp.int32),
                        pltpu.VMEM((IDX_HEADS, SEL_ROWS, LANES), f32),
                        pltpu.VMEM((seq // SEL_SUB, SEL_SUB, SEL_ROWS), jnp.int16),
                        pltpu.VMEM((seq // SEL_SUB, SEL_SUB, SEL_ROWS), jnp.int16)],
        compiler_params=_params(("parallel", "parallel")),
        name="dsa_select",
    )(qi, ki, tail)


ATT_TILE = SEL_CHUNK
ATT_GROUP = DSA_HEADS // DSA_KV_HEADS


def _attention_kernel(qt_ref, kt_ref, q_ref, k_ref, v_ref, b_ref, o_ref, m_ref, acc_ref):
    p_id = pl.program_id(2)
    i = qt_ref[p_id]
    j = kt_ref[p_id]
    dh = DSA_HEAD_DIM
    t = ATT_TILE

    @pl.when(j == 0)
    def _():
        m_ref[...] = jnp.full(m_ref.shape, NEG_BIG, f32)
        acc_ref[...] = jnp.zeros_like(acc_ref)

    k = k_ref[0]
    v_ones = jnp.concatenate([v_ref[0], jnp.ones((t, dh), bf16)], axis=1)
    for h in range(ATT_GROUP):
        s = lax.dot_general(q_ref[0, :, h * dh:(h + 1) * dh], k, NT_DIMS, preferred_element_type=f32)
        s = s + b_ref[0, 0].astype(f32)
        m_old = m_ref[h]
        m_new = jnp.maximum(m_old, jnp.max(s, axis=1, keepdims=True))
        p = jnp.concatenate([jnp.exp2(s[:, c * LANES:(c + 1) * LANES] - m_new) for c in range(t // LANES)],
                            axis=1)
        scale = jnp.exp2(m_old - m_new)
        pv = jnp.dot(p.astype(bf16), v_ones, preferred_element_type=f32)
        acc_ref[h] = jnp.concatenate([scale, scale], axis=1) * acc_ref[h] + pv
        m_ref[h] = m_new

    @pl.when(j == i)
    def _():
        for h in range(ATT_GROUP):
            acc = acc_ref[h]
            o_ref[0, :, h * dh:(h + 1) * dh] = (acc[:, :dh] / acc[:, dh:]).astype(o_ref.dtype)


def _attention(q, k, v, bias, batch, seq):
    t = ATT_TILE
    n = seq // t
    dh = DSA_HEAD_DIM
    gw = ATT_GROUP * dh
    pairs = [(i, j) for i in range(n) for j in range(i + 1)]
    q_tile = jnp.asarray([p[0] for p in pairs], jnp.int32)
    k_tile = jnp.asarray([p[1] for p in pairs], jnp.int32)
    grid_spec = pltpu.PrefetchScalarGridSpec(
        num_scalar_prefetch=2,
        grid=(batch, DSA_KV_HEADS, len(pairs)),
        in_specs=[pl.BlockSpec((1, t, gw), lambda b, g, p, qt, kt: (b, qt[p], g)),
                  pl.BlockSpec((1, t, dh), lambda b, g, p, qt, kt: (b, kt[p], g)),
                  pl.BlockSpec((1, t, dh), lambda b, g, p, qt, kt: (b, kt[p], g)),
                  pl.BlockSpec((1, 1, t, t), lambda b, g, p, qt, kt: (b, kt[p], qt[p], 0))],
        out_specs=pl.BlockSpec((1, t, gw), lambda b, g, p, qt, kt: (b, qt[p], g)),
        scratch_shapes=[pltpu.VMEM((ATT_GROUP, t, LANES), f32), pltpu.VMEM((ATT_GROUP, t, 2 * dh), f32)],
    )
    return pl.pallas_call(
        _attention_kernel,
        grid_spec=grid_spec,
        out_shape=jax.ShapeDtypeStruct((batch, seq, DSA_Q), bf16),
        compiler_params=_params(("parallel", "parallel", "arbitrary")),
        name="dsa_attention",
    )(q_tile, k_tile, q, k, v, bias)


def _even_mixer(hb, w_in, gn_g, conv_w, conv_b, conv_ln_g, conv_ln_b, batch, seq):
    proj = _matmul(hb, w_in, bf16).reshape(batch, seq, -1)
    ret = _retention(proj, gn_g, batch, seq)
    conv = _conformer_conv(proj, conv_w, conv_b, conv_ln_g, conv_ln_b, batch, seq)
    return jnp.concatenate([ret, conv], axis=-1).reshape(batch * seq, -1)


def _odd_mixer(hb, w_in, batch, seq):
    wt = w_in.T
    main = _matmul_nt(hb, wt, bf16, 0, DSA_MAIN).reshape(batch, seq, DSA_MAIN)
    tail = _matmul_nt(hb, wt, f32, DSA_MAIN, LANES).reshape(batch, seq, LANES)
    q, k, v, qi, ki = _dsa_prep(main, tail, batch, seq)
    bias = _select(qi, ki, tail, batch, seq, min(TOPK_MAX, seq // 4))
    return _attention(q, k, v, bias, batch, seq).reshape(batch * seq, DSA_Q)


def kernel(x, even_w_in, even_ret_gn_g, even_conv_w, even_conv_b, even_conv_ln_g, even_conv_ln_b, even_w_out,
           odd_w_in, odd_w_out, mix_ln_g, mix_ln_b, moe_w_gu, moe_w_down, ffn_ln_g, ffn_ln_b, router_w, router_b):
    batch, seq, d = x.shape
    depth = mix_ln_g.shape[0]
    alpha = (2 * depth) ** 0.25
    router_ws = _router_weights(router_w)
    h = x.reshape(batch * seq, d).astype(f32)
    hb = h
    for layer in range(depth):
        i = layer // 2
        if layer % 2 == 0:
            mixed = _even_mixer(hb, even_w_in[i], even_ret_gn_g[i], even_conv_w[i], even_conv_b[i],
                                even_conv_ln_g[i], even_conv_ln_b[i], batch, seq)
            w_out = even_w_out[i]
        else:
            mixed = _odd_mixer(hb, odd_w_in[i], batch, seq)
            w_out = odd_w_out[i]
        hn, hnb, ridx, rwgt = _proj_norm_router(mixed, w_out.astype(bf16), h, mix_ln_g[layer], mix_ln_b[layer],
                                                router_ws, router_b.astype(f32), alpha)
        h, hb = _moe(hn, hnb, ridx, rwgt, moe_w_gu, moe_w_down, layer,
                     ffn_ln_g[layer], ffn_ln_b[layer], alpha)
    return h.reshape(batch, seq, d).astype(x.dtype)
```

```python
import functools

import numpy as np
import jax
import jax.numpy as jnp
from jax import lax
from jax.experimental import pallas as pl
from jax.experimental.pallas import tpu as pltpu

f32 = jnp.float32
bf16 = jnp.bfloat16

ROPE_THETA = 10000.0
LN_EPS = 1e-5
RET_HEADS = 4
RET_HEAD_DIM = 256
RET_WIDTH = RET_HEADS * RET_HEAD_DIM
RET_BLOCK = 256
CONV_CHANNELS = 1024
CONV_TAPS = 31
DSA_HEADS = 16
DSA_HEAD_DIM = 128
DSA_KV_HEADS = 4
IDX_HEADS = 16
IDX_DIM = 64
TOPK_MAX = 256
N_EXPERTS = 16
N_GROUPS = 4
EXPERTS_PER_GROUP = N_EXPERTS // N_GROUPS
EXPERT_FF = 512

LANES = 128
SUBLANES = 8
PACKED_ROWS = 16
VMEM_LIMIT = 56 * 1024 * 1024
NEG_BIG = -1e30
LOG2E = 1.4426950408889634
INT_MIN = -2 ** 31
I16_MIN, I16_MAX = -2 ** 15, 2 ** 15 - 1

NT_DIMS = (((1,), (1,)), ((), ()))
TN_DIMS = (((0,), (0,)), ((), ()))


def _params(sem):
    return pltpu.CompilerParams(dimension_semantics=sem, vmem_limit_bytes=VMEM_LIMIT)


def _layer_norm(z, g, b):
    mu = jnp.mean(z, axis=-1, keepdims=True)
    zc = z - mu
    var = jnp.mean(zc * zc, axis=-1, keepdims=True)
    return zc * lax.rsqrt(var + LN_EPS) * g + b


def _mm_kernel(x_ref, w_ref, o_ref, wb_ref):
    @pl.when(pl.program_id(1) == 0)
    def _():
        wb_ref[...] = w_ref[...].astype(bf16)

    o_ref[...] = jnp.dot(x_ref[...].astype(bf16), wb_ref[...], preferred_element_type=f32).astype(o_ref.dtype)


def _matmul(x, w, out_dtype, n=None, tm=1024, tn=1024):
    m, k = x.shape
    n = w.shape[1] if n is None else n
    tm, tn = min(tm, m), min(tn, n)
    assert m % tm == 0 and n % tn == 0
    return pl.pallas_call(
        _mm_kernel,
        grid=(n // tn, m // tm),
        in_specs=[pl.BlockSpec((tm, k), lambda j, i: (i, 0)),
                  pl.BlockSpec((k, tn), lambda j, i: (0, j))],
        out_specs=pl.BlockSpec((tm, tn), lambda j, i: (i, j)),
        out_shape=jax.ShapeDtypeStruct((m, n), out_dtype),
        scratch_shapes=[pltpu.VMEM((k, tn), bf16)],
        compiler_params=_params(("arbitrary", "arbitrary")),
        name="matmul",
    )(x, w)


def _mm_nt_kernel(x_ref, wt_ref, o_ref, wb_ref, *, valid):
    @pl.when(pl.program_id(1) == 0)
    def _():
        w = wt_ref[...]
        if valid < wt_ref.shape[0]:
            w = jnp.where(lax.broadcasted_iota(jnp.int32, w.shape, 0) < valid, w, 0.0)
        wb_ref[...] = w.astype(bf16)

    o_ref[...] = lax.dot_general(x_ref[...].astype(bf16), wb_ref[...], NT_DIMS,
                                 preferred_element_type=f32).astype(o_ref.dtype)


def _matmul_nt(x, wt, out_dtype, row0, n, tm=1024, tn=1024):
    m, k = x.shape
    tm, tn = min(tm, m), min(tn, n)
    assert m % tm == 0 and n % tn == 0 and row0 % tn == 0
    valid = min(tn, wt.shape[0] - row0 - (n - tn))
    assert valid == tn or n == tn
    return pl.pallas_call(
        functools.partial(_mm_nt_kernel, valid=valid),
        grid=(n // tn, m // tm),
        in_specs=[pl.BlockSpec((tm, k), lambda j, i: (i, 0)),
                  pl.BlockSpec((tn, k), lambda j, i: (row0 // tn + j, 0))],
        out_specs=pl.BlockSpec((tm, tn), lambda j, i: (i, j)),
        out_shape=jax.ShapeDtypeStruct((m, n), out_dtype),
        scratch_shapes=[pltpu.VMEM((tn, k), bf16)],
        compiler_params=_params(("arbitrary", "arbitrary")),
        name="matmul_nt",
    )(x, wt)


def _retention_tables(chunk):
    h = np.arange(RET_HEADS, dtype=np.float64)
    log_g = np.log(1.0 - 2.0 ** (-5.0 - h))
    j = np.arange(chunk, dtype=np.float64)
    diff = j[:, None] - j[None, :]
    decay_in = np.where(diff[None] >= 0, np.exp(np.maximum(diff, 0.0)[None] * log_g[:, None, None]), 0.0)
    xi = np.exp((j[None, :] + 1.0) * log_g[:, None])
    zeta = np.exp((chunk - 1.0 - j[None, :]) * log_g[:, None])
    chunk_decay = np.exp(chunk * log_g)
    xi_b = np.broadcast_to(xi[:, :, None], (RET_HEADS, chunk, RET_HEAD_DIM))
    zeta_b = np.broadcast_to(zeta[:, :, None], (RET_HEADS, chunk, RET_HEAD_DIM))
    return jnp.asarray(decay_in, f32), jnp.asarray(xi_b, f32), jnp.asarray(zeta_b, f32), chunk_decay


def _rope_tables(seq, half):
    inv = ROPE_THETA ** (-np.arange(half, dtype=np.float64) / half)
    ang = np.arange(seq, dtype=np.float64)[:, None] * inv[None, :]
    return np.cos(ang), np.sin(ang)


def _retention_kernel(q_ref, k_ref, v_ref, g_ref, cos_ref, sin_ref, din_ref, xi_ref, zeta_ref,
                      gn_ref, o_ref, state_ref, *, chunk_decay):
    @pl.when(pl.program_id(1) == 0)
    def _():
        state_ref[...] = jnp.zeros_like(state_ref)

    cos = cos_ref[...]
    sin = sin_ref[...]
    dh = RET_HEAD_DIM
    half = dh // 2

    def rope(x):
        x1, x2 = x[:, :half], x[:, half:]
        return jnp.concatenate([x1 * cos - x2 * sin, x2 * cos + x1 * sin], axis=1)

    for h in range(RET_HEADS):
        cols = slice(h * dh, (h + 1) * dh)
        q = rope(q_ref[0, :, cols].astype(f32))
        k = rope(k_ref[0, :, cols].astype(f32)) * (dh ** -0.5)
        qb = q.astype(bf16)
        kb = k.astype(bf16)
        vb = v_ref[0, :, cols].astype(bf16)
        attn = lax.dot_general(qb, kb, NT_DIMS, preferred_element_type=f32) * din_ref[h]
        inner = jnp.dot(attn.astype(bf16), vb, preferred_element_type=f32)
        state = state_ref[h]
        cross = jnp.dot(qb, state.astype(bf16), preferred_element_type=f32) * xi_ref[h]
        kz = (k * zeta_ref[h]).astype(bf16)
        state_ref[h] = state * chunk_decay[h] + lax.dot_general(kz, vb, TN_DIMS, preferred_element_type=f32)
        y = inner + cross
        mu = jnp.mean(y, axis=-1, keepdims=True)
        yc = y - mu
        var = jnp.mean(yc * yc, axis=-1, keepdims=True)
        yn = yc * lax.rsqrt(var + LN_EPS) * gn_ref[:, cols]
        g = g_ref[0, :, cols].astype(f32)
        o_ref[0, :, cols] = (g * jax.nn.sigmoid(g) * yn).astype(o_ref.dtype)


def _retention(proj, gn_g, batch, seq):
    c = min(RET_BLOCK, seq)
    dh = RET_HEAD_DIM
    w = RET_WIDTH
    decay_in, xi_b, zeta_b, chunk_decay = _retention_tables(c)
    cos, sin = _rope_tables(seq, dh // 2)
    col = lambda j: pl.BlockSpec((1, c, w), lambda b, s, j=j: (b, s, j))
    whole = lambda shape: pl.BlockSpec(shape, lambda b, s: (0,) * len(shape))
    return pl.pallas_call(
        functools.partial(_retention_kernel, chunk_decay=[float(x) for x in chunk_decay]),
        grid=(batch, seq // c),
        in_specs=[col(0), col(1), col(2), col(3),
                  pl.BlockSpec((c, dh // 2), lambda b, s: (s, 0)),
                  pl.BlockSpec((c, dh // 2), lambda b, s: (s, 0)),
                  whole((RET_HEADS, c, c)), whole((RET_HEADS, c, dh)), whole((RET_HEADS, c, dh)),
                  whole((1, w))],
        out_specs=pl.BlockSpec((1, c, w), lambda b, s: (b, s, 0)),
        out_shape=jax.ShapeDtypeStruct((batch, seq, w), bf16),
        scratch_shapes=[pltpu.VMEM((RET_HEADS, dh, dh), f32)],
        compiler_params=_params(("parallel", "arbitrary")),
        name="retention",
    )(proj, proj, proj, proj, jnp.asarray(cos, f32), jnp.asarray(sin, f32),
      decay_in, xi_b, zeta_b, gn_g.reshape(1, w))


CONV_ROWS = 128
CONV_HIST = 32
CONV_SUB = 32


def _conv_kernel(ga_ref, gb_ref, w_ref, b_ref, lg_ref, lb_ref, o_ref, ush):
    total = CONV_HIST + CONV_ROWS

    @pl.when(pl.program_id(1) == 0)
    def _():
        ush[0, 0:CONV_HIST, :] = jnp.zeros((CONV_HIST, CONV_CHANNELS), f32)

    ush[0, CONV_HIST:total, :] = ga_ref[0].astype(f32) * jax.nn.sigmoid(gb_ref[0].astype(f32))
    for r in range(1, SUBLANES):
        ush[r, 0:total - SUBLANES, :] = ush[0, r:r + total - SUBLANES, :]
    first = CONV_HIST - (CONV_TAPS - 1)
    for r0 in range(0, CONV_ROWS, CONV_SUB):
        acc = jnp.zeros((CONV_SUB, CONV_CHANNELS), f32)
        for t in range(CONV_TAPS):
            d = first + t
            lo = r0 + d - d % SUBLANES
            acc = acc + w_ref[t:t + 1, :] * ush[d % SUBLANES, lo:lo + CONV_SUB, :]
        y = _layer_norm(acc + b_ref[...], lg_ref[...], lb_ref[...])
        o_ref[0, r0:r0 + CONV_SUB, :] = (y * jax.nn.sigmoid(y)).astype(o_ref.dtype)
    ush[0, 0:CONV_HIST, :] = ush[0, CONV_ROWS:total, :]


def _conformer_conv(proj, conv_w, conv_b, ln_g, ln_b, batch, seq):
    c = CONV_CHANNELS
    a_blk = 4 * RET_WIDTH // c
    row = pl.BlockSpec((1, c), lambda b, s: (0, 0))
    return pl.pallas_call(
        _conv_kernel,
        grid=(batch, seq // CONV_ROWS),
        in_specs=[pl.BlockSpec((1, CONV_ROWS, c), lambda b, s: (b, s, a_blk)),
                  pl.BlockSpec((1, CONV_ROWS, c), lambda b, s: (b, s, a_blk + 1)),
                  pl.BlockSpec((CONV_TAPS, c), lambda b, s: (0, 0)),
                  row, row, row],
        out_specs=pl.BlockSpec((1, CONV_ROWS, c), lambda b, s: (b, s, 0)),
        out_shape=jax.ShapeDtypeStruct((batch, seq, c), bf16),
        scratch_shapes=[pltpu.VMEM((SUBLANES, CONV_HIST + CONV_ROWS, c), f32)],
        compiler_params=_params(("parallel", "arbitrary")),
        name="conformer_conv",
    )(proj, proj, conv_w, conv_b.reshape(1, c), ln_g.reshape(1, c), ln_b.reshape(1, c))


def _route(sel, aff):
    epg = EXPERTS_PER_GROUP
    gscore = []
    for g in range(N_GROUPS):
        v = sel[g * epg:(g + 1) * epg]
        best = None
        for a in range(epg):
            for b in range(a + 1, epg):
                s = v[a] + v[b]
                best = s if best is None else jnp.maximum(best, s)
        gscore.append(best)
    gmax = functools.reduce(jnp.maximum, gscore)
    taken = None
    gsel = []
    for g in range(N_GROUPS):
        hit = gscore[g] == gmax
        if taken is not None:
            hit = jnp.logical_and(hit, jnp.logical_not(taken))
        taken = hit if taken is None else jnp.logical_or(taken, hit)
        gsel.append(hit)

    def pick(rows, j):
        out = rows[(N_GROUPS - 1) * epg + j]
        for g in range(N_GROUPS - 2, -1, -1):
            out = jnp.where(gsel[g], rows[g * epg + j], out)
        return out

    x = [pick(sel, j) for j in range(epg)]
    a = [pick(aff, j) for j in range(epg)]
    base = jnp.zeros_like(x[0], dtype=jnp.int32)
    for g in range(1, N_GROUPS):
        base = jnp.where(gsel[g], g * epg, base)

    def first_max(vals, excluded):
        vmax = functools.reduce(jnp.maximum, vals)
        taken = None
        hits = []
        for j in range(epg):
            hit = vals[j] == vmax
            if excluded is not None:
                hit = jnp.logical_and(hit, jnp.logical_not(excluded[j]))
            if taken is not None:
                hit = jnp.logical_and(hit, jnp.logical_not(taken))
            taken = hit if taken is None else jnp.logical_or(taken, hit)
            hits.append(hit)
        return hits

    h1 = first_max(x, None)
    x2 = [jnp.where(h1[j], -jnp.inf, x[j]) for j in range(epg)]
    h2 = first_max(x2, h1)

    def gather(hits):
        idx = base
        val = jnp.zeros_like(a[0])
        for j in range(epg):
            idx = jnp.where(hits[j], base + j, idx)
            val = jnp.where(hits[j], a[j], val)
        return idx, val

    e0, a0 = gather(h1)
    e1, a1 = gather(h2)
    tot = a0 + a1
    return e0, e1, a0 / tot, a1 / tot


def _router_weights(router_w):
    d, e = router_w.shape
    w = router_w.astype(f32)
    w_hi = w.astype(bf16)
    w_lo = (w - w_hi.astype(f32)).astype(bf16)
    top = jnp.concatenate([w_hi, w_lo, jnp.zeros((d, LANES - 2 * e), bf16)], axis=1)
    bot = jnp.concatenate([w_hi, jnp.zeros((d, LANES - e), bf16)], axis=1)
    return jnp.concatenate([top, bot], axis=0)


def _proj_norm_router_kernel(a_ref, w_ref, h_ref, g_ref, b_ref, rw_ref, rb_ref,
                             hn_ref, hb_ref, ridx_ref, rwgt_ref, *, alpha):
    m = jnp.dot(a_ref[...], w_ref[...], preferred_element_type=f32)
    hn = _layer_norm(alpha * h_ref[...] + m, g_ref[...], b_ref[...])
    hn_ref[...] = hn
    hi = hn.astype(bf16)
    hb_ref[...] = hi
    lo = (hn - hi.astype(f32)).astype(bf16)
    d = hn.shape[1]
    parts = (jnp.dot(hi, rw_ref[0:d, :], preferred_element_type=f32)
             + jnp.dot(lo, rw_ref[d:2 * d, :], preferred_element_type=f32))
    parts = parts.T
    logits = parts[0:N_EXPERTS] + parts[N_EXPERTS:2 * N_EXPERTS]
    aff = jax.nn.sigmoid(logits)
    sel = aff + rb_ref[...]
    e0, e1, w0, w1 = _route([sel[e:e + 1, :] for e in range(N_EXPERTS)],
                            [aff[e:e + 1, :] for e in range(N_EXPERTS)])
    ridx_ref[0:1, :] = e0
    ridx_ref[1:2, :] = e1
    rwgt_ref[0:1, :] = w0
    rwgt_ref[1:2, :] = w1


def _proj_norm_router(a, w, h, ln_g, ln_b, router_ws, router_b, alpha, tm=512):
    t, k = a.shape
    d = w.shape[1]
    tm = min(tm, t)
    row = pl.BlockSpec((1, d), lambda i: (0, 0))
    return pl.pallas_call(
        functools.partial(_proj_norm_router_kernel, alpha=alpha),
        grid=(t // tm,),
        in_specs=[pl.BlockSpec((tm, k), lambda i: (i, 0)),
                  pl.BlockSpec((k, d), lambda i: (0, 0)),
                  pl.BlockSpec((tm, d), lambda i: (i, 0)),
                  row, row,
                  pl.BlockSpec((2 * d, LANES), lambda i: (0, 0)),
                  pl.BlockSpec((N_EXPERTS, 1), lambda i: (0, 0))],
        out_specs=[pl.BlockSpec((tm, d), lambda i: (i, 0)),
                   pl.BlockSpec((tm, d), lambda i: (i, 0)),
                   pl.BlockSpec((2, tm), lambda i: (0, i)),
                   pl.BlockSpec((2, tm), lambda i: (0, i))],
        out_shape=[jax.ShapeDtypeStruct((t, d), f32), jax.ShapeDtypeStruct((t, d), bf16),
                   jax.ShapeDtypeStruct((2, t), jnp.int32), jax.ShapeDtypeStruct((2, t), f32)],
        compiler_params=_params(("parallel",)),
        name="proj_norm_router",
    )(a, w, h, ln_g.reshape(1, d), ln_b.reshape(1, d), router_ws, router_b.reshape(N_EXPERTS, 1))


MOE_TILE = 512


def _experts_kernel(te_ref, tv_ref, x_ref, wgu_ref, wd_ref, *refs, has_prev):
    o_ref, wgu_b, wd_b = refs[1:] if has_prev else refs
    i = pl.program_id(0)
    prev = te_ref[jnp.maximum(i - 1, 0)]

    @pl.when(jnp.logical_or(i == 0, te_ref[i] != prev))
    def _():
        wgu_b[...] = wgu_ref[0].astype(bf16)
        wd_b[...] = wd_ref[0].astype(bf16)

    @pl.when(tv_ref[i] > 0)
    def _():
        hgu = jnp.dot(x_ref[...], wgu_b[...], preferred_element_type=f32)
        a, b = hgu[:, :EXPERT_FF], hgu[:, EXPERT_FF:]
        act = (a * jax.nn.sigmoid(a) * b).astype(bf16)
        o_ref[...] = jnp.dot(act, wd_b[...], preferred_element_type=f32).astype(o_ref.dtype)

    @pl.when(tv_ref[i] == 0)
    def _():
        o_ref[...] = jnp.zeros_like(o_ref)


def _experts(xs, w_gu, w_down, layer, tile_expert, tile_valid, tile0, rows, prev=None):
    d = xs.shape[1]
    ff2 = w_gu.shape[3]
    in_specs = [pl.BlockSpec((MOE_TILE, d), lambda i, te, tv: (i, 0)),
                pl.BlockSpec((None, 1, d, ff2), lambda i, te, tv: (layer, te[i], 0, 0)),
                pl.BlockSpec((None, 1, ff2 // 2, d), lambda i, te, tv: (layer, te[i], 0, 0))]
    args = [tile_expert, tile_valid, xs, w_gu, w_down]
    if prev is not None:
        in_specs.append(pl.BlockSpec(memory_space=pl.ANY))
        args.append(prev)
    grid_spec = pltpu.PrefetchScalarGridSpec(
        num_scalar_prefetch=2,
        grid=(xs.shape[0] // MOE_TILE,),
        in_specs=in_specs,
        out_specs=pl.BlockSpec((MOE_TILE, d), lambda i, te, tv: (i + tile0, 0)),
        scratch_shapes=[pltpu.VMEM((d, ff2), bf16), pltpu.VMEM((ff2 // 2, d), bf16)],
    )
    return pl.pallas_call(
        functools.partial(_experts_kernel, has_prev=prev is not None),
        grid_spec=grid_spec,
        out_shape=jax.ShapeDtypeStruct((rows, d), bf16),
        input_output_aliases={} if prev is None else {len(args) - 1: 0},
        compiler_params=_params(("arbitrary",)),
        name="experts",
    )(*args)


def _dispatch(ridx):
    t = ridx.shape[1]
    slots = 2 * t
    es = ridx.T.reshape(slots)
    experts = jnp.arange(N_EXPERTS, dtype=jnp.int32)
    onehot = (es[:, None] == experts[None, :]).astype(jnp.int32)
    csum = jnp.cumsum(onehot, axis=0)
    rank = jnp.sum((csum - onehot) * onehot, axis=1)
    counts = csum[-1]
    tiles_per = (counts + MOE_TILE - 1) // MOE_TILE
    tile_end = jnp.cumsum(tiles_per)
    start = (tile_end - tiles_per) * MOE_TILE
    dest = start[es] + rank
    n_tiles = slots // MOE_TILE + N_EXPERTS
    tile_ids = jnp.arange(n_tiles, dtype=jnp.int32)
    tile_valid = (tile_ids < tile_end[-1]).astype(jnp.int32)
    tile_owner = jnp.minimum(jnp.sum((tile_ids[:, None] >= tile_end[None, :]).astype(jnp.int32), axis=1),
                             N_EXPERTS - 1)
    last_used = jnp.max(jnp.where(counts > 0, experts, 0))
    tile_expert = jnp.minimum(tile_owner, last_used)
    rows = n_tiles * MOE_TILE
    order = jnp.argsort(es, stable=True).astype(jnp.int32)
    first_sorted = jnp.cumsum(counts) - counts
    per_row = lambda table: jnp.repeat(table[tile_owner], MOE_TILE)
    row = jnp.arange(rows, dtype=jnp.int32)
    within = row - per_row(start)
    src = jnp.clip(per_row(first_sorted) + within, 0, slots - 1)
    tok = jnp.where(within < per_row(counts), order[src] // 2, row % t)
    return tok, dest, tile_expert, tile_valid


def _combine_norm_kernel(h_ref, y0_ref, y1_ref, w0_ref, w1_ref, g_ref, b_ref, *refs, alpha):
    o_ref, ob_ref = refs[-2:]
    f = w0_ref[...] * y0_ref[...].astype(f32) + w1_ref[...] * y1_ref[...].astype(f32)
    hn = _layer_norm(alpha * h_ref[...] + f, g_ref[...], b_ref[...])
    o_ref[...] = hn
    ob_ref[...] = hn.astype(bf16)


def _combine_norm(h, y, w0, w1, ln_g, ln_b, alpha, tile0, prev=None, tm=512):
    t, d = h.shape
    tm = min(tm, t)
    nt = y.shape[0] // (2 * tm)
    blk = pl.BlockSpec((tm, d), lambda i: (i + tile0, 0))
    col = pl.BlockSpec((tm, 1), lambda i: (i + tile0, 0))
    row = pl.BlockSpec((1, d), lambda i: (0, 0))
    in_specs = [blk, pl.BlockSpec((tm, d), lambda i: (i, 0)), pl.BlockSpec((tm, d), lambda i: (i + nt, 0)),
                col, col, row, row]
    args = [h, y, y, w0, w1, ln_g.reshape(1, d), ln_b.reshape(1, d)]
    aliases = {}
    if prev is not None:
        in_specs += [pl.BlockSpec(memory_space=pl.ANY)] * 2
        aliases = {len(args): 0, len(args) + 1: 1}
        args += list(prev)
    return pl.pallas_call(
        functools.partial(_combine_norm_kernel, alpha=alpha),
        grid=(nt,),
        in_specs=in_specs,
        out_specs=[blk, blk],
        out_shape=[jax.ShapeDtypeStruct((t, d), f32), jax.ShapeDtypeStruct((t, d), bf16)],
        input_output_aliases=aliases,
        compiler_params=_params(("arbitrary",)),
        name="combine_norm",
    )(*args)


def _moe(hn, hb, ridx, rwgt, w_gu, w_down, layer, ln_g, ln_b, alpha):
    t, d = hn.shape
    tok, dest, tile_expert, tile_valid = _dispatch(ridx)
    n_tiles = tile_expert.shape[0]
    half = n_tiles // 2
    cut = half * MOE_TILE
    rows = n_tiles * MOE_TILE
    ys = _experts(hb.at[tok[:cut]].get(mode="promise_in_bounds"), w_gu, w_down, layer,
                  tile_expert[:half], tile_valid[:half], 0, rows)
    ys = _experts(hb.at[tok[cut:]].get(mode="promise_in_bounds"), w_gu, w_down, layer,
                  tile_expert[half:], tile_valid[half:], half, rows, prev=ys)
    slot_rows = dest.reshape(t, 2)
    w0, w1 = rwgt[0].reshape(t, 1), rwgt[1].reshape(t, 1)
    th = t // 2
    out = None
    for part in range(2):
        rows_of = slot_rows[part * th:(part + 1) * th].T.reshape(2 * th)
        y = ys.at[rows_of].get(mode="promise_in_bounds")
        out = _combine_norm(hn, y, w0, w1, ln_g, ln_b, alpha, part * (th // min(512, t)), prev=out)
    return out


DSA_Q = DSA_HEADS * DSA_HEAD_DIM
DSA_KV = DSA_KV_HEADS * DSA_HEAD_DIM
DSA_QI = IDX_HEADS * IDX_DIM
DSA_MAIN = DSA_Q + 2 * DSA_KV + DSA_QI


def _lane_iota(shape):
    return lax.broadcasted_iota(jnp.int32, shape, 1)


def _rope128(x, cos2, sin2):
    return x * cos2 + pltpu.roll(x, LANES // 2, 1) * sin2


def _rope64(x, cos4, sin4):
    lane = _lane_iota(x.shape)
    partner = jnp.where(lane % IDX_DIM < IDX_DIM // 2,
                        pltpu.roll(x, LANES - IDX_DIM // 2, 1), pltpu.roll(x, IDX_DIM // 2, 1))
    return x * cos4 + partner * sin4


def _dsa_prep_kernel(p_ref, t_ref, c2_ref, s2_ref, c4_ref, s4_ref,
                     q_ref, k_ref, v_ref, qi_ref, ki_ref):
    c2, s2, c4, s4 = c2_ref[...], s2_ref[...], c4_ref[...], s4_ref[...]
    qscale = LOG2E * DSA_HEAD_DIM ** -0.5
    for j in range(DSA_Q // LANES):
        x = p_ref[0, :, j * LANES:(j + 1) * LANES].astype(f32)
        q_ref[0, :, j * LANES:(j + 1) * LANES] = (_rope128(x, c2, s2) * qscale).astype(bf16)
    for j in range(DSA_KV // LANES):
        o = DSA_Q + j * LANES
        k_ref[0, :, j * LANES:(j + 1) * LANES] = _rope128(p_ref[0, :, o:o + LANES].astype(f32), c2, s2).astype(bf16)
        o = DSA_Q + DSA_KV + j * LANES
        v_ref[0, :, j * LANES:(j + 1) * LANES] = p_ref[0, :, o:o + LANES].astype(bf16)
    lane = _lane_iota((p_ref.shape[1], LANES))
    low = lane < IDX_DIM
    iscale = IDX_DIM ** -0.5
    for j in range(DSA_QI // LANES):
        o = DSA_Q + 2 * DSA_KV + j * LANES
        r = _rope64(p_ref[0, :, o:o + LANES].astype(f32), c4, s4) * iscale
        qi_ref[0, 2 * j] = jnp.where(low, r, 0.0).astype(bf16)
        qi_ref[0, 2 * j + 1] = jnp.where(low, pltpu.roll(r, LANES // 2, 1), 0.0).astype(bf16)
    ki_ref[0] = jnp.where(low, _rope64(t_ref[0], c4, s4), 0.0).astype(bf16)


def _dsa_prep(main, tail, batch, seq, ts=256):
    ts = min(ts, seq)
    c, s = _rope_tables(seq, DSA_HEAD_DIM // 2)
    c2 = jnp.asarray(np.concatenate([c, c], axis=1), f32)
    s2 = jnp.asarray(np.concatenate([-s, s], axis=1), f32)
    c, s = _rope_tables(seq, IDX_DIM // 2)
    c4 = jnp.asarray(np.concatenate([c, c, c, c], axis=1), f32)
    s4 = jnp.asarray(np.concatenate([-s, s, -s, s], axis=1), f32)
    tab = pl.BlockSpec((ts, LANES), lambda b, i: (i, 0))
    out = lambda w: pl.BlockSpec((1, ts, w), lambda b, i: (b, i, 0))
    return pl.pallas_call(
        _dsa_prep_kernel,
        grid=(batch, seq // ts),
        in_specs=[out(DSA_MAIN), out(LANES), tab, tab, tab, tab],
        out_specs=[out(DSA_Q), out(DSA_KV), out(DSA_KV),
                   pl.BlockSpec((1, IDX_HEADS, ts, LANES), lambda b, i: (b, 0, i, 0)), out(LANES)],
        out_shape=[jax.ShapeDtypeStruct((batch, seq, DSA_Q), bf16),
                   jax.ShapeDtypeStruct((batch, seq, DSA_KV), bf16),
                   jax.ShapeDtypeStruct((batch, seq, DSA_KV), bf16),
                   jax.ShapeDtypeStruct((batch, IDX_HEADS, seq, LANES), bf16),
                   jax.ShapeDtypeStruct((batch, seq, LANES), bf16)],
        compiler_params=_params(("parallel", "parallel")),
        name="dsa_prep",
    )(main, tail, c2, s2, c4, s4)


SEL_ROWS = 256
SEL_CHUNK = 512
SEL_SUB = 256


def _order_key(x):
    bits = pltpu.bitcast(x, jnp.int32)
    return bits ^ ((bits >> 31) & 0x7FFFFFFF)


def _select_kernel(qi_ref, ki_ref, t_ref, bias_ref, key_ref, wb_ref, hi_ref, lo_ref, *, n_sel, n_chunks):
    tq, kc = SEL_ROWS, SEL_CHUNK
    n_sub = kc // SEL_SUB
    qt = pl.program_id(1)
    q0 = qt * tq
    n_act = lax.div(qt, kc // tq) + 1
    qpos = q0 + lax.broadcasted_iota(jnp.int32, (tq, 1), 0)
    wscale = IDX_HEADS ** -0.5

    for h in range(IDX_HEADS):
        wb_ref[h] = jnp.broadcast_to(t_ref[0, :, IDX_DIM + h:IDX_DIM + h + 1] * wscale, (tq, LANES))
    q_all = qi_ref[0].reshape(IDX_HEADS * tq, LANES)

    def score_chunk(c, carry):
        off = pl.multiple_of(c * kc, kc)
        for s in range(kc // SEL_SUB):
            ki = ki_ref[0, pl.ds(off + s * SEL_SUB, SEL_SUB), :]
            r = lax.dot_general(q_all, ki, NT_DIMS, preferred_element_type=f32)
            acc = jnp.zeros((tq, SEL_SUB), f32)
            for h in range(IDX_HEADS):
                w = jnp.concatenate([wb_ref[h]] * (SEL_SUB // LANES), axis=1)
                acc = acc + jnp.maximum(r[h * tq:(h + 1) * tq], 0.0) * w
            kpos = off + s * SEL_SUB + _lane_iota((tq, SEL_SUB))
            sc = jnp.where(kpos <= qpos, acc + 0.0, -jnp.inf)
            key_ref[c, :, s * SEL_SUB:(s + 1) * SEL_SUB] = _order_key(sc)
            kt = _order_key(sc.T)
            hi_ref[c * n_sub + s] = (kt >> 16).astype(jnp.int16)
            lo_ref[c * n_sub + s] = ((kt & 0xFFFF) + I16_MIN).astype(jnp.int16)
        return carry

    lax.fori_loop(0, n_act, score_chunk, 0)
    n_blk = n_act * n_sub

    def count_ge16(ref, cand):
        cand = cand.astype(jnp.int16)

        def body(c, acc):
            for s in range(n_sub):
                hit = jnp.where(ref[c * n_sub + s] >= cand, jnp.int16(1), jnp.int16(0))
                for r in range(SEL_SUB // PACKED_ROWS):
                    acc = acc + hit[r * PACKED_ROWS:(r + 1) * PACKED_ROWS, :]
            return acc
        acc = lax.fori_loop(0, n_act, body, jnp.zeros((PACKED_ROWS, tq), jnp.int16))
        return jnp.sum(acc.astype(f32), axis=0, keepdims=True)

    def rank_select16(ref, rank):
        ans = jnp.where(count_ge16(ref, jnp.zeros((1, tq), jnp.int32)) >= rank, 0, I16_MIN).astype(jnp.int32)

        def bit_step(i, ans):
            cand = ans | jnp.left_shift(jnp.int32(1), 14 - i)
            return jnp.where(count_ge16(ref, cand) >= rank, cand, ans)

        return lax.fori_loop(0, 15, bit_step, ans)

    def count_gt16(ref, val):
        return jnp.where(val == I16_MAX, 0.0, count_ge16(ref, jnp.minimum(val + 1, I16_MAX)))

    kf = float(n_sel)
    top = rank_select16(hi_ref, kf)
    above = count_gt16(hi_ref, top)
    top16 = top.astype(jnp.int16)

    def low_block(j, carry):
        lo_ref[j] = jnp.where(hi_ref[j] == top16, lo_ref[j], jnp.int16(I16_MIN))
        return carry

    lax.fori_loop(0, n_blk, low_block, 0)
    bottom = rank_select16(lo_ref, kf - above)
    need_row = kf - above - count_gt16(lo_ref, bottom)

    rows = jnp.concatenate([top.astype(f32), bottom.astype(f32), need_row,
                            jnp.zeros((LANES - 3, tq), f32)], axis=0).T
    ans = rows[:, 0:1].astype(jnp.int32) * 65536 + (rows[:, 1:2].astype(jnp.int32) - I16_MIN)
    need = rows[:, 2:3]
    ans = jnp.where(qpos < n_sel, INT_MIN, ans)

    upper = (lax.broadcasted_iota(jnp.int32, (SEL_SUB, SEL_SUB), 0)
             < lax.broadcasted_iota(jnp.int32, (SEL_SUB, SEL_SUB), 1)).astype(bf16)

    def emit_chunk(c, seen):
        off = pl.multiple_of(c * kc, kc)
        for s in range(kc // SEL_SUB):
            key = key_ref[c, :, s * SEL_SUB:(s + 1) * SEL_SUB]
            eq = key == ans
            eqf = jnp.where(eq, 1.0, 0.0)
            before = seen + jnp.dot(eqf.astype(bf16), upper, preferred_element_type=f32)
            keep = jnp.logical_or(key > ans, jnp.logical_and(eq, before < need))
            kpos = off + s * SEL_SUB + _lane_iota((tq, SEL_SUB))
            keep = jnp.logical_and(keep, kpos <= qpos)
            bias_ref[0, c, :, s * SEL_SUB:(s + 1) * SEL_SUB] = jnp.where(keep, 0.0, NEG_BIG).astype(bf16)
            seen = seen + jnp.sum(eqf, axis=1, keepdims=True)
        return seen

    lax.fori_loop(0, n_act, emit_chunk, jnp.zeros((tq, 1), f32))

    def fill_chunk(c, carry):
        bias_ref[0, c] = jnp.full((tq, kc), NEG_BIG, bf16)
        return carry

    lax.fori_loop(n_act, n_chunks, fill_chunk, 0)


def _select(qi, ki, tail, batch, seq, n_sel):
    n_chunks = seq // SEL_CHUNK
    assert SEL_CHUNK >= n_sel and SEL_CHUNK % SEL_ROWS == 0
    assert seq <= I16_MAX
    return pl.pallas_call(
        functools.partial(_select_kernel, n_sel=n_sel, n_chunks=n_chunks),
        grid=(batch, seq // SEL_ROWS),
        in_specs=[pl.BlockSpec((1, IDX_HEADS, SEL_ROWS, LANES), lambda b, i: (b, 0, i, 0)),
                  pl.BlockSpec((1, seq, LANES), lambda b, i: (b, 0, 0)),
                  pl.BlockSpec((1, SEL_ROWS, LANES), lambda b, i: (b, i, 0))],
        out_specs=pl.BlockSpec((1, n_chunks, SEL_ROWS, SEL_CHUNK), lambda b, i: (b, 0, i, 0)),
        out_shape=jax.ShapeDtypeStruct((batch, n_chunks, seq, SEL_CHUNK), bf16),
        scratch_shapes=[pltpu.VMEM((n_chunks, SEL_ROWS, SEL_CHUNK), jnp.int32),
                        pltpu.VMEM((IDX_HEADS, SEL_ROWS, LANES), f32),
                        pltpu.VMEM((seq // SEL_SUB, SEL_SUB, SEL_ROWS), jnp.int16),
                        pltpu.VMEM((seq // SEL_SUB, SEL_SUB, SEL_ROWS), jnp.int16)],
        compiler_params=_params(("parallel", "parallel")),
        name="dsa_select",
    )(qi, ki, tail)


ATT_TILE = SEL_CHUNK
ATT_GROUP = DSA_HEADS // DSA_KV_HEADS


def _attention_kernel(qt_ref, kt_ref, q_ref, k_ref, v_ref, b_ref, o_ref, m_ref, acc_ref):
    p_id = pl.program_id(2)
    i = qt_ref[p_id]
    j = kt_ref[p_id]
    dh = DSA_HEAD_DIM
    t = ATT_TILE

    @pl.when(j == 0)
    def _():
        m_ref[...] = jnp.full(m_ref.shape, NEG_BIG, f32)
        acc_ref[...] = jnp.zeros_like(acc_ref)

    k = k_ref[0]
    v_ones = jnp.concatenate([v_ref[0], jnp.ones((t, dh), bf16)], axis=1)
    for h in range(ATT_GROUP):
        s = lax.dot_general(q_ref[0, :, h * dh:(h + 1) * dh], k, NT_DIMS, preferred_element_type=f32)
        s = s + b_ref[0, 0].astype(f32)
        m_old = m_ref[h]
        m_new = jnp.maximum(m_old, jnp.max(s, axis=1, keepdims=True))
        p = jnp.concatenate([jnp.exp2(s[:, c * LANES:(c + 1) * LANES] - m_new) for c in range(t // LANES)],
                            axis=1)
        scale = jnp.exp2(m_old - m_new)
        pv = jnp.dot(p.astype(bf16), v_ones, preferred_element_type=f32)
        acc_ref[h] = jnp.concatenate([scale, scale], axis=1) * acc_ref[h] + pv
        m_ref[h] = m_new

    @pl.when(j == i)
    def _():
        for h in range(ATT_GROUP):
            acc = acc_ref[h]
            o_ref[0, :, h * dh:(h + 1) * dh] = (acc[:, :dh] / acc[:, dh:]).astype(o_ref.dtype)


def _attention(q, k, v, bias, batch, seq):
    t = ATT_TILE
    n = seq // t
    dh = DSA_HEAD_DIM
    gw = ATT_GROUP * dh
    pairs = [(i, j) for i in range(n) for j in range(i + 1)]
    q_tile = jnp.asarray([p[0] for p in pairs], jnp.int32)
    k_tile = jnp.asarray([p[1] for p in pairs], jnp.int32)
    grid_spec = pltpu.PrefetchScalarGridSpec(
        num_scalar_prefetch=2,
        grid=(batch, DSA_KV_HEADS, len(pairs)),
        in_specs=[pl.BlockSpec((1, t, gw), lambda b, g, p, qt, kt: (b, qt[p], g)),
                  pl.BlockSpec((1, t, dh), lambda b, g, p, qt, kt: (b, kt[p], g)),
                  pl.BlockSpec((1, t, dh), lambda b, g, p, qt, kt: (b, kt[p], g)),
                  pl.BlockSpec((1, 1, t, t), lambda b, g, p, qt, kt: (b, kt[p], qt[p], 0))],
        out_specs=pl.BlockSpec((1, t, gw), lambda b, g, p, qt, kt: (b, qt[p], g)),
        scratch_shapes=[pltpu.VMEM((ATT_GROUP, t, LANES), f32), pltpu.VMEM((ATT_GROUP, t, 2 * dh), f32)],
    )
    return pl.pallas_call(
        _attention_kernel,
        grid_spec=grid_spec,
        out_shape=jax.ShapeDtypeStruct((batch, seq, DSA_Q), bf16),
        compiler_params=_params(("parallel", "parallel", "arbitrary")),
        name="dsa_attention",
    )(q_tile, k_tile, q, k, v, bias)


def _even_mixer(hb, w_in, gn_g, conv_w, conv_b, conv_ln_g, conv_ln_b, batch, seq):
    proj = _matmul(hb, w_in, bf16).reshape(batch, seq, -1)
    ret = _retention(proj, gn_g, batch, seq)
    conv = _conformer_conv(proj, conv_w, conv_b, conv_ln_g, conv_ln_b, batch, seq)
    return jnp.concatenate([ret, conv], axis=-1).reshape(batch * seq, -1)


def _odd_mixer(hb, w_in, batch, seq):
    wt = w_in.T
    main = _matmul_nt(hb, wt, bf16, 0, DSA_MAIN).reshape(batch, seq, DSA_MAIN)
    tail = _matmul_nt(hb, wt, f32, DSA_MAIN, LANES).reshape(batch, seq, LANES)
    q, k, v, qi, ki = _dsa_prep(main, tail, batch, seq)
    bias = _select(qi, ki, tail, batch, seq, min(TOPK_MAX, seq // 4))
    return _attention(q, k, v, bias, batch, seq).reshape(batch * seq, DSA_Q)


def kernel(x, even_w_in, even_ret_gn_g, even_conv_w, even_conv_b, even_conv_ln_g, even_conv_ln_b, even_w_out,
           odd_w_in, odd_w_out, mix_ln_g, mix_ln_b, moe_w_gu, moe_w_down, ffn_ln_g, ffn_ln_b, router_w, router_b):
    batch, seq, d = x.shape
    depth = mix_ln_g.shape[0]
    alpha = (2 * depth) ** 0.25
    router_ws = _router_weights(router_w)
    h = x.reshape(batch * seq, d).astype(f32)
    hb = h
    for layer in range(depth):
        i = layer // 2
        if layer % 2 == 0:
            mixed = _even_mixer(hb, even_w_in[i], even_ret_gn_g[i], even_conv_w[i], even_conv_b[i],
                                even_conv_ln_g[i], even_conv_ln_b[i], batch, seq)
            w_out = even_w_out[i]
        else:
            mixed = _odd_mixer(hb, odd_w_in[i], batch, seq)
            w_out = odd_w_out[i]
        hn, hnb, ridx, rwgt = _proj_norm_router(mixed, w_out.astype(bf16), h, mix_ln_g[layer], mix_ln_b[layer],
                                                router_ws, router_b.astype(f32), alpha)
        h, hb = _moe(hn, hnb, ridx, rwgt, moe_w_gu, moe_w_down, layer,
                     ffn_ln_g[layer], ffn_ln_b[layer], alpha)
    return h.reshape(batch, seq, d).astype(x.dtype)
```

```python
import functools

import numpy as np
import jax
import jax.numpy as jnp
from jax import lax
from jax.experimental import pallas as pl
from jax.experimental.pallas import tpu as pltpu

f32 = jnp.float32
bf16 = jnp.bfloat16

ROPE_THETA = 10000.0
LN_EPS = 1e-5
RET_HEADS = 4
RET_HEAD_DIM = 256
RET_WIDTH = RET_HEADS * RET_HEAD_DIM
RET_BLOCK = 256
CONV_CHANNELS = 1024
CONV_TAPS = 31
DSA_HEADS = 16
DSA_HEAD_DIM = 128
DSA_KV_HEADS = 4
IDX_HEADS = 16
IDX_DIM = 64
TOPK_MAX = 256
N_EXPERTS = 16
N_GROUPS = 4
EXPERTS_PER_GROUP = N_EXPERTS // N_GROUPS
EXPERT_FF = 512

LANES = 128
SUBLANES = 8
PACKED_ROWS = 16
VMEM_LIMIT = 56 * 1024 * 1024
NEG_BIG = -1e30
LOG2E = 1.4426950408889634
INT_MIN = -2 ** 31
I16_MIN, I16_MAX = -2 ** 15, 2 ** 15 - 1

NT_DIMS = (((1,), (1,)), ((), ()))
TN_DIMS = (((0,), (0,)), ((), ()))


def _params(sem):
    return pltpu.CompilerParams(dimension_semantics=sem, vmem_limit_bytes=VMEM_LIMIT)


def _layer_norm(z, g, b):
    mu = jnp.mean(z, axis=-1, keepdims=True)
    zc = z - mu
    var = jnp.mean(zc * zc, axis=-1, keepdims=True)
    return zc * lax.rsqrt(var + LN_EPS) * g + b


def _mm_kernel(x_ref, w_ref, o_ref, wb_ref):
    @pl.when(pl.program_id(1) == 0)
    def _():
        wb_ref[...] = w_ref[...].astype(bf16)

    o_ref[...] = jnp.dot(x_ref[...].astype(bf16), wb_ref[...], preferred_element_type=f32).astype(o_ref.dtype)


def _matmul(x, w, out_dtype, n=None, tm=1024, tn=1024):
    m, k = x.shape
    n = w.shape[1] if n is None else n
    tm, tn = min(tm, m), min(tn, n)
    assert m % tm == 0 and n % tn == 0
    return pl.pallas_call(
        _mm_kernel,
        grid=(n // tn, m // tm),
        in_specs=[pl.BlockSpec((tm, k), lambda j, i: (i, 0)),
                  pl.BlockSpec((k, tn), lambda j, i: (0, j))],
        out_specs=pl.BlockSpec((tm, tn), lambda j, i: (i, j)),
        out_shape=jax.ShapeDtypeStruct((m, n), out_dtype),
        scratch_shapes=[pltpu.VMEM((k, tn), bf16)],
        compiler_params=_params(("arbitrary", "arbitrary")),
        name="matmul",
    )(x, w)


def _mm_nt_kernel(x_ref, wt_ref, o_ref, wb_ref, *, valid):
    @pl.when(pl.program_id(1) == 0)
    def _():
        w = wt_ref[...]
        if valid < wt_ref.shape[0]:
            w = jnp.where(lax.broadcasted_iota(jnp.int32, w.shape, 0) < valid, w, 0.0)
        wb_ref[...] = w.astype(bf16)

    o_ref[...] = lax.dot_general(x_ref[...].astype(bf16), wb_ref[...], NT_DIMS,
                                 preferred_element_type=f32).astype(o_ref.dtype)


def _matmul_nt(x, wt, out_dtype, row0, n, tm=1024, tn=1024):
    m, k = x.shape
    tm, tn = min(tm, m), min(tn, n)
    assert m % tm == 0 and n % tn == 0 and row0 % tn == 0
    valid = min(tn, wt.shape[0] - row0 - (n - tn))
    assert valid == tn or n == tn
    return pl.pallas_call(
        functools.partial(_mm_nt_kernel, valid=valid),
        grid=(n // tn, m // tm),
        in_specs=[pl.BlockSpec((tm, k), lambda j, i: (i, 0)),
                  pl.BlockSpec((tn, k), lambda j, i: (row0 // tn + j, 0))],
        out_specs=pl.BlockSpec((tm, tn), lambda j, i: (i, j)),
        out_shape=jax.ShapeDtypeStruct((m, n), out_dtype),
        scratch_shapes=[pltpu.VMEM((tn, k), bf16)],
        compiler_params=_params(("arbitrary", "arbitrary")),
        name="matmul_nt",
    )(x, wt)


def _retention_tables(chunk):
    h = np.arange(RET_HEADS, dtype=np.float64)
    log_g = np.log(1.0 - 2.0 ** (-5.0 - h))
    j = np.arange(chunk, dtype=np.float64)
    diff = j[:, None] - j[None, :]
    decay_in = np.where(diff[None] >= 0, np.exp(np.maximum(diff, 0.0)[None] * log_g[:, None, None]), 0.0)
    xi = np.exp((j[None, :] + 1.0) * log_g[:, None])
    zeta = np.exp((chunk - 1.0 - j[None, :]) * log_g[:, None])
    chunk_decay = np.exp(chunk * log_g)
    xi_b = np.broadcast_to(xi[:, :, None], (RET_HEADS, chunk, RET_HEAD_DIM))
    zeta_b = np.broadcast_to(zeta[:, :, None], (RET_HEADS, chunk, RET_HEAD_DIM))
    return jnp.asarray(decay_in, f32), jnp.asarray(xi_b, f32), jnp.asarray(zeta_b, f32), chunk_decay


def _rope_tables(seq, half):
    inv = ROPE_THETA ** (-np.arange(half, dtype=np.float64) / half)
    ang = np.arange(seq, dtype=np.float64)[:, None] * inv[None, :]
    return np.cos(ang), np.sin(ang)


def _retention_kernel(q_ref, k_ref, v_ref, g_ref, cos_ref, sin_ref, din_ref, xi_ref, zeta_ref,
                      gn_ref, o_ref, state_ref, *, chunk_decay):
    @pl.when(pl.program_id(1) == 0)
    def _():
        state_ref[...] = jnp.zeros_like(state_ref)

    cos = cos_ref[...]
    sin = sin_ref[...]
    dh = RET_HEAD_DIM
    half = dh // 2

    def rope(x):
        x1, x2 = x[:, :half], x[:, half:]
        return jnp.concatenate([x1 * cos - x2 * sin, x2 * cos + x1 * sin], axis=1)

    for h in range(RET_HEADS):
        cols = slice(h * dh, (h + 1) * dh)
        q = rope(q_ref[0, :, cols].astype(f32))
        k = rope(k_ref[0, :, cols].astype(f32)) * (dh ** -0.5)
        qb = q.astype(bf16)
        kb = k.astype(bf16)
        vb = v_ref[0, :, cols].astype(bf16)
        attn = lax.dot_general(qb, kb, NT_DIMS, preferred_element_type=f32) * din_ref[h]
        inner = jnp.dot(attn.astype(bf16), vb, preferred_element_type=f32)
        state = state_ref[h]
        cross = jnp.dot(qb, state.astype(bf16), preferred_element_type=f32) * xi_ref[h]
        kz = (k * zeta_ref[h]).astype(bf16)
        state_ref[h] = state * chunk_decay[h] + lax.dot_general(kz, vb, TN_DIMS, preferred_element_type=f32)
        y = inner + cross
        mu = jnp.mean(y, axis=-1, keepdims=True)
        yc = y - mu
        var = jnp.mean(yc * yc, axis=-1, keepdims=True)
        yn = yc * lax.rsqrt(var + LN_EPS) * gn_ref[:, cols]
        g = g_ref[0, :, cols].astype(f32)
        o_ref[0, :, cols] = (g * jax.nn.sigmoid(g) * yn).astype(o_ref.dtype)


def _retention(proj, gn_g, batch, seq):
    c = min(RET_BLOCK, seq)
    dh = RET_HEAD_DIM
    w = RET_WIDTH
    decay_in, xi_b, zeta_b, chunk_decay = _retention_tables(c)
    cos, sin = _rope_tables(seq, dh // 2)
    col = lambda j: pl.BlockSpec((1, c, w), lambda b, s, j=j: (b, s, j))
    whole = lambda shape: pl.BlockSpec(shape, lambda b, s: (0,) * len(shape))
    return pl.pallas_call(
        functools.partial(_retention_kernel, chunk_decay=[float(x) for x in chunk_decay]),
        grid=(batch, seq // c),
        in_specs=[col(0), col(1), col(2), col(3),
                  pl.BlockSpec((c, dh // 2), lambda b, s: (s, 0)),
                  pl.BlockSpec((c, dh // 2), lambda b, s: (s, 0)),
                  whole((RET_HEADS, c, c)), whole((RET_HEADS, c, dh)), whole((RET_HEADS, c, dh)),
                  whole((1, w))],
        out_specs=pl.BlockSpec((1, c, w), lambda b, s: (b, s, 0)),
        out_shape=jax.ShapeDtypeStruct((batch, seq, w), bf16),
        scratch_shapes=[pltpu.VMEM((RET_HEADS, dh, dh), f32)],
        compiler_params=_params(("parallel", "arbitrary")),
        name="retention",
    )(proj, proj, proj, proj, jnp.asarray(cos, f32), jnp.asarray(sin, f32),
      decay_in, xi_b, zeta_b, gn_g.reshape(1, w))


CONV_ROWS = 128
CONV_HIST = 32
CONV_SUB = 32


def _conv_kernel(ga_ref, gb_ref, w_ref, b_ref, lg_ref, lb_ref, o_ref, ush):
    total = CONV_HIST + CONV_ROWS

    @pl.when(pl.program_id(1) == 0)
    def _():
        ush[0, 0:CONV_HIST, :] = jnp.zeros((CONV_HIST, CONV_CHANNELS), f32)

    ush[0, CONV_HIST:total, :] = ga_ref[0].astype(f32) * jax.nn.sigmoid(gb_ref[0].astype(f32))
    for r in range(1, SUBLANES):
        ush[r, 0:total - SUBLANES, :] = ush[0, r:r + total - SUBLANES, :]
    first = CONV_HIST - (CONV_TAPS - 1)
    for r0 in range(0, CONV_ROWS, CONV_SUB):
        acc = jnp.zeros((CONV_SUB, CONV_CHANNELS), f32)
        for t in range(CONV_TAPS):
            d = first + t
            lo = r0 + d - d % SUBLANES
            acc = acc + w_ref[t:t + 1, :] * ush[d % SUBLANES, lo:lo + CONV_SUB, :]
        y = _layer_norm(acc + b_ref[...], lg_ref[...], lb_ref[...])
        o_ref[0, r0:r0 + CONV_SUB, :] = (y * jax.nn.sigmoid(y)).astype(o_ref.dtype)
    ush[0, 0:CONV_HIST, :] = ush[0, CONV_ROWS:total, :]


def _conformer_conv(proj, conv_w, conv_b, ln_g, ln_b, batch, seq):
    c = CONV_CHANNELS
    a_blk = 4 * RET_WIDTH // c
    row = pl.BlockSpec((1, c), lambda b, s: (0, 0))
    return pl.pallas_call(
        _conv_kernel,
        grid=(batch, seq // CONV_ROWS),
        in_specs=[pl.BlockSpec((1, CONV_ROWS, c), lambda b, s: (b, s, a_blk)),
                  pl.BlockSpec((1, CONV_ROWS, c), lambda b, s: (b, s, a_blk + 1)),
                  pl.BlockSpec((CONV_TAPS, c), lambda b, s: (0, 0)),
                  row, row, row],
        out_specs=pl.BlockSpec((1, CONV_ROWS, c), lambda b, s: (b, s, 0)),
        out_shape=jax.ShapeDtypeStruct((batch, seq, c), bf16),
        scratch_shapes=[pltpu.VMEM((SUBLANES, CONV_HIST + CONV_ROWS, c), f32)],
        compiler_params=_params(("parallel", "arbitrary")),
        name="conformer_conv",
    )(proj, proj, conv_w, conv_b.reshape(1, c), ln_g.reshape(1, c), ln_b.reshape(1, c))


def _route(sel, aff):
    epg = EXPERTS_PER_GROUP
    gscore = []
    for g in range(N_GROUPS):
        v = sel[g * epg:(g + 1) * epg]
        best = None
        for a in range(epg):
            for b in range(a + 1, epg):
                s = v[a] + v[b]
                best = s if best is None else jnp.maximum(best, s)
        gscore.append(best)
    gmax = functools.reduce(jnp.maximum, gscore)
    taken = None
    gsel = []
    for g in range(N_GROUPS):
        hit = gscore[g] == gmax
        if taken is not None:
            hit = jnp.logical_and(hit, jnp.logical_not(taken))
        taken = hit if taken is None else jnp.logical_or(taken, hit)
        gsel.append(hit)

    def pick(rows, j):
        out = rows[(N_GROUPS - 1) * epg + j]
        for g in range(N_GROUPS - 2, -1, -1):
            out = jnp.where(gsel[g], rows[g * epg + j], out)
        return out

    x = [pick(sel, j) for j in range(epg)]
    a = [pick(aff, j) for j in range(epg)]
    base = jnp.zeros_like(x[0], dtype=jnp.int32)
    for g in range(1, N_GROUPS):
        base = jnp.where(gsel[g], g * epg, base)

    def first_max(vals, excluded):
        vmax = functools.reduce(jnp.maximum, vals)
        taken = None
        hits = []
        for j in range(epg):
            hit = vals[j] == vmax
            if excluded is not None:
                hit = jnp.logical_and(hit, jnp.logical_not(excluded[j]))
            if taken is not None:
                hit = jnp.logical_and(hit, jnp.logical_not(taken))
            taken = hit if taken is None else jnp.logical_or(taken, hit)
            hits.append(hit)
        return hits

    h1 = first_max(x, None)
    x2 = [jnp.where(h1[j], -jnp.inf, x[j]) for j in range(epg)]
    h2 = first_max(x2, h1)

    def gather(hits):
        idx = base
        val = jnp.zeros_like(a[0])
        for j in range(epg):
            idx = jnp.where(hits[j], base + j, idx)
            val = jnp.where(hits[j], a[j], val)
        return idx, val

    e0, a0 = gather(h1)
    e1, a1 = gather(h2)
    tot = a0 + a1
    return e0, e1, a0 / tot, a1 / tot


def _router_weights(router_w):
    d, e = router_w.shape
    w = router_w.astype(f32)
    w_hi = w.astype(bf16)
    w_lo = (w - w_hi.astype(f32)).astype(bf16)
    top = jnp.concatenate([w_hi, w_lo, jnp.zeros((d, LANES - 2 * e), bf16)], axis=1)
    bot = jnp.concatenate([w_hi, jnp.zeros((d, LANES - e), bf16)], axis=1)
    return jnp.concatenate([top, bot], axis=0)


def _proj_norm_router_kernel(a_ref, w_ref, h_ref, g_ref, b_ref, rw_ref, rb_ref,
                             hn_ref, hb_ref, ridx_ref, rwgt_ref, *, alpha):
    m = jnp.dot(a_ref[...], w_ref[...], preferred_element_type=f32)
    hn = _layer_norm(alpha * h_ref[...] + m, g_ref[...], b_ref[...])
    hn_ref[...] = hn
    hi = hn.astype(bf16)
    hb_ref[...] = hi
    lo = (hn - hi.astype(f32)).astype(bf16)
    d = hn.shape[1]
    parts = (jnp.dot(hi, rw_ref[0:d, :], preferred_element_type=f32)
             + jnp.dot(lo, rw_ref[d:2 * d, :], preferred_element_type=f32))
    parts = parts.T
    logits = parts[0:N_EXPERTS] + parts[N_EXPERTS:2 * N_EXPERTS]
    aff = jax.nn.sigmoid(logits)
    sel = aff + rb_ref[...]
    e0, e1, w0, w1 = _route([sel[e:e + 1, :] for e in range(N_EXPERTS)],
                            [aff[e:e + 1, :] for e in range(N_EXPERTS)])
    ridx_ref[0:1, :] = e0
    ridx_ref[1:2, :] = e1
    rwgt_ref[0:1, :] = w0
    rwgt_ref[1:2, :] = w1


def _proj_norm_router(a, w, h, ln_g, ln_b, router_ws, router_b, alpha, tm=512):
    t, k = a.shape
    d = w.shape[1]
    tm = min(tm, t)
    row = pl.BlockSpec((1, d), lambda i: (0, 0))
    return pl.pallas_call(
        functools.partial(_proj_norm_router_kernel, alpha=alpha),
        grid=(t // tm,),
        in_specs=[pl.BlockSpec((tm, k), lambda i: (i, 0)),
                  pl.BlockSpec((k, d), lambda i: (0, 0)),
                  pl.BlockSpec((tm, d), lambda i: (i, 0)),
                  row, row,
                  pl.BlockSpec((2 * d, LANES), lambda i: (0, 0)),
                  pl.BlockSpec((N_EXPERTS, 1), lambda i: (0, 0))],
        out_specs=[pl.BlockSpec((tm, d), lambda i: (i, 0)),
                   pl.BlockSpec((tm, d), lambda i: (i, 0)),
                   pl.BlockSpec((2, tm), lambda i: (0, i)),
                   pl.BlockSpec((2, tm), lambda i: (0, i))],
        out_shape=[jax.ShapeDtypeStruct((t, d), f32), jax.ShapeDtypeStruct((t, d), bf16),
                   jax.ShapeDtypeStruct((2, t), jnp.int32), jax.ShapeDtypeStruct((2, t), f32)],
        compiler_params=_params(("parallel",)),
        name="proj_norm_router",
    )(a, w, h, ln_g.reshape(1, d), ln_b.reshape(1, d), router_ws, router_b.reshape(N_EXPERTS, 1))


MOE_TILE = 512


def _experts_kernel(te_ref, tv_ref, x_ref, wgu_ref, wd_ref, *refs, has_prev):
    o_ref, wgu_b, wd_b = refs[1:] if has_prev else refs
    i = pl.program_id(0)
    prev = te_ref[jnp.maximum(i - 1, 0)]

    @pl.when(jnp.logical_or(i == 0, te_ref[i] != prev))
    def _():
        wgu_b[...] = wgu_ref[0].astype(bf16)
        wd_b[...] = wd_ref[0].astype(bf16)

    @pl.when(tv_ref[i] > 0)
    def _():
        hgu = jnp.dot(x_ref[...], wgu_b[...], preferred_element_type=f32)
        a, b = hgu[:, :EXPERT_FF], hgu[:, EXPERT_FF:]
        act = (a * jax.nn.sigmoid(a) * b).astype(bf16)
        o_ref[...] = jnp.dot(act, wd_b[...], preferred_element_type=f32).astype(o_ref.dtype)

    @pl.when(tv_ref[i] == 0)
    def _():
        o_ref[...] = jnp.zeros_like(o_ref)


def _experts(xs, w_gu, w_down, layer, tile_expert, tile_valid, tile0, rows, prev=None):
    d = xs.shape[1]
    ff2 = w_gu.shape[3]
    last_x = xs.shape[0] // MOE_TILE - 1
    in_specs = [pl.BlockSpec((MOE_TILE, d), lambda i, te, tv: (jnp.minimum(i, last_x), 0)),
                pl.BlockSpec((None, 1, d, ff2), lambda i, te, tv: (layer, te[i], 0, 0)),
                pl.BlockSpec((None, 1, ff2 // 2, d), lambda i, te, tv: (layer, te[i], 0, 0))]
    args = [tile_expert, tile_valid, xs, w_gu, w_down]
    if prev is not None:
        in_specs.append(pl.BlockSpec(memory_space=pl.ANY))
        args.append(prev)
    grid_spec = pltpu.PrefetchScalarGridSpec(
        num_scalar_prefetch=2,
        grid=(tile_expert.shape[0],),
        in_specs=in_specs,
        out_specs=pl.BlockSpec((MOE_TILE, d), lambda i, te, tv: (i + tile0, 0)),
        scratch_shapes=[pltpu.VMEM((d, ff2), bf16), pltpu.VMEM((ff2 // 2, d), bf16)],
    )
    return pl.pallas_call(
        functools.partial(_experts_kernel, has_prev=prev is not None),
        grid_spec=grid_spec,
        out_shape=jax.ShapeDtypeStruct((rows, d), bf16),
        input_output_aliases={} if prev is None else {len(args) - 1: 0},
        compiler_params=_params(("arbitrary",)),
        name="experts",
    )(*args)


def _dispatch(ridx):
    t = ridx.shape[1]
    slots = 2 * t
    es = ridx.T.reshape(slots)
    experts = jnp.arange(N_EXPERTS, dtype=jnp.int32)
    onehot = (es[:, None] == experts[None, :]).astype(jnp.int32)
    csum = jnp.cumsum(onehot, axis=0)
    rank = jnp.sum((csum - onehot) * onehot, axis=1)
    counts = csum[-1]
    tiles_per = (counts + MOE_TILE - 1) // MOE_TILE
    tile_end = jnp.cumsum(tiles_per)
    start = (tile_end - tiles_per) * MOE_TILE
    dest = start[es] + rank
    n_tiles = slots // MOE_TILE + N_EXPERTS
    tile_ids = jnp.arange(n_tiles, dtype=jnp.int32)
    tile_valid = (tile_ids < tile_end[-1]).astype(jnp.int32)
    tile_owner = jnp.minimum(jnp.sum((tile_ids[:, None] >= tile_end[None, :]).astype(jnp.int32), axis=1),
                             N_EXPERTS - 1)
    last_used = jnp.max(jnp.where(counts > 0, experts, 0))
    tile_expert = jnp.minimum(tile_owner, last_used)
    rows = n_tiles * MOE_TILE
    order = jnp.argsort(es, stable=True).astype(jnp.int32)
    first_sorted = jnp.cumsum(counts) - counts
    per_row = lambda table: jnp.repeat(table[tile_owner], MOE_TILE)
    row = jnp.arange(rows, dtype=jnp.int32)
    within = row - per_row(start)
    src = jnp.clip(per_row(first_sorted) + within, 0, slots - 1)
    tok = jnp.where(within < per_row(counts), order[src] // 2, row % t)
    return tok, dest, tile_expert, tile_valid


def _combine_norm_kernel(h_ref, y0_ref, y1_ref, w0_ref, w1_ref, g_ref, b_ref, o_ref, ob_ref, *, alpha):
    f = w0_ref[...] * y0_ref[...].astype(f32) + w1_ref[...] * y1_ref[...].astype(f32)
    hn = _layer_norm(alpha * h_ref[...] + f, g_ref[...], b_ref[...])
    o_ref[...] = hn
    ob_ref[...] = hn.astype(bf16)


def _combine_norm(h, y, w0, w1, ln_g, ln_b, alpha, tm=512):
    t, d = h.shape
    tm = min(tm, t)
    nt = t // tm
    blk = pl.BlockSpec((tm, d), lambda i: (i, 0))
    col = pl.BlockSpec((tm, 1), lambda i: (i, 0))
    row = pl.BlockSpec((1, d), lambda i: (0, 0))
    return pl.pallas_call(
        functools.partial(_combine_norm_kernel, alpha=alpha),
        grid=(nt,),
        in_specs=[blk, blk, pl.BlockSpec((tm, d), lambda i: (i + nt, 0)), col, col, row, row],
        out_specs=[blk, blk],
        out_shape=[jax.ShapeDtypeStruct((t, d), f32), jax.ShapeDtypeStruct((t, d), bf16)],
        compiler_params=_params(("parallel",)),
        name="combine_norm",
    )(h, y, y, w0, w1, ln_g.reshape(1, d), ln_b.reshape(1, d))


def _moe(hn, hb, ridx, rwgt, w_gu, w_down, layer, ln_g, ln_b, alpha):
    t, d = hn.shape
    tok, dest, tile_expert, tile_valid = _dispatch(ridx)
    n_tiles = tile_expert.shape[0]
    half = n_tiles // 2
    cut = half * MOE_TILE
    rows = n_tiles * MOE_TILE
    first = jnp.arange(n_tiles) < half
    ys = _experts(hb.at[tok[:cut]].get(mode="promise_in_bounds"), w_gu, w_down, layer,
                  jnp.where(first, tile_expert, tile_expert[half - 1]), jnp.where(first, tile_valid, 0), 0, rows)
    ys = _experts(hb.at[tok[cut:]].get(mode="promise_in_bounds"), w_gu, w_down, layer,
                  tile_expert[half:], tile_valid[half:], half, rows, prev=ys)
    y = ys.at[dest.reshape(t, 2).T.reshape(2 * t)].get(mode="promise_in_bounds")
    return _combine_norm(hn, y, rwgt[0].reshape(t, 1), rwgt[1].reshape(t, 1), ln_g, ln_b, alpha)


DSA_Q = DSA_HEADS * DSA_HEAD_DIM
DSA_KV = DSA_KV_HEADS * DSA_HEAD_DIM
DSA_QI = IDX_HEADS * IDX_DIM
DSA_MAIN = DSA_Q + 2 * DSA_KV + DSA_QI


def _lane_iota(shape):
    return lax.broadcasted_iota(jnp.int32, shape, 1)


def _rope128(x, cos2, sin2):
    return x * cos2 + pltpu.roll(x, LANES // 2, 1) * sin2


def _rope64(x, cos4, sin4):
    lane = _lane_iota(x.shape)
    partner = jnp.where(lane % IDX_DIM < IDX_DIM // 2,
                        pltpu.roll(x, LANES - IDX_DIM // 2, 1), pltpu.roll(x, IDX_DIM // 2, 1))
    return x * cos4 + partner * sin4


def _dsa_prep_kernel(p_ref, t_ref, c2_ref, s2_ref, c4_ref, s4_ref,
                     q_ref, k_ref, v_ref, qi_ref, ki_ref):
    c2, s2, c4, s4 = c2_ref[...], s2_ref[...], c4_ref[...], s4_ref[...]
    qscale = LOG2E * DSA_HEAD_DIM ** -0.5
    for j in range(DSA_Q // LANES):
        x = p_ref[0, :, j * LANES:(j + 1) * LANES].astype(f32)
        q_ref[0, :, j * LANES:(j + 1) * LANES] = (_rope128(x, c2, s2) * qscale).astype(bf16)
    for j in range(DSA_KV // LANES):
        o = DSA_Q + j * LANES
        k_ref[0, :, j * LANES:(j + 1) * LANES] = _rope128(p_ref[0, :, o:o + LANES].astype(f32), c2, s2).astype(bf16)
        o = DSA_Q + DSA_KV + j * LANES
        v_ref[0, :, j * LANES:(j + 1) * LANES] = p_ref[0, :, o:o + LANES].astype(bf16)
    lane = _lane_iota((p_ref.shape[1], LANES))
    low = lane < IDX_DIM
    iscale = IDX_DIM ** -0.5
    for j in range(DSA_QI // LANES):
        o = DSA_Q + 2 * DSA_KV + j * LANES
        r = _rope64(p_ref[0, :, o:o + LANES].astype(f32), c4, s4) * iscale
        qi_ref[0, 2 * j] = jnp.where(low, r, 0.0).astype(bf16)
        qi_ref[0, 2 * j + 1] = jnp.where(low, pltpu.roll(r, LANES // 2, 1), 0.0).astype(bf16)
    ki_ref[0] = jnp.where(low, _rope64(t_ref[0], c4, s4), 0.0).astype(bf16)


def _dsa_prep(main, tail, batch, seq, ts=256):
    ts = min(ts, seq)
    c, s = _rope_tables(seq, DSA_HEAD_DIM // 2)
    c2 = jnp.asarray(np.concatenate([c, c], axis=1), f32)
    s2 = jnp.asarray(np.concatenate([-s, s], axis=1), f32)
    c, s = _rope_tables(seq, IDX_DIM // 2)
    c4 = jnp.asarray(np.concatenate([c, c, c, c], axis=1), f32)
    s4 = jnp.asarray(np.concatenate([-s, s, -s, s], axis=1), f32)
    tab = pl.BlockSpec((ts, LANES), lambda b, i: (i, 0))
    out = lambda w: pl.BlockSpec((1, ts, w), lambda b, i: (b, i, 0))
    return pl.pallas_call(
        _dsa_prep_kernel,
        grid=(batch, seq // ts),
        in_specs=[out(DSA_MAIN), out(LANES), tab, tab, tab, tab],
        out_specs=[out(DSA_Q), out(DSA_KV), out(DSA_KV),
                   pl.BlockSpec((1, IDX_HEADS, ts, LANES), lambda b, i: (b, 0, i, 0)), out(LANES)],
        out_shape=[jax.ShapeDtypeStruct((batch, seq, DSA_Q), bf16),
                   jax.ShapeDtypeStruct((batch, seq, DSA_KV), bf16),
                   jax.ShapeDtypeStruct((batch, seq, DSA_KV), bf16),
                   jax.ShapeDtypeStruct((batch, IDX_HEADS, seq, LANES), bf16),
                   jax.ShapeDtypeStruct((batch, seq, LANES), bf16)],
        compiler_params=_params(("parallel", "parallel")),
        name="dsa_prep",
    )(main, tail, c2, s2, c4, s4)


SEL_ROWS = 256
SEL_CHUNK = 512
SEL_SUB = 256


def _order_key(x):
    bits = pltpu.bitcast(x, jnp.int32)
    return bits ^ ((bits >> 31) & 0x7FFFFFFF)


def _select_kernel(qi_ref, ki_ref, t_ref, bias_ref, key_ref, wb_ref, hi_ref, lo_ref, *, n_sel, n_chunks):
    tq, kc = SEL_ROWS, SEL_CHUNK
    n_sub = kc // SEL_SUB
    qt = pl.program_id(1)
    q0 = qt * tq
    n_act = lax.div(qt, kc // tq) + 1
    qpos = q0 + lax.broadcasted_iota(jnp.int32, (tq, 1), 0)
    wscale = IDX_HEADS ** -0.5

    for h in range(IDX_HEADS):
        wb_ref[h] = jnp.broadcast_to(t_ref[0, :, IDX_DIM + h:IDX_DIM + h + 1] * wscale, (tq, LANES))
    q_all = qi_ref[0].reshape(IDX_HEADS * tq, LANES)

    def score_chunk(c, carry):
        off = pl.multiple_of(c * kc, kc)
        for s in range(kc // SEL_SUB):
            ki = ki_ref[0, pl.ds(off + s * SEL_SUB, SEL_SUB), :]
            r = lax.dot_general(q_all, ki, NT_DIMS, preferred_element_type=f32)
            acc = jnp.zeros((tq, SEL_SUB), f32)
            for h in range(IDX_HEADS):
                w = jnp.concatenate([wb_ref[h]] * (SEL_SUB // LANES), axis=1)
                acc = acc + jnp.maximum(r[h * tq:(h + 1) * tq], 0.0) * w
            kpos = off + s * SEL_SUB + _lane_iota((tq, SEL_SUB))
            sc = jnp.where(kpos <= qpos, acc + 0.0, -jnp.inf)
            key_ref[c, :, s * SEL_SUB:(s + 1) * SEL_SUB] = _order_key(sc)
            kt = _order_key(sc.T)
            hi_ref[c * n_sub + s] = (kt >> 16).astype(jnp.int16)
            lo_ref[c * n_sub + s] = ((kt & 0xFFFF) + I16_MIN).astype(jnp.int16)
        return carry

    lax.fori_loop(0, n_act, score_chunk, 0)
    n_blk = n_act * n_sub

    def count_ge16(ref, cand):
        cand = cand.astype(jnp.int16)

        def body(c, acc):
            for s in range(n_sub):
                hit = jnp.where(ref[c * n_sub + s] >= cand, jnp.int16(1), jnp.int16(0))
                for r in range(SEL_SUB // PACKED_ROWS):
                    acc = acc + hit[r * PACKED_ROWS:(r + 1) * PACKED_ROWS, :]
            return acc
        acc = lax.fori_loop(0, n_act, body, jnp.zeros((PACKED_ROWS, tq), jnp.int16))
        return jnp.sum(acc.astype(f32), axis=0, keepdims=True)

    def rank_select16(ref, rank):
        ans = jnp.where(count_ge16(ref, jnp.zeros((1, tq), jnp.int32)) >= rank, 0, I16_MIN).astype(jnp.int32)

        def bit_step(i, ans):
            cand = ans | jnp.left_shift(jnp.int32(1), 14 - i)
            return jnp.where(count_ge16(ref, cand) >= rank, cand, ans)

        return lax.fori_loop(0, 15, bit_step, ans)

    def count_gt16(ref, val):
        return jnp.where(val == I16_MAX, 0.0, count_ge16(ref, jnp.minimum(val + 1, I16_MAX)))

    kf = float(n_sel)
    top = rank_select16(hi_ref, kf)
    above = count_gt16(hi_ref, top)
    top16 = top.astype(jnp.int16)

    def low_block(j, carry):
        lo_ref[j] = jnp.where(hi_ref[j] == top16, lo_ref[j], jnp.int16(I16_MIN))
        return carry

    lax.fori_loop(0, n_blk, low_block, 0)
    bottom = rank_select16(lo_ref, kf - above)
    need_row = kf - above - count_gt16(lo_ref, bottom)

    rows = jnp.concatenate([top.astype(f32), bottom.astype(f32), need_row,
                            jnp.zeros((LANES - 3, tq), f32)], axis=0).T
    ans = rows[:, 0:1].astype(jnp.int32) * 65536 + (rows[:, 1:2].astype(jnp.int32) - I16_MIN)
    need = rows[:, 2:3]
    ans = jnp.where(qpos < n_sel, INT_MIN, ans)

    upper = (lax.broadcasted_iota(jnp.int32, (SEL_SUB, SEL_SUB), 0)
             < lax.broadcasted_iota(jnp.int32, (SEL_SUB, SEL_SUB), 1)).astype(bf16)

    def emit_chunk(c, seen):
        off = pl.multiple_of(c * kc, kc)
        for s in range(kc // SEL_SUB):
            key = key_ref[c, :, s * SEL_SUB:(s + 1) * SEL_SUB]
            eq = key == ans
            eqf = jnp.where(eq, 1.0, 0.0)
            before = seen + jnp.dot(eqf.astype(bf16), upper, preferred_element_type=f32)
            keep = jnp.logical_or(key > ans, jnp.logical_and(eq, before < need))
            kpos = off + s * SEL_SUB + _lane_iota((tq, SEL_SUB))
            keep = jnp.logical_and(keep, kpos <= qpos)
            bias_ref[0, c, :, s * SEL_SUB:(s + 1) * SEL_SUB] = jnp.where(keep, 0.0, NEG_BIG).astype(bf16)
            seen = seen + jnp.sum(eqf, axis=1, keepdims=True)
        return seen

    lax.fori_loop(0, n_act, emit_chunk, jnp.zeros((tq, 1), f32))

    def fill_chunk(c, carry):
        bias_ref[0, c] = jnp.full((tq, kc), NEG_BIG, bf16)
        return carry

    lax.fori_loop(n_act, n_chunks, fill_chunk, 0)


def _select(qi, ki, tail, batch, seq, n_sel):
    n_chunks = seq // SEL_CHUNK
    assert SEL_CHUNK >= n_sel and SEL_CHUNK % SEL_ROWS == 0
    assert seq <= I16_MAX
    return pl.pallas_call(
        functools.partial(_select_kernel, n_sel=n_sel, n_chunks=n_chunks),
        grid=(batch, seq // SEL_ROWS),
        in_specs=[pl.BlockSpec((1, IDX_HEADS, SEL_ROWS, LANES), lambda b, i: (b, 0, i, 0)),
                  pl.BlockSpec((1, seq, LANES), lambda b, i: (b, 0, 0)),
                  pl.BlockSpec((1, SEL_ROWS, LANES), lambda b, i: (b, i, 0))],
        out_specs=pl.BlockSpec((1, n_chunks, SEL_ROWS, SEL_CHUNK), lambda b, i: (b, 0, i, 0)),
        out_shape=jax.ShapeDtypeStruct((batch, n_chunks, seq, SEL_CHUNK), bf16),
        scratch_shapes=[pltpu.VMEM((n_chunks, SEL_ROWS, SEL_CHUNK), jnp.int32),
                        pltpu.VMEM((IDX_HEADS, SEL_ROWS, LANES), f32),
                        pltpu.VMEM((seq // SEL_SUB, SEL_SUB, SEL_ROWS), jnp.int16),
                        pltpu.VMEM((seq // SEL_SUB, SEL_SUB, SEL_ROWS), jnp.int16)],
        compiler_params=_params(("parallel", "parallel")),
        name="dsa_select",
    )(qi, ki, tail)


ATT_TILE = SEL_CHUNK
ATT_GROUP = DSA_HEADS // DSA_KV_HEADS


def _attention_kernel(qt_ref, kt_ref, q_ref, k_ref, v_ref, b_ref, o_ref, m_ref, acc_ref):
    p_id = pl.program_id(2)
    i = qt_ref[p_id]
    j = kt_ref[p_id]
    dh = DSA_HEAD_DIM
    t = ATT_TILE

    @pl.when(j == 0)
    def _():
        m_ref[...] = jnp.full(m_ref.shape, NEG_BIG, f32)
        acc_ref[...] = jnp.zeros_like(acc_ref)

    k = k_ref[0]
    v_ones = jnp.concatenate([v_ref[0], jnp.ones((t, dh), bf16)], axis=1)
    for h in range(ATT_GROUP):
        s = lax.dot_general(q_ref[0, :, h * dh:(h + 1) * dh], k, NT_DIMS, preferred_element_type=f32)
        s = s + b_ref[0, 0].astype(f32)
        m_old = m_ref[h]
        m_new = jnp.maximum(m_old, jnp.max(s, axis=1, keepdims=True))
        p = jnp.concatenate([jnp.exp2(s[:, c * LANES:(c + 1) * LANES] - m_new) for c in range(t // LANES)],
                            axis=1)
        scale = jnp.exp2(m_old - m_new)
        pv = jnp.dot(p.astype(bf16), v_ones, preferred_element_type=f32)
        acc_ref[h] = jnp.concatenate([scale, scale], axis=1) * acc_ref[h] + pv
        m_ref[h] = m_new

    @pl.when(j == i)
    def _():
        for h in range(ATT_GROUP):
            acc = acc_ref[h]
            o_ref[0, :, h * dh:(h + 1) * dh] = (acc[:, :dh] / acc[:, dh:]).astype(o_ref.dtype)


def _attention(q, k, v, bias, batch, seq):
    t = ATT_TILE
    n = seq // t
    dh = DSA_HEAD_DIM
    gw = ATT_GROUP * dh
    pairs = [(i, j) for i in range(n) for j in range(i + 1)]
    q_tile = jnp.asarray([p[0] for p in pairs], jnp.int32)
    k_tile = jnp.asarray([p[1] for p in pairs], jnp.int32)
    grid_spec = pltpu.PrefetchScalarGridSpec(
        num_scalar_prefetch=2,
        grid=(batch, DSA_KV_HEADS, len(pairs)),
        in_specs=[pl.BlockSpec((1, t, gw), lambda b, g, p, qt, kt: (b, qt[p], g)),
                  pl.BlockSpec((1, t, dh), lambda b, g, p, qt, kt: (b, kt[p], g)),
                  pl.BlockSpec((1, t, dh), lambda b, g, p, qt, kt: (b, kt[p], g)),
                  pl.BlockSpec((1, 1, t, t), lambda b, g, p, qt, kt: (b, kt[p], qt[p], 0))],
        out_specs=pl.BlockSpec((1, t, gw), lambda b, g, p, qt, kt: (b, qt[p], g)),
        scratch_shapes=[pltpu.VMEM((ATT_GROUP, t, LANES), f32), pltpu.VMEM((ATT_GROUP, t, 2 * dh), f32)],
    )
    return pl.pallas_call(
        _attention_kernel,
        grid_spec=grid_spec,
        out_shape=jax.ShapeDtypeStruct((batch, seq, DSA_Q), bf16),
        compiler_params=_params(("parallel", "parallel", "arbitrary")),
        name="dsa_attention",
    )(q_tile, k_tile, q, k, v, bias)


def _even_mixer(hb, w_in, gn_g, conv_w, conv_b, conv_ln_g, conv_ln_b, batch, seq):
    proj = _matmul(hb, w_in, bf16).reshape(batch, seq, -1)
    ret = _retention(proj, gn_g, batch, seq)
    conv = _conformer_conv(proj, conv_w, conv_b, conv_ln_g, conv_ln_b, batch, seq)
    return jnp.concatenate([ret, conv], axis=-1).reshape(batch * seq, -1)


def _odd_mixer(hb, w_in, batch, seq):
    wt = w_in.T
    main = _matmul_nt(hb, wt, bf16, 0, DSA_MAIN).reshape(batch, seq, DSA_MAIN)
    tail = _matmul_nt(hb, wt, f32, DSA_MAIN, LANES).reshape(batch, seq, LANES)
    q, k, v, qi, ki = _dsa_prep(main, tail, batch, seq)
    bias = _select(qi, ki, tail, batch, seq, min(TOPK_MAX, seq // 4))
    return _attention(q, k, v, bias, batch, seq).reshape(batch * seq, DSA_Q)


def kernel(x, even_w_in, even_ret_gn_g, even_conv_w, even_conv_b, even_conv_ln_g, even_conv_ln_b, even_w_out,
           odd_w_in, odd_w_out, mix_ln_g, mix_ln_b, moe_w_gu, moe_w_down, ffn_ln_g, ffn_ln_b, router_w, router_b):
    batch, seq, d = x.shape
    depth = mix_ln_g.shape[0]
    alpha = (2 * depth) ** 0.25
    router_ws = _router_weights(router_w)
    h = x.reshape(batch * seq, d).astype(f32)
    hb = h
    for layer in range(depth):
        i = layer // 2
        if layer % 2 == 0:
            mixed = _even_mixer(hb, even_w_in[i], even_ret_gn_g[i], even_conv_w[i], even_conv_b[i],
                                even_conv_ln_g[i], even_conv_ln_b[i], batch, seq)
            w_out = even_w_out[i]
        else:
            mixed = _odd_mixer(hb, odd_w_in[i], batch, seq)
            w_out = odd_w_out[i]
        hn, hnb, ridx, rwgt = _proj_norm_router(mixed, w_out.astype(bf16), h, mix_ln_g[layer], mix_ln_b[layer],
                                                router_ws, router_b.astype(f32), alpha)
        h, hb = _moe(hn, hnb, ridx, rwgt, moe_w_gu, moe_w_down, layer,
                     ffn_ln_g[layer], ffn_ln_b[layer], alpha)
    return h.reshape(batch, seq, d).astype(x.dtype)
```
